```python
import math
import numpy as np
import jax
import jax.numpy as jnp
from jax import lax

D_MODEL = 1024
BATCH = 8
SEQ = 4096
DEPTH = 1

EPS = 1e-6
NEG_INF = -1e30

N_HEADS = 16
N_KV_GROUPS = 2
HEADS_PER_GROUP = N_HEADS // N_KV_GROUPS
HEAD_DIM = 64
ROT_DIM = HEAD_DIM // 4
ROPE_THETA = 500000.0
CMP_BLOCK = 32
CMP_STRIDE = 16
CMP_HIDDEN = 256
SLC_BLOCK = 64
SLC_TOPK = 16
WINDOW = 512
Q_BLOCK = 64

SSM_INNER = 2 * D_MODEL
SSM_HEAD_DIM = 64
SSM_HEADS = SSM_INNER // SSM_HEAD_DIM
SSM_GROUPS = 4
SSM_STATE = 128
SSM_CONV = 4
SSM_CHUNK = 128
DT_MIN = 1e-3
DT_MAX = 1e-1
DT_PROJ_SCALE = 0.1

D_FF = -(-8 * D_MODEL // (3 * 256)) * 256

Q_W = N_HEADS * HEAD_DIM
KV_W = N_KV_GROUPS * HEAD_DIM
XBC_W = SSM_INNER + 2 * SSM_GROUPS * SSM_STATE
IN_WIDTHS = (Q_W, 6 * KV_W, 3 * N_HEADS, SSM_INNER, XBC_W, SSM_HEADS, 2 * D_MODEL)
IN_TOTAL = sum(IN_WIDTHS)
IN_OFFSETS = tuple(int(o) for o in np.cumsum(IN_WIDTHS)[:-1])

kernel_name = 'hybrid_nsa_mamba2_adaln_block'


def _rms(x):
    xf = x.astype(jnp.float32)
    return xf * lax.rsqrt(jnp.mean(xf * xf, axis=-1, keepdims=True) + EPS)


def rms_norm(x, w):
    return (_rms(x) * w.astype(jnp.float32)).astype(x.dtype)


def partial_rotary(x, pos):
    half = ROT_DIM // 2
    inv_freq = jnp.asarray(np.power(ROPE_THETA, -np.arange(0, ROT_DIM, 2) / ROT_DIM).astype(np.float32))
    ang = pos.astype(jnp.float32)[:, None] * inv_freq[None, :]
    cos = jnp.cos(ang)[:, None, :]
    sin = jnp.sin(ang)[:, None, :]
    x1 = x[..., :half].astype(jnp.float32)
    x2 = x[..., half:ROT_DIM].astype(jnp.float32)
    rot = jnp.concatenate([x1 * cos - x2 * sin, x2 * cos + x1 * sin], axis=-1).astype(x.dtype)
    return jnp.concatenate([rot, x[..., ROT_DIM:]], axis=-1)


def masked_softmax(s, mask):
    p = jax.nn.softmax(jnp.where(mask, s, NEG_INF), axis=-1)
    return jnp.where(mask, p, 0.0)


def compress_blocks(k, pe, w1, w2):
    b, s, g, d = k.shape
    nc = (s - CMP_BLOCK) // CMP_STRIDE + 1
    idx = np.arange(nc)[:, None] * CMP_STRIDE + np.arange(CMP_BLOCK)[None, :]
    blk = k[:, idx] + pe[None, None, :, None, :]
    blk = jnp.swapaxes(blk, 2, 3).reshape(b, nc, g, CMP_BLOCK * d)
    return jax.nn.silu(blk @ w1) @ w2


def selection_block_map(s):
    nc = (s - CMP_BLOCK) // CMP_STRIDE + 1
    nb = s // SLC_BLOCK
    cs = np.arange(nc) * CMP_STRIDE
    bs = np.arange(nb) * SLC_BLOCK
    ov = np.minimum(cs[:, None] + CMP_BLOCK, bs[None, :] + SLC_BLOCK) - np.maximum(cs[:, None], bs[None, :])
    return jnp.asarray((np.clip(ov, 0, None) / CMP_BLOCK).astype(np.float32))


def nsa_attention(q, kc, vc, ks, vs, kw, vw, gates, pe_k, w1_k, w2_k, pe_v, w1_v, w2_v):
    b, s, h, d = q.shape
    g, j = N_KV_GROUPS, HEADS_PER_GROUP
    scale = d ** -0.5
    k_cmp = compress_blocks(kc, pe_k, w1_k, w2_k)
    v_cmp = compress_blocks(vc, pe_v, w1_v, w2_v)
    nc = k_cmp.shape[1]
    cmp_end = jnp.arange(nc) * CMP_STRIDE + CMP_BLOCK - 1
    bmap = selection_block_map(s)
    nb = s // SLC_BLOCK
    n_sel = min(SLC_TOPK, nb)
    k_blk = ks.reshape(b, nb, SLC_BLOCK, g, d).transpose(0, 3, 1, 2, 4)
    v_blk = vs.reshape(b, nb, SLC_BLOCK, g, d).transpose(0, 3, 1, 2, 4)
    kw_pad = jnp.pad(kw, ((0, 0), (WINDOW, 0), (0, 0), (0, 0)))
    vw_pad = jnp.pad(vw, ((0, 0), (WINDOW, 0), (0, 0), (0, 0)))
    b_ix = jnp.arange(b)[:, None, None, None]
    g_ix = jnp.arange(g)[None, :, None, None]
    blk_ids = jnp.arange(nb)

    def query_block(i):
        s0 = i * Q_BLOCK
        t = s0 + jnp.arange(Q_BLOCK)
        qb = lax.dynamic_slice_in_dim(q, s0, Q_BLOCK, axis=1).reshape(b, Q_BLOCK, g, j, d)
        gb = lax.dynamic_slice_in_dim(gates, s0, Q_BLOCK, axis=1).reshape(b, Q_BLOCK, g, j, 3)
        s_c = jnp.einsum('btgjd,bngd->bgjtn', qb, k_cmp, preferred_element_type=jnp.float32) * scale
        p_c = masked_softmax(s_c, cmp_end[None, :] <= t[:, None])
        o_c = jnp.einsum('bgjtn,bngd->btgjd', p_c.astype(v_cmp.dtype), v_cmp)
        imp = jnp.einsum('bgjtn,nm->bgtm', p_c, bmap)
        cur = t // SLC_BLOCK
        valid = blk_ids[None, :] <= cur[:, None]
        forced = (blk_ids[None, :] == 0) | (blk_ids[None, :] == cur[:, None]) | (blk_ids[None, :] == cur[:, None] - 1)
        imp = jnp.where(forced, jnp.inf, jnp.where(valid, imp, -jnp.inf))
        _, sel = lax.top_k(imp, n_sel)
        kg = k_blk[b_ix, g_ix, sel].reshape(b, g, Q_BLOCK, n_sel * SLC_BLOCK, d)
        vg = v_blk[b_ix, g_ix, sel].reshape(b, g, Q_BLOCK, n_sel * SLC_BLOCK, d)
        kpos = (sel[..., None] * SLC_BLOCK + jnp.arange(SLC_BLOCK)).reshape(b, g, Q_BLOCK, n_sel * SLC_BLOCK)
        m_s = (kpos <= t[None, None, :, None])[:, :, None]
        s_s = jnp.einsum('btgjd,bgtkd->bgjtk', qb, kg, preferred_element_type=jnp.float32) * scale
        p_s = masked_softmax(s_s, m_s)
        o_s = jnp.einsum('bgjtk,bgtkd->btgjd', p_s.astype(vg.dtype), vg)
        kwb = lax.dynamic_slice_in_dim(kw_pad, s0, WINDOW + Q_BLOCK, axis=1)
        vwb = lax.dynamic_slice_in_dim(vw_pad, s0, WINDOW + Q_BLOCK, axis=1)
        kpos_w = s0 - WINDOW + jnp.arange(WINDOW + Q_BLOCK)
        diff = t[:, None] - kpos_w[None, :]
        m_w = (diff >= 0) & (diff < WINDOW) & (kpos_w[None, :] >= 0)
        s_w = jnp.einsum('btgjd,bkgd->bgjtk', qb, kwb, preferred_element_type=jnp.float32) * scale
        p_w = masked_softmax(s_w, m_w)
        o_w = jnp.einsum('bgjtk,bkgd->btgjd', p_w.astype(vwb.dtype), vwb)
        o = gb[..., 0:1] * o_c + gb[..., 1:2] * o_s + gb[..., 2:3] * o_w
        return o.reshape(b, Q_BLOCK, h * d)

    out = lax.map(query_block, jnp.arange(s // Q_BLOCK))
    return jnp.swapaxes(out, 0, 1).reshape(b, s, h * d)


def ssd_chunked(x, dt, a, bm, cm):
    b, s, h, p = x.shape
    g, n = bm.shape[-2:]
    j = h // g
    q = SSM_CHUNK
    c = s // q
    x = x.astype(jnp.float32).reshape(b, c, q, g, j, p)
    dt = dt.reshape(b, c, q, g, j)
    bm = bm.astype(jnp.float32).reshape(b, c, q, g, n)
    cm = cm.astype(jnp.float32).reshape(b, c, q, g, n)
    a_cum = jnp.cumsum(jnp.moveaxis(dt * a.reshape(g, j), 2, -1), axis=-1)
    xdt = x * dt[..., None]
    causal = jnp.tril(jnp.ones((q, q), dtype=bool))
    decay_in = jnp.exp(jnp.where(causal, a_cum[..., :, None] - a_cum[..., None, :], -jnp.inf))
    cb = jnp.einsum('bclgn,bcsgn->bcgls', cm, bm)
    y_diag = jnp.einsum('bcgjls,bcsgjp->bclgjp', cb[:, :, :, None] * decay_in, xdt)
    decay_out = jnp.moveaxis(jnp.exp(a_cum[..., -1:] - a_cum), -1, 2)
    states = jnp.einsum('bcsgn,bcsgjp->bcgjpn', bm, xdt * decay_out[..., None])
    chunk_decay = jnp.exp(a_cum[..., -1])

    def step(state, inp):
        st, dec = inp
        return state * dec[..., None, None] + st, state

    h0 = jnp.zeros((b, g, j, p, n), jnp.float32)
    _, prev = lax.scan(step, h0, (jnp.moveaxis(states, 1, 0), jnp.moveaxis(chunk_decay, 1, 0)))
    prev = jnp.moveaxis(prev, 0, 1)
    y_off = jnp.einsum('bclgn,bcgjpn->bclgjp', cm, prev) * jnp.moveaxis(jnp.exp(a_cum), -1, 2)[..., None]
    return (y_diag + y_off).reshape(b, s, h, p)


def mamba2_mixer(z, xbc, dt_raw, conv_w, conv_b, dt_bias, a_log, d_skip, norm_w):
    b, s, ch = xbc.shape
    xbc = lax.conv_general_dilated(xbc, conv_w[:, None, :], window_strides=(1,), padding=((SSM_CONV - 1, 0),),
                                   dimension_numbers=('NWC', 'WIO', 'NWC'), feature_group_count=ch) + conv_b
    xbc = jax.nn.silu(xbc)
    xs, bm, cm = jnp.split(xbc, [SSM_INNER, SSM_INNER + SSM_GROUPS * SSM_STATE], axis=-1)
    xs = xs.reshape(b, s, SSM_HEADS, SSM_HEAD_DIM)
    bm = bm.reshape(b, s, SSM_GROUPS, SSM_STATE)
    cm = cm.reshape(b, s, SSM_GROUPS, SSM_STATE)
    dt = jax.nn.softplus(dt_raw.astype(jnp.float32) + dt_bias.astype(jnp.float32))
    a = -jnp.exp(a_log.astype(jnp.float32))
    y = ssd_chunked(xs, dt, a, bm, cm) + d_skip.astype(jnp.float32)[:, None] * xs.astype(jnp.float32)
    y = y.reshape(b, s, SSM_INNER) * jax.nn.silu(z.astype(jnp.float32))
    y = _rms(y.reshape(b, s, SSM_GROUPS, SSM_INNER // SSM_GROUPS)).reshape(b, s, SSM_INNER)
    return (y * norm_w.astype(jnp.float32)).astype(z.dtype)


def setup_inputs(seed: int = 0) -> dict:
    key = jax.random.key(seed)
    ks = jax.random.split(key, 26)
    L, D = DEPTH, D_MODEL

    def nrm(k, shape, sc):
        return jax.random.normal(k, shape, jnp.float32) * sc

    col_scale = np.concatenate([np.full(w, sc, np.float32) for w, sc in
                                zip(IN_WIDTHS, (1.0, 1.0, 1.0, 1.0, 1.0, DT_PROJ_SCALE, 1.0))]) * (D ** -0.5)
    dt0 = jnp.exp(jax.random.uniform(ks[14], (L, SSM_HEADS), jnp.float32, math.log(DT_MIN), math.log(DT_MAX)))
    return {
        'x': nrm(ks[0], (BATCH, SEQ, D), 1.0),
        'c': nrm(ks[1], (BATCH, D), 1.0),
        'w_ada': nrm(ks[2], (L, D, 6 * D), D ** -0.5),
        'b_ada': nrm(ks[3], (L, 6 * D), 0.02),
        'norm1_w': 1.0 + nrm(ks[4], (L, D), 0.02),
        'w_in': nrm(ks[5], (L, D, IN_TOTAL), 1.0) * jnp.asarray(col_scale),
        'cmp_pe_k': nrm(ks[6], (L, CMP_BLOCK, HEAD_DIM), 0.1),
        'cmp_w1_k': nrm(ks[7], (L, CMP_BLOCK * HEAD_DIM, CMP_HIDDEN), (CMP_BLOCK * HEAD_DIM) ** -0.5),
        'cmp_w2_k': nrm(ks[8], (L, CMP_HIDDEN, HEAD_DIM), CMP_HIDDEN ** -0.5),
        'cmp_pe_v': nrm(ks[9], (L, CMP_BLOCK, HEAD_DIM), 0.1),
        'cmp_w1_v': nrm(ks[10], (L, CMP_BLOCK * HEAD_DIM, CMP_HIDDEN), (CMP_BLOCK * HEAD_DIM) ** -0.5),
        'cmp_w2_v': nrm(ks[11], (L, CMP_HIDDEN, HEAD_DIM), CMP_HIDDEN ** -0.5),
        'conv_w': nrm(ks[12], (L, SSM_CONV, XBC_W), SSM_CONV ** -0.5),
        'conv_b': nrm(ks[13], (L, XBC_W), 0.01),
        'dt_bias': dt0 + jnp.log(-jnp.expm1(-dt0)),
        'a_log': jnp.log(jax.random.uniform(ks[15], (L, SSM_HEADS), jnp.float32, 1.0, 16.0)),
        'd_skip': 1.0 + nrm(ks[16], (L, SSM_HEADS), 0.01),
        'ssm_norm_w': 1.0 + nrm(ks[17], (L, SSM_INNER), 0.02),
        'w_attn_out': nrm(ks[18], (L, Q_W, D), Q_W ** -0.5),
        'w_ssm_out': nrm(ks[19], (L, SSM_INNER, D), SSM_INNER ** -0.5),
        'w_o': nrm(ks[20], (L, D, D), D ** -0.5),
        'norm2_w': 1.0 + nrm(ks[21], (L, D), 0.02),
        'w_gate': nrm(ks[22], (L, D, D_FF), D ** -0.5),
        'w_up': nrm(ks[23], (L, D, D_FF), D ** -0.5),
        'w_down': nrm(ks[24], (L, D_FF, D), D_FF ** -0.5),
        'norm_f_w': 1.0 + nrm(ks[25], (D,), 0.02),
    }


def reference(x, c, w_ada, b_ada, norm1_w, w_in, cmp_pe_k, cmp_w1_k, cmp_w2_k, cmp_pe_v, cmp_w1_v, cmp_w2_v,
              conv_w, conv_b, dt_bias, a_log, d_skip, ssm_norm_w, w_attn_out, w_ssm_out, w_o, norm2_w,
              w_gate, w_up, w_down, norm_f_w):
    b, s, _ = x.shape
    pos = jnp.arange(s)
    for l in range(DEPTH):
        mod = (jax.nn.silu(c) @ w_ada[l] + b_ada[l])[:, None, :]
        sh1, sc1, gt1, sh2, sc2, gt2 = jnp.split(mod, 6, axis=-1)
        h = rms_norm(x, norm1_w[l]) * (1.0 + sc1) + sh1
        proj = h @ w_in[l]
        q, kv, nsa_g, z, xbc, dt_raw, mg = jnp.split(proj, IN_OFFSETS, axis=-1)
        q = partial_rotary(q.reshape(b, s, N_HEADS, HEAD_DIM), pos)
        kv = kv.reshape(b, s, 6, N_KV_GROUPS, HEAD_DIM)
        kc = partial_rotary(kv[:, :, 0], pos)
        ks = partial_rotary(kv[:, :, 2], pos)
        kw = partial_rotary(kv[:, :, 4], pos)
        nsa_gates = jax.nn.sigmoid(nsa_g.reshape(b, s, N_HEADS, 3))
        o_attn = nsa_attention(q, kc, kv[:, :, 1], ks, kv[:, :, 3], kw, kv[:, :, 5], nsa_gates,
                               cmp_pe_k[l], cmp_w1_k[l], cmp_w2_k[l], cmp_pe_v[l], cmp_w1_v[l], cmp_w2_v[l])
        y_attn = o_attn @ w_attn_out[l]
        o_ssm = mamba2_mixer(z, xbc, dt_raw, conv_w[l], conv_b[l], dt_bias[l], a_log[l], d_skip[l], ssm_norm_w[l])
        y_ssm = o_ssm @ w_ssm_out[l]
        g_attn, g_ssm = jnp.split(jax.nn.sigmoid(mg), 2, axis=-1)
        x = x + gt1 * ((g_attn * y_attn + g_ssm * y_ssm) @ w_o[l])
        h = rms_norm(x, norm2_w[l]) * (1.0 + sc2) + sh2
        x = x + gt2 * ((jax.nn.silu(h @ w_gate[l]) * (h @ w_up[l])) @ w_down[l])
    return rms_norm(x, norm_f_w)
```

```python
import functools

import numpy as np
import jax
import jax.numpy as jnp
from jax import lax
from jax.experimental import pallas as pl
from jax.experimental.pallas import tpu as pltpu

F32 = jnp.float32
BF16 = jnp.bfloat16
HIGHEST = lax.Precision.HIGHEST

EPS = 1e-6
NEG = -1e30

N_HEADS = 16
N_GROUPS = 2
HPG = N_HEADS // N_GROUPS
HD = 64
ROT_HALF = 8
ROPE_THETA = 500000.0
CMP_BLOCK = 32
CMP_STRIDE = 16
CMP_HIDDEN = 256
SLC_BLOCK = 64
SLC_TOPK = 16
WINDOW = 512

SSM_HEADS = 32
SSM_HD = 64
SSM_GROUPS = 4
SSM_STATE = 128
SSM_CONV = 4
SSM_CHUNK = 128
SSM_INNER = SSM_HEADS * SSM_HD

TQ = 128
LANES = 128
VMEM_LIMIT = 56 * 1024 * 1024


def _cparams(sem, vmem=VMEM_LIMIT):
    return pltpu.CompilerParams(dimension_semantics=sem, vmem_limit_bytes=vmem)


def _sigmoid(x):
    return 1.0 / (1.0 + jnp.exp(-x))


def _silu(x):
    return x * _sigmoid(x)


def _norm_mod(x, w, sc, sh):
    ms = jnp.mean(x * x, axis=-1, keepdims=True)
    return (x * lax.rsqrt(ms + EPS) * w) * (1.0 + sc) + sh


def _dot(a, b):
    return jnp.dot(a, b, preferred_element_type=F32)


def _dot_nt(a, b):
    return lax.dot_general(a, b, (((1,), (1,)), ((), ())), preferred_element_type=F32)


def _dot_tn(a, b):
    return lax.dot_general(a, b, (((0,), (0,)), ((), ())), preferred_element_type=F32)


def _ada_kernel(c_ref, w_ref, b_ref, o_ref):
    c = c_ref[...]
    o_ref[...] = jnp.dot(_silu(c), w_ref[...], precision=HIGHEST, preferred_element_type=F32) + b_ref[...]


def _ada_mod(c, w, b):
    bsz, d = c.shape
    n = w.shape[1]
    tn = 1024
    return pl.pallas_call(
        _ada_kernel,
        out_shape=jax.ShapeDtypeStruct((bsz, n), F32),
        grid=(n // tn,),
        in_specs=[pl.BlockSpec((bsz, d), lambda j: (0, 0)),
                  pl.BlockSpec((d, tn), lambda j: (0, j)),
                  pl.BlockSpec((1, tn), lambda j: (0, j))],
        out_specs=pl.BlockSpec((bsz, tn), lambda j: (0, j)),
        compiler_params=_cparams(("parallel",)),
        name="ada_mod",
    )(c, w, b.reshape(1, n))


def _inproj_attn_kernel(x_ref, mod_ref, nw_ref, wqT_ref, wkn_ref, wvT_ref, wgT_ref, wdt_ref,
                        cosT_ref, sinT_ref, cosK_ref, sinA_ref, sinB_ref,
                        qT_ref, kn_ref, vT_ref, gT_ref, dt_ref):
    tm = x_ref.shape[0]
    mod = mod_ref[0]
    h = _norm_mod(x_ref[...], nw_ref[...], mod[1:2], mod[0:1]).astype(BF16)

    qT = _dot_nt(wqT_ref[...], h)
    cT = cosT_ref[...]
    sT = sinT_ref[...]
    scale = HD ** -0.5
    for hh in range(N_HEADS):
        r0 = hh * HD
        x1 = qT[r0:r0 + ROT_HALF]
        x2 = qT[r0 + ROT_HALF:r0 + 2 * ROT_HALF]
        blk = jnp.concatenate([x1 * cT - x2 * sT, x2 * cT + x1 * sT, qT[r0 + 2 * ROT_HALF:r0 + HD]], axis=0)
        qT_ref[0, r0:r0 + HD, :] = (blk * scale).astype(BF16)

    kn = _dot(h, wkn_ref[...])
    nrot = 6 * HD
    kk = kn[:, :nrot]
    reps = nrot // LANES
    cK = jnp.concatenate([cosK_ref[...]] * reps, axis=1)
    sA = jnp.concatenate([sinA_ref[...]] * reps, axis=1)
    sB = jnp.concatenate([sinB_ref[...]] * reps, axis=1)
    kk = kk * cK + pltpu.roll(kk, nrot - ROT_HALF, 1) * sA + pltpu.roll(kk, ROT_HALF, 1) * sB
    for i in range(6):
        kn_ref[0, i] = kk[:, i * HD:(i + 1) * HD].astype(BF16)
    for i in range(2):
        kn_ref[0, 6 + i] = kn[:, nrot + i * HD:nrot + (i + 1) * HD].astype(BF16)

    vT = _dot_nt(wvT_ref[...], h).astype(BF16)
    for j in range(tm // LANES):
        vT_ref[0, j] = vT[:, j * LANES:(j + 1) * LANES]

    gT_ref[0] = _sigmoid(_dot_nt(wgT_ref[...], h))
    dt_ref[...] = _dot(h, wdt_ref[...])


def _inproj_attn(x2d, mod, nw, wqT, wkn, wvT, wgT, wdt, tabs, bsz, seq):
    d = x2d.shape[1]
    tm = 512
    tpb = seq // tm
    cosT, sinT, cosK, sinA, sinB = tabs
    full = lambda a: pl.BlockSpec(a.shape, lambda i: (0,) * a.ndim)
    out_shape = (
        jax.ShapeDtypeStruct((bsz, N_HEADS * HD, seq), BF16),
        jax.ShapeDtypeStruct((bsz, 8, seq, HD), BF16),
        jax.ShapeDtypeStruct((bsz, seq // LANES, 4 * HD, LANES), BF16),
        jax.ShapeDtypeStruct((bsz, 3 * N_HEADS, seq), F32),
        jax.ShapeDtypeStruct((bsz * seq, LANES), F32),
    )
    return pl.pallas_call(
        _inproj_attn_kernel,
        out_shape=out_shape,
        grid=(bsz * tpb,),
        in_specs=[pl.BlockSpec((tm, d), lambda i: (i, 0)),
                  pl.BlockSpec((1, 6, d), lambda i: (i // tpb, 0, 0)),
                  full(nw), full(wqT), full(wkn), full(wvT), full(wgT), full(wdt),
                  pl.BlockSpec((ROT_HALF, tm), lambda i: (0, i % tpb)),
                  pl.BlockSpec((ROT_HALF, tm), lambda i: (0, i % tpb)),
                  pl.BlockSpec((tm, LANES), lambda i: (i % tpb, 0)),
                  pl.BlockSpec((tm, LANES), lambda i: (i % tpb, 0)),
                  pl.BlockSpec((tm, LANES), lambda i: (i % tpb, 0))],
        out_specs=(pl.BlockSpec((1, N_HEADS * HD, tm), lambda i: (i // tpb, 0, i % tpb)),
                   pl.BlockSpec((1, 8, tm, HD), lambda i: (i // tpb, 0, i % tpb, 0)),
                   pl.BlockSpec((1, tm // LANES, 4 * HD, LANES), lambda i: (i // tpb, i % tpb, 0, 0)),
                   pl.BlockSpec((1, 3 * N_HEADS, tm), lambda i: (i // tpb, 0, i % tpb)),
                   pl.BlockSpec((tm, LANES), lambda i: (i, 0))),
        compiler_params=_cparams(("parallel",)),
        name="inproj_attn",
    )(x2d, mod, nw, wqT, wkn, wvT, wgT, wdt, cosT, sinT, cosK, sinA, sinB)


def _inproj_big_kernel(x_ref, mod_ref, nw_ref, w_ref, o_ref, h_scr):
    j = pl.program_id(1)

    @pl.when(j == 0)
    def _():
        mod = mod_ref[0]
        h_scr[...] = _norm_mod(x_ref[...], nw_ref[...], mod[1:2], mod[0:1]).astype(BF16)

    acc = _dot(h_scr[...], w_ref[...])
    is_gate = jnp.logical_or(j == 2, j == 3)

    @pl.when(is_gate)
    def _():
        o_ref[...] = _sigmoid(acc).astype(BF16)

    @pl.when(jnp.logical_not(is_gate))
    def _():
        o_ref[...] = acc.astype(BF16)


def _inproj_big(x2d, mod, nw, wbig, bsz, seq):
    d = x2d.shape[1]
    n = wbig.shape[1]
    tm, tn = 1024, 1024
    tpb = seq // tm
    return pl.pallas_call(
        _inproj_big_kernel,
        out_shape=jax.ShapeDtypeStruct((bsz * seq, n), BF16),
        grid=(bsz * tpb, n // tn),
        in_specs=[pl.BlockSpec((tm, d), lambda i, j: (i, 0)),
                  pl.BlockSpec((1, 6, d), lambda i, j: (i // tpb, 0, 0)),
                  pl.BlockSpec((1, d), lambda i, j: (0, 0)),
                  pl.BlockSpec((d, tn), lambda i, j: (0, j))],
        out_specs=pl.BlockSpec((tm, tn), lambda i, j: (i, j)),
        scratch_shapes=[pltpu.VMEM((tm, d), BF16)],
        compiler_params=_cparams(("parallel", "arbitrary")),
        name="inproj_big",
    )(x2d, mod, nw, wbig)


def _compress_kernel(x_ref, pea_ref, peb_ref, w1a_ref, w1b_ref, w2_ref, o_ref, *, transpose_out, nc):
    x = x_ref[0, 0].astype(F32)
    nrow = x.shape[0]
    ya = _dot((x + pea_ref[...]).astype(BF16), w1a_ref[...])
    yb = _dot((x + peb_ref[...]).astype(BF16), w1b_ref[...])
    hp = ya + pltpu.roll(yb, nrow - 1, 0)
    row = lax.broadcasted_iota(jnp.int32, hp.shape, 0)
    hh = jnp.where(row < nc, _silu(hp), 0.0).astype(BF16)
    if transpose_out:
        o_ref[0, 0] = _dot_nt(w2_ref[...], hh).astype(BF16)
    else:
        o_ref[0, 0] = _dot(hh, w2_ref[...]).astype(BF16)


def _compress(kn16, slot0, pe, w1, w2, transpose_out, bsz, seq):
    nrow = seq // CMP_STRIDE
    nc = (seq - CMP_BLOCK) // CMP_STRIDE + 1
    half = CMP_STRIDE * HD
    pea = pe[:CMP_STRIDE].reshape(1, half)
    peb = pe[CMP_STRIDE:].reshape(1, half)
    w1a = w1[:half].astype(BF16)
    w1b = w1[half:].astype(BF16)
    if transpose_out:
        w2p = w2.T.astype(BF16)
        oshape, oblock = (bsz, N_GROUPS, HD, nrow), (1, 1, HD, nrow)
    else:
        w2p = w2.astype(BF16)
        oshape, oblock = (bsz, N_GROUPS, nrow, HD), (1, 1, nrow, HD)
    full = lambda a: pl.BlockSpec(a.shape, lambda b, g: (0,) * a.ndim)
    return pl.pallas_call(
        functools.partial(_compress_kernel, transpose_out=transpose_out, nc=nc),
        out_shape=jax.ShapeDtypeStruct(oshape, BF16),
        grid=(bsz, N_GROUPS),
        in_specs=[pl.BlockSpec((1, 1, nrow, half), lambda b, g: (b, slot0 + g, 0, 0)),
                  full(pea), full(peb), full(w1a), full(w1b), full(w2p)],
        out_specs=pl.BlockSpec(oblock, lambda b, g: (b, g, 0, 0)),
        compiler_params=_cparams(("parallel", "parallel")),
        name="compress_v" if transpose_out else "compress_k",
    )(kn16, pea, peb, w1a, w1b, w2p)


def _nsa_kernel(q_ref, kc_ref, vcT_ref, ks_ref, kw_ref, vsT_ref, vwT_ref, g_ref, bmapT_ref, o_ref,
                v_scr, sel_scr, m_scr, l_scr, acc_scr, *, nb):
    qi = pl.program_id(2)
    t0 = qi * TQ
    wl = HPG * TQ
    q = q_ref[0]
    Q = jnp.concatenate([q[hh * HD:(hh + 1) * HD, :] for hh in range(HPG)], axis=1)
    tok1 = t0 + lax.broadcasted_iota(jnp.int32, (1, TQ), 1)
    tok = jnp.concatenate([tok1] * HPG, axis=1)

    ncp = kc_ref.shape[2]
    sc = _dot(kc_ref[0, 0], Q)
    n_idx = lax.broadcasted_iota(jnp.int32, (ncp, 1), 0)
    mask_c = (n_idx * CMP_STRIDE + (CMP_BLOCK - 1)) <= tok
    sc = jnp.where(mask_c, sc, NEG)
    mc = jnp.max(sc, axis=0, keepdims=True)
    ec = jnp.exp(sc - mc)
    pc = jnp.where(mask_c, ec / jnp.sum(ec, axis=0, keepdims=True), 0.0)
    o_c = _dot(vcT_ref[0, 0], pc.astype(BF16))

    psum = pc[:, 0:TQ]
    for hh in range(1, HPG):
        psum = psum + pc[:, hh * TQ:(hh + 1) * TQ]
    imp = jnp.dot(bmapT_ref[...], psum, precision=HIGHEST, preferred_element_type=F32)
    jb = lax.broadcasted_iota(jnp.int32, (nb, 1), 0)
    cur = jnp.right_shift(tok1, int(np.log2(SLC_BLOCK)))
    forced = (jb == 0) | (jb == cur) | (jb == cur - 1)
    v = jnp.where(forced, jnp.inf, jnp.where(jb <= cur, imp, -jnp.inf))
    v_scr[...] = v

    def rank_body(i, rank):
        vi = v_scr[pl.ds(i, 1), :]
        beats = (vi > v) | ((vi == v) & (i < jb))
        return rank + beats.astype(jnp.int32)

    rank = lax.fori_loop(0, nb, rank_body, jnp.zeros((nb, TQ), jnp.int32))
    self = (rank < min(SLC_TOPK, nb)).astype(F32)
    for j in range(nb):
        sel_scr[j] = jnp.broadcast_to(self[j:j + 1, :], (8, TQ))

    kpos_l = lax.broadcasted_iota(jnp.int32, (TQ, 1), 0)

    def attend(k_ref, vT_ref, n_tiles, mask_fn):
        m_scr[...] = jnp.full((1, wl), NEG, F32)
        l_scr[...] = jnp.zeros((1, wl), F32)
        acc_scr[...] = jnp.zeros((HD, wl), F32)

        def body(i, carry):
            kt = qi - i
            k0 = pl.multiple_of(kt * TQ, TQ)
            k = k_ref[0, 0, pl.ds(k0, TQ), :]
            s = _dot(k, Q)
            mask1 = mask_fn(kt, k0 + kpos_l)
            mask = jnp.concatenate([mask1] * HPG, axis=1)
            s = jnp.where(mask, s, NEG)
            m_old = m_scr[...]
            m_new = jnp.maximum(m_old, jnp.max(s, axis=0, keepdims=True))
            alpha = jnp.exp(m_old - m_new)
            p = jnp.exp(s - m_new)
            l_scr[...] = l_scr[...] * alpha + jnp.sum(p, axis=0, keepdims=True)
            acc_scr[...] = acc_scr[...] * alpha + _dot(vT_ref[0, kt], p.astype(BF16))
            m_scr[...] = m_new
            return carry

        lax.fori_loop(0, n_tiles, body, 0)
        return acc_scr[...] / l_scr[...]

    def slc_mask(kt, kpos):
        s0 = sel_scr[2 * kt]
        s1 = sel_scr[2 * kt + 1]
        rep = SLC_BLOCK // 8
        selm = jnp.concatenate([s0] * rep + [s1] * rep, axis=0)
        return (selm > 0.5) & (kpos <= tok1)

    def win_mask(kt, kpos):
        diff = tok1 - kpos
        return (diff >= 0) & (diff < WINDOW)

    o_s = attend(ks_ref, vsT_ref, qi + 1, slc_mask)
    o_w = attend(kw_ref, vwT_ref, jnp.minimum(qi, WINDOW // TQ) + 1, win_mask)

    gts = g_ref[0, 0]
    for hh in range(HPG):
        sl = slice(hh * TQ, (hh + 1) * TQ)
        o = (gts[hh:hh + 1] * o_c[:, sl] + gts[HPG + hh:HPG + hh + 1] * o_s[:, sl]
             + gts[2 * HPG + hh:2 * HPG + hh + 1] * o_w[:, sl])
        o_ref[0, hh * HD:(hh + 1) * HD, :] = o.astype(BF16)


def _nsa(qT, kc, vcT, kn, vT, gT, bmapT, bsz, seq):
    nq = seq // TQ
    nb = seq // SLC_BLOCK
    ncp = kc.shape[2]
    wl = HPG * TQ
    gT4 = gT.reshape(bsz, N_GROUPS, 3 * HPG, seq)
    return pl.pallas_call(
        functools.partial(_nsa_kernel, nb=nb),
        out_shape=jax.ShapeDtypeStruct((bsz, N_HEADS * HD, seq), BF16),
        grid=(bsz, N_GROUPS, nq),
        in_specs=[pl.BlockSpec((1, HPG * HD, TQ), lambda b, g, i: (b, g, i)),
                  pl.BlockSpec((1, 1, ncp, HD), lambda b, g, i: (b, g, 0, 0)),
                  pl.BlockSpec((1, 1, HD, ncp), lambda b, g, i: (b, g, 0, 0)),
                  pl.BlockSpec((1, 1, seq, HD), lambda b, g, i: (b, 2 + g, 0, 0)),
                  pl.BlockSpec((1, 1, seq, HD), lambda b, g, i: (b, 4 + g, 0, 0)),
                  pl.BlockSpec((1, nq, HD, LANES), lambda b, g, i: (b, 0, g, 0)),
                  pl.BlockSpec((1, nq, HD, LANES), lambda b, g, i: (b, 0, 2 + g, 0)),
                  pl.BlockSpec((1, 1, 3 * HPG, TQ), lambda b, g, i: (b, g, 0, i)),
                  pl.BlockSpec(bmapT.shape, lambda b, g, i: (0, 0))],
        out_specs=pl.BlockSpec((1, HPG * HD, TQ), lambda b, g, i: (b, g, i)),
        scratch_shapes=[pltpu.VMEM((nb, TQ), F32),
                        pltpu.VMEM((nb, 8, TQ), F32),
                        pltpu.VMEM((1, wl), F32),
                        pltpu.VMEM((1, wl), F32),
                        pltpu.VMEM((HD, wl), F32)],
        compiler_params=_cparams(("parallel", "parallel", "arbitrary")),
        name="nsa_attention",
    )(qT, kc, vcT, kn, kn, vT, vT, gT4, bmapT)


def _ssd_kernel(z_ref, xs_ref, bc_ref, dt_ref, cwx_ref, cwb_ref, cbx_ref, cbb_ref, dtb_ref, alog_ref,
                dsk_ref, nw_ref, o_ref, xpx_scr, xpb_scr, st_scr):
    c = pl.program_id(1)
    L = SSM_CHUNK
    pad = 8
    npair = SSM_HEADS // 2
    ppg = npair // SSM_GROUPS

    @pl.when(c == 0)
    def _():
        xpx_scr[0:pad, :] = jnp.zeros((pad, xpx_scr.shape[1]), F32)
        xpb_scr[0:pad, :] = jnp.zeros((pad, xpb_scr.shape[1]), F32)
        st_scr[...] = jnp.zeros(st_scr.shape, F32)

    def conv_silu(x_ref, w_ref, b_ref, scr):
        scr[pad:pad + L, :] = x_ref[...].astype(F32)
        acc = b_ref[...] + w_ref[0:1, :] * scr[pl.ds(pad - (SSM_CONV - 1), L), :]
        for k in range(1, SSM_CONV):
            acc = acc + w_ref[k:k + 1, :] * scr[pl.ds(pad - (SSM_CONV - 1) + k, L), :]
        scr[0:pad, :] = scr[L:L + pad, :]
        return _silu(acc)

    xs = conv_silu(xs_ref, cwx_ref, cbx_ref, xpx_scr)
    bcm = conv_silu(bc_ref, cwb_ref, cbb_ref, xpb_scr).astype(BF16)
    xs_b = xs.astype(BF16)

    dtv = dt_ref[...] + dtb_ref[...]
    dt = jnp.maximum(dtv, 0.0) + jnp.log1p(jnp.exp(-jnp.abs(dtv)))
    a = -jnp.exp(alog_ref[...])
    row = lax.broadcasted_iota(jnp.int32, (L, L), 0)
    col = lax.broadcasted_iota(jnp.int32, (L, L), 1)
    causal = col <= row
    a_cum = jnp.dot(causal.astype(F32), dt * a, precision=HIGHEST, preferred_element_type=F32)
    a_cumT = a_cum.T
    dtT = dt.T
    a_last = a_cum[L - 1:L, :]
    a_lastT = a_cumT[:, L - 1:L]
    wT = dtT * jnp.exp(a_lastT - a_cumT)
    e_cum = jnp.exp(a_cum)
    e_last = jnp.exp(a_last)
    lo = lax.broadcasted_iota(jnp.int32, (1, LANES), 1) < SSM_HD

    ys = []
    for g in range(SSM_GROUPS):
        Bg = bcm[:, g * SSM_STATE:(g + 1) * SSM_STATE]
        Cg = bcm[:, (SSM_GROUPS + g) * SSM_STATE:(SSM_GROUPS + g + 1) * SSM_STATE]
        cb = _dot_nt(Cg, Bg)
        BgT = Bg.astype(F32).T
        for jp in range(ppg):
            pj = g * ppg + jp
            h0, h1 = 2 * pj, 2 * pj + 1
            xp = xs_b[:, pj * LANES:(pj + 1) * LANES]

            def head_mats(h):
                ecol = jnp.broadcast_to(a_cum[:, h:h + 1], (L, L))
                dec = jnp.exp(jnp.where(causal, ecol - a_cumT[h:h + 1, :], NEG))
                gm = (cb * dec * dtT[h:h + 1, :]).astype(BF16)
                bw = (BgT * wT[h:h + 1, :]).astype(BF16)
                return gm, bw

            g0, b0 = head_mats(h0)
            g1, b1 = head_mats(h1)
            y_d = jnp.where(lo, _dot(g0, xp), _dot(g1, xp))
            st_prev = st_scr[pj]
            escale = jnp.where(lo, jnp.broadcast_to(e_cum[:, h0:h0 + 1], (L, LANES)),
                               jnp.broadcast_to(e_cum[:, h1:h1 + 1], (L, LANES)))
            y_o = _dot(Cg, st_prev.astype(BF16)) * escale
            st_new = jnp.where(lo, _dot(b0, xp), _dot(b1, xp))
            cdec = jnp.where(lo, jnp.broadcast_to(e_last[:, h0:h0 + 1], (1, LANES)),
                             jnp.broadcast_to(e_last[:, h1:h1 + 1], (1, LANES)))
            st_scr[pj] = st_prev * cdec + st_new
            sl = slice(pj * LANES, (pj + 1) * LANES)
            ys.append(y_d + y_o + dsk_ref[:, sl] * xs[:, sl])
    y = jnp.concatenate(ys, axis=1)
    y = y * _silu(z_ref[...].astype(F32))
    gw = SSM_INNER // SSM_GROUPS
    for g in range(SSM_GROUPS):
        yg = y[:, g * gw:(g + 1) * gw]
        ms = jnp.mean(yg * yg, axis=-1, keepdims=True)
        o_ref[:, g * gw:(g + 1) * gw] = (yg * lax.rsqrt(ms + EPS) * nw_ref[:, g * gw:(g + 1) * gw]).astype(BF16)


def _ssd(zxm, dt, cwx, cwb, cbx, cbb, dtb, alog, dsk, nw, bsz, seq):
    L = SSM_CHUNK
    nchunk = seq // L
    nbc = 2 * SSM_GROUPS * SSM_STATE
    full = lambda a: pl.BlockSpec(a.shape, lambda b, c: (0,) * a.ndim)
    rowblk = lambda width, idx: pl.BlockSpec((L, width), lambda b, c: (b * nchunk + c, idx))
    return pl.pallas_call(
        _ssd_kernel,
        out_shape=jax.ShapeDtypeStruct((bsz * seq, SSM_INNER), BF16),
        grid=(bsz, nchunk),
        in_specs=[rowblk(SSM_INNER, 0), rowblk(SSM_INNER, 2), rowblk(nbc, 6), rowblk(LANES, 0),
                  full(cwx), full(cwb), full(cbx), full(cbb), full(dtb), full(alog), full(dsk), full(nw)],
        out_specs=pl.BlockSpec((L, SSM_INNER), lambda b, c: (b * nchunk + c, 0)),
        scratch_shapes=[pltpu.VMEM((L + 8, SSM_INNER), F32),
                        pltpu.VMEM((L + 8, nbc), F32),
                        pltpu.VMEM((SSM_HEADS // 2, SSM_STATE, LANES), F32)],
        compiler_params=_cparams(("parallel", "arbitrary")),
        name="ssd",
    )(zxm, zxm, zxm, dt, cwx, cwb, cbx, cbb, dtb, alog, dsk, nw)


def _outproj_kernel(oT_ref, os_ref, ga_ref, gs_ref, x_ref, mod_ref, wa_ref, ws_ref, wo_ref, o_ref):
    ya = _dot_tn(oT_ref[0], wa_ref[...])
    ys = _dot(os_ref[...], ws_ref[...])
    m = ga_ref[...].astype(F32) * ya + gs_ref[...].astype(F32) * ys
    r = _dot(m.astype(BF16), wo_ref[...])
    o_ref[...] = x_ref[...] + mod_ref[0][2:3] * r


def _outproj(oT, ossm, zxm, x2d, mod, wa, ws, wo, bsz, seq):
    d = x2d.shape[1]
    tm = 512
    tpb = seq // tm
    full = lambda a: pl.BlockSpec(a.shape, lambda i: (0,) * a.ndim)
    return pl.pallas_call(
        _outproj_kernel,
        out_shape=jax.ShapeDtypeStruct((bsz * seq, d), F32),
        grid=(bsz * tpb,),
        in_specs=[pl.BlockSpec((1, N_HEADS * HD, tm), lambda i: (i // tpb, 0, i % tpb)),
                  pl.BlockSpec((tm, SSM_INNER), lambda i: (i, 0)),
                  pl.BlockSpec((tm, d), lambda i: (i, 2)),
                  pl.BlockSpec((tm, d), lambda i: (i, 3)),
                  pl.BlockSpec((tm, d), lambda i: (i, 0)),
                  pl.BlockSpec((1, 6, d), lambda i: (i // tpb, 0, 0)),
                  full(wa), full(ws), full(wo)],
        out_specs=pl.BlockSpec((tm, d), lambda i: (i, 0)),
        compiler_params=_cparams(("parallel",)),
        name="outproj",
    )(oT, ossm, zxm, zxm, x2d, mod, wa, ws, wo)


def _ffn_kernel(x_ref, mod_ref, n2_ref, nf_ref, wg_ref, wu_ref, wd_ref, o_ref, *, final_norm):
    x = x_ref[...]
    mod = mod_ref[0]
    h = _norm_mod(x, n2_ref[...], mod[4:5], mod[3:4]).astype(BF16)
    a = _silu(_dot(h, wg_ref[...])) * _dot(h, wu_ref[...])
    r = _dot(a.astype(BF16), wd_ref[...])
    x2 = x + mod[5:6] * r
    if final_norm:
        ms = jnp.mean(x2 * x2, axis=-1, keepdims=True)
        x2 = x2 * lax.rsqrt(ms + EPS) * nf_ref[...]
    o_ref[...] = x2


def _ffn(x1, mod, n2, nf, wg, wu, wd, final_norm, bsz, seq):
    d = x1.shape[1]
    tm = 512
    tpb = seq // tm
    const = lambda a: pl.BlockSpec(a.shape, lambda i: (0,) * a.ndim, pipeline_mode=pl.Buffered(1))
    return pl.pallas_call(
        functools.partial(_ffn_kernel, final_norm=final_norm),
        out_shape=jax.ShapeDtypeStruct((bsz * seq, d), F32),
        grid=(bsz * tpb,),
        in_specs=[pl.BlockSpec((tm, d), lambda i: (i, 0)),
                  pl.BlockSpec((1, 6, d), lambda i: (i // tpb, 0, 0)),
                  const(n2), const(nf), const(wg), const(wu), const(wd)],
        out_specs=pl.BlockSpec((tm, d), lambda i: (i, 0)),
        compiler_params=_cparams(("parallel",)),
        name="ffn",
    )(x1, mod, n2, nf, wg, wu, wd)


def _rotary_tables(seq):
    inv_freq = jnp.asarray(np.power(ROPE_THETA, -np.arange(0, 2 * ROT_HALF, 2) / (2 * ROT_HALF)).astype(np.float32))
    ang = jnp.arange(seq).astype(F32)[:, None] * inv_freq[None, :]
    cos, sin = jnp.cos(ang), jnp.sin(ang)
    ones = jnp.ones((seq, HD - 2 * ROT_HALF), F32)
    zeros8 = jnp.zeros((seq, ROT_HALF), F32)
    zeros = jnp.zeros((seq, HD - 2 * ROT_HALF), F32)
    cosK = jnp.concatenate([cos, cos, ones], axis=1)
    sinA = jnp.concatenate([-sin, zeros8, zeros], axis=1)
    sinB = jnp.concatenate([zeros8, sin, zeros], axis=1)
    rep = LANES // HD
    tile = lambda t: jnp.concatenate([t] * rep, axis=1)
    return cos.T, sin.T, tile(cosK), tile(sinA), tile(sinB)


def _selection_block_map_T(seq):
    nc = (seq - CMP_BLOCK) // CMP_STRIDE + 1
    nb = seq // SLC_BLOCK
    cs = np.arange(nc) * CMP_STRIDE
    bs = np.arange(nb) * SLC_BLOCK
    ov = np.minimum(cs[:, None] + CMP_BLOCK, bs[None, :] + SLC_BLOCK) - np.maximum(cs[:, None], bs[None, :])
    bmap = (np.clip(ov, 0, None) / CMP_BLOCK).astype(np.float32)
    out = np.zeros((nb, seq // CMP_STRIDE), np.float32)
    out[:, :nc] = bmap.T
    return jnp.asarray(out)


def kernel(x, c, w_ada, b_ada, norm1_w, w_in, cmp_pe_k, cmp_w1_k, cmp_w2_k, cmp_pe_v, cmp_w1_v, cmp_w2_v,
           conv_w, conv_b, dt_bias, a_log, d_skip, ssm_norm_w, w_attn_out, w_ssm_out, w_o, norm2_w,
           w_gate, w_up, w_down, norm_f_w):
    bsz, seq, d = x.shape
    depth = w_ada.shape[0]
    q_w = N_HEADS * HD
    kv_w = N_GROUPS * HD
    xbc_w = SSM_INNER + 2 * SSM_GROUPS * SSM_STATE
    offs = np.cumsum([q_w, 6 * kv_w, 3 * N_HEADS, SSM_INNER, xbc_w, SSM_HEADS])
    tabs = _rotary_tables(seq)
    bmapT = _selection_block_map_T(seq)
    x2d = x.reshape(bsz * seq, d)
    for l in range(depth):
        mod = _ada_mod(c, w_ada[l], b_ada[l]).reshape(bsz, 6, d)
        wi = w_in[l]
        wq = wi[:, :offs[0]]
        wkv = wi[:, offs[0]:offs[1]].reshape(d, 6, N_GROUPS * HD)
        wg = wi[:, offs[1]:offs[2]].reshape(d, N_GROUPS, HPG, 3)
        wz = wi[:, offs[2]:offs[3]]
        wxbc = wi[:, offs[3]:offs[4]]
        wdt = wi[:, offs[4]:offs[5]]
        wmg = wi[:, offs[5]:]
        wqT = wq.T.astype(BF16)
        wkn = jnp.concatenate([wkv[:, 0], wkv[:, 2], wkv[:, 4], wkv[:, 1]], axis=1).astype(BF16)
        wvT = jnp.concatenate([wkv[:, 3], wkv[:, 5]], axis=1).T.astype(BF16)
        wgT = jnp.transpose(wg, (1, 3, 2, 0)).reshape(3 * N_HEADS, d).astype(BF16)
        wdtp = jnp.pad(wdt, ((0, 0), (0, LANES - SSM_HEADS))).astype(BF16)
        wbig = jnp.concatenate([wz, wmg, wxbc], axis=1).astype(BF16)
        nw1 = norm1_w[l].reshape(1, d)

        qT, kn, vT, gT, dt = _inproj_attn(x2d, mod, nw1, wqT, wkn, wvT, wgT, wdtp, tabs, bsz, seq)
        zxm = _inproj_big(x2d, mod, nw1, wbig, bsz, seq)

        kn16 = kn.reshape(bsz, 8, seq // CMP_STRIDE, CMP_STRIDE * HD)
        kc = _compress(kn16, 0, cmp_pe_k[l], cmp_w1_k[l], cmp_w2_k[l], False, bsz, seq)
        vcT = _compress(kn16, 6, cmp_pe_v[l], cmp_w1_v[l], cmp_w2_v[l], True, bsz, seq)
        oT = _nsa(qT, kc, vcT, kn, vT, gT, bmapT, bsz, seq)

        cw = conv_w[l]
        cb = conv_b[l].reshape(1, xbc_w)
        pad_h = lambda v: jnp.pad(v.reshape(1, SSM_HEADS), ((0, 0), (0, LANES - SSM_HEADS)))
        dsk = jnp.repeat(d_skip[l], SSM_HD).reshape(1, SSM_INNER)
        ossm = _ssd(zxm, dt, cw[:, :SSM_INNER], cw[:, SSM_INNER:], cb[:, :SSM_INNER], cb[:, SSM_INNER:],
                    pad_h(dt_bias[l]), pad_h(a_log[l]), dsk, ssm_norm_w[l].reshape(1, SSM_INNER), bsz, seq)

        x1 = _outproj(oT, ossm, zxm, x2d, mod, w_attn_out[l].astype(BF16), w_ssm_out[l].astype(BF16),
                      w_o[l].astype(BF16), bsz, seq)
        x2d = _ffn(x1, mod, norm2_w[l].reshape(1, d), norm_f_w.reshape(1, d), w_gate[l].astype(BF16),
                   w_up[l].astype(BF16), w_down[l].astype(BF16), l == depth - 1, bsz, seq)
    return x2d.reshape(bsz, seq, d)
```

```python
import functools

import numpy as np
import jax
import jax.numpy as jnp
from jax import lax
from jax.experimental import pallas as pl
from jax.experimental.pallas import tpu as pltpu

F32 = jnp.float32
BF16 = jnp.bfloat16
HIGHEST = lax.Precision.HIGHEST

EPS = 1e-6
NEG = -1e30

N_HEADS = 16
N_GROUPS = 2
HPG = N_HEADS // N_GROUPS
HD = 64
ROT_HALF = 8
ROPE_THETA = 500000.0
CMP_BLOCK = 32
CMP_STRIDE = 16
CMP_HIDDEN = 256
SLC_BLOCK = 64
SLC_TOPK = 16
WINDOW = 512

SSM_HEADS = 32
SSM_HD = 64
SSM_GROUPS = 4
SSM_STATE = 128
SSM_CONV = 4
SSM_CHUNK = 128
SSM_INNER = SSM_HEADS * SSM_HD

TQ = 128
KS = 512
LANES = 128
VROWS = HD + 16
LOG2E = float(np.log2(np.e))
VMEM_LIMIT = 56 * 1024 * 1024


def _cparams(sem, vmem=VMEM_LIMIT):
    return pltpu.CompilerParams(dimension_semantics=sem, vmem_limit_bytes=vmem)


def _sigmoid(x):
    return 1.0 / (1.0 + jnp.exp(-x))


def _silu(x):
    return x * _sigmoid(x)


def _norm_mod(x, w, sc, sh):
    ms = jnp.mean(x * x, axis=-1, keepdims=True)
    return (x * lax.rsqrt(ms + EPS) * w) * (1.0 + sc) + sh


def _dot(a, b):
    return jnp.dot(a, b, preferred_element_type=F32)


def _dot_nt(a, b):
    return lax.dot_general(a, b, (((1,), (1,)), ((), ())), preferred_element_type=F32)


def _dot_tn(a, b):
    return lax.dot_general(a, b, (((0,), (0,)), ((), ())), preferred_element_type=F32)


def _ada_kernel(c_ref, w_ref, b_ref, o_ref):
    c = c_ref[...]
    o_ref[...] = jnp.dot(_silu(c), w_ref[...], precision=HIGHEST, preferred_element_type=F32) + b_ref[...]


def _ada_mod(c, w, b):
    bsz, d = c.shape
    n = w.shape[1]
    tn = 1024
    return pl.pallas_call(
        _ada_kernel,
        out_shape=jax.ShapeDtypeStruct((bsz, n), F32),
        grid=(n // tn,),
        in_specs=[pl.BlockSpec((bsz, d), lambda j: (0, 0)),
                  pl.BlockSpec((d, tn), lambda j: (0, j)),
                  pl.BlockSpec((1, tn), lambda j: (0, j))],
        out_specs=pl.BlockSpec((bsz, tn), lambda j: (0, j)),
        compiler_params=_cparams(("parallel",)),
        name="ada_mod",
    )(c, w, b.reshape(1, n))


def _inproj_attn_kernel(x_ref, mod_ref, nw_ref, wqT_ref, wkn_ref, wvT_ref, wgT_ref, wdt_ref,
                        cosT_ref, sinT_ref, cosK_ref, sinA_ref, sinB_ref,
                        qT_ref, kn_ref, vT_ref, gT_ref, dt_ref):
    tm = x_ref.shape[0]
    mod = mod_ref[0]
    h = _norm_mod(x_ref[...], nw_ref[...], mod[1:2], mod[0:1]).astype(BF16)

    qT = _dot_nt(wqT_ref[...], h)
    cT = cosT_ref[...]
    sT = sinT_ref[...]
    scale = HD ** -0.5 * LOG2E
    for hh in range(N_HEADS):
        r0 = hh * HD
        x1 = qT[r0:r0 + ROT_HALF]
        x2 = qT[r0 + ROT_HALF:r0 + 2 * ROT_HALF]
        blk = jnp.concatenate([x1 * cT - x2 * sT, x2 * cT + x1 * sT, qT[r0 + 2 * ROT_HALF:r0 + HD]], axis=0)
        qT_ref[0, r0:r0 + HD, :] = (blk * scale).astype(BF16)

    kn = _dot(h, wkn_ref[...])
    nrot = 6 * HD
    kk = kn[:, :nrot]
    reps = nrot // LANES
    cK = jnp.concatenate([cosK_ref[...]] * reps, axis=1)
    sA = jnp.concatenate([sinA_ref[...]] * reps, axis=1)
    sB = jnp.concatenate([sinB_ref[...]] * reps, axis=1)
    kk = kk * cK + pltpu.roll(kk, nrot - ROT_HALF, 1) * sA + pltpu.roll(kk, ROT_HALF, 1) * sB
    for i in range(6):
        kn_ref[0, i] = kk[:, i * HD:(i + 1) * HD].astype(BF16)
    for i in range(2):
        kn_ref[0, 6 + i] = kn[:, nrot + i * HD:nrot + (i + 1) * HD].astype(BF16)

    vT = _dot_nt(wvT_ref[...], h).astype(BF16)
    ones = jnp.ones((VROWS - HD, LANES), BF16)
    for j in range(tm // LANES):
        for r in range(4):
            vT_ref[0, j, r * VROWS:r * VROWS + HD, :] = vT[r * HD:(r + 1) * HD, j * LANES:(j + 1) * LANES]
            vT_ref[0, j, r * VROWS + HD:(r + 1) * VROWS, :] = ones

    gT_ref[0] = _sigmoid(_dot_nt(wgT_ref[...], h))
    dt_ref[...] = _dot(h, wdt_ref[...])


def _inproj_attn(x2d, mod, nw, wqT, wkn, wvT, wgT, wdt, tabs, bsz, seq):
    d = x2d.shape[1]
    tm = 512
    tpb = seq // tm
    cosT, sinT, cosK, sinA, sinB = tabs
    full = lambda a: pl.BlockSpec(a.shape, lambda i: (0,) * a.ndim)
    out_shape = (
        jax.ShapeDtypeStruct((bsz, N_HEADS * HD, seq), BF16),
        jax.ShapeDtypeStruct((bsz, 8, seq, HD), BF16),
        jax.ShapeDtypeStruct((bsz, seq // LANES, 4 * VROWS, LANES), BF16),
        jax.ShapeDtypeStruct((bsz, 3 * N_HEADS, seq), F32),
        jax.ShapeDtypeStruct((bsz * seq, LANES), F32),
    )
    return pl.pallas_call(
        _inproj_attn_kernel,
        out_shape=out_shape,
        grid=(bsz * tpb,),
        in_specs=[pl.BlockSpec((tm, d), lambda i: (i, 0)),
                  pl.BlockSpec((1, 6, d), lambda i: (i // tpb, 0, 0)),
                  full(nw), full(wqT), full(wkn), full(wvT), full(wgT), full(wdt),
                  pl.BlockSpec((ROT_HALF, tm), lambda i: (0, i % tpb)),
                  pl.BlockSpec((ROT_HALF, tm), lambda i: (0, i % tpb)),
                  pl.BlockSpec((tm, LANES), lambda i: (i % tpb, 0)),
                  pl.BlockSpec((tm, LANES), lambda i: (i % tpb, 0)),
                  pl.BlockSpec((tm, LANES), lambda i: (i % tpb, 0))],
        out_specs=(pl.BlockSpec((1, N_HEADS * HD, tm), lambda i: (i // tpb, 0, i % tpb)),
                   pl.BlockSpec((1, 8, tm, HD), lambda i: (i // tpb, 0, i % tpb, 0)),
                   pl.BlockSpec((1, tm // LANES, 4 * VROWS, LANES), lambda i: (i // tpb, i % tpb, 0, 0)),
                   pl.BlockSpec((1, 3 * N_HEADS, tm), lambda i: (i // tpb, 0, i % tpb)),
                   pl.BlockSpec((tm, LANES), lambda i: (i, 0))),
        compiler_params=_cparams(("parallel",)),
        name="inproj_attn",
    )(x2d, mod, nw, wqT, wkn, wvT, wgT, wdt, cosT, sinT, cosK, sinA, sinB)


def _inproj_big_kernel(x_ref, mod_ref, nw_ref, w_ref, o_ref, h_scr):
    j = pl.program_id(1)

    @pl.when(j == 0)
    def _():
        mod = mod_ref[0]
        h_scr[...] = _norm_mod(x_ref[...], nw_ref[...], mod[1:2], mod[0:1]).astype(BF16)

    acc = _dot(h_scr[...], w_ref[...])
    is_gate = jnp.logical_or(j == 2, j == 3)

    @pl.when(is_gate)
    def _():
        o_ref[...] = _sigmoid(acc).astype(BF16)

    @pl.when(jnp.logical_not(is_gate))
    def _():
        o_ref[...] = acc.astype(BF16)


def _inproj_big(x2d, mod, nw, wbig, bsz, seq):
    d = x2d.shape[1]
    n = wbig.shape[1]
    tm, tn = 1024, 1024
    tpb = seq // tm
    return pl.pallas_call(
        _inproj_big_kernel,
        out_shape=jax.ShapeDtypeStruct((bsz * seq, n), BF16),
        grid=(bsz * tpb, n // tn),
        in_specs=[pl.BlockSpec((tm, d), lambda i, j: (i, 0)),
                  pl.BlockSpec((1, 6, d), lambda i, j: (i // tpb, 0, 0)),
                  pl.BlockSpec((1, d), lambda i, j: (0, 0)),
                  pl.BlockSpec((d, tn), lambda i, j: (0, j))],
        out_specs=pl.BlockSpec((tm, tn), lambda i, j: (i, j)),
        scratch_shapes=[pltpu.VMEM((tm, d), BF16)],
        compiler_params=_cparams(("parallel", "arbitrary")),
        name="inproj_big",
    )(x2d, mod, nw, wbig)


def _compress_kernel(x_ref, pea_ref, peb_ref, w1a_ref, w1b_ref, w2_ref, o_ref, *, transpose_out, nc):
    x = x_ref[0, 0].astype(F32)
    nrow = x.shape[0]
    ya = _dot((x + pea_ref[...]).astype(BF16), w1a_ref[...])
    yb = _dot((x + peb_ref[...]).astype(BF16), w1b_ref[...])
    hp = ya + pltpu.roll(yb, nrow - 1, 0)
    row = lax.broadcasted_iota(jnp.int32, hp.shape, 0)
    hh = jnp.where(row < nc, _silu(hp), 0.0).astype(BF16)
    if transpose_out:
        o_ref[0, 0] = _dot_nt(w2_ref[...], hh).astype(BF16)
    else:
        o_ref[0, 0] = _dot(hh, w2_ref[...]).astype(BF16)


def _compress(kn16, slot0, pe, w1, w2, transpose_out, bsz, seq):
    nrow = seq // CMP_STRIDE
    nc = (seq - CMP_BLOCK) // CMP_STRIDE + 1
    half = CMP_STRIDE * HD
    pea = pe[:CMP_STRIDE].reshape(1, half)
    peb = pe[CMP_STRIDE:].reshape(1, half)
    w1a = w1[:half].astype(BF16)
    w1b = w1[half:].astype(BF16)
    if transpose_out:
        w2p = w2.T.astype(BF16)
        oshape, oblock = (bsz, N_GROUPS, HD, nrow), (1, 1, HD, nrow)
    else:
        w2p = w2.astype(BF16)
        oshape, oblock = (bsz, N_GROUPS, nrow, HD), (1, 1, nrow, HD)
    full = lambda a: pl.BlockSpec(a.shape, lambda b, g: (0,) * a.ndim)
    return pl.pallas_call(
        functools.partial(_compress_kernel, transpose_out=transpose_out, nc=nc),
        out_shape=jax.ShapeDtypeStruct(oshape, BF16),
        grid=(bsz, N_GROUPS),
        in_specs=[pl.BlockSpec((1, 1, nrow, half), lambda b, g: (b, slot0 + g, 0, 0)),
                  full(pea), full(peb), full(w1a), full(w1b), full(w2p)],
        out_specs=pl.BlockSpec(oblock, lambda b, g: (b, g, 0, 0)),
        compiler_params=_cparams(("parallel", "parallel")),
        name="compress_v" if transpose_out else "compress_k",
    )(kn16, pea, peb, w1a, w1b, w2p)


def _nsa_kernel(q_ref, kc_ref, vcT_ref, ks_ref, kw_ref, vsT_ref, vwT_ref, g_ref, bmapT_ref, o_ref,
                v_scr, sel_scr, m_scr, acc_scr, *, nb):
    qi = pl.program_id(2)
    t0 = qi * TQ
    wl = HPG * TQ
    q = q_ref[0]
    Q = jnp.concatenate([q[hh * HD:(hh + 1) * HD, :] for hh in range(HPG)], axis=1)
    tok1 = t0 + lax.broadcasted_iota(jnp.int32, (1, TQ), 1)
    tok = jnp.concatenate([tok1] * HPG, axis=1)

    ncp = kc_ref.shape[2]
    sc = _dot(kc_ref[0, 0], Q)
    n_idx = lax.broadcasted_iota(jnp.int32, (ncp, 1), 0)
    mask_c = (n_idx * CMP_STRIDE + (CMP_BLOCK - 1)) <= tok
    sc = jnp.where(mask_c, sc, NEG)
    mc = jnp.max(sc, axis=0, keepdims=True)
    ec = jnp.exp2(sc - mc)
    pc = jnp.where(mask_c, ec / jnp.sum(ec, axis=0, keepdims=True), 0.0)
    o_c = _dot(vcT_ref[0, 0], pc.astype(BF16))

    psum = pc[:, 0:TQ]
    for hh in range(1, HPG):
        psum = psum + pc[:, hh * TQ:(hh + 1) * TQ]
    imp = jnp.dot(bmapT_ref[...], psum, precision=HIGHEST, preferred_element_type=F32)
    jb = lax.broadcasted_iota(jnp.int32, (nb, 1), 0)
    cur = jnp.right_shift(tok1, int(np.log2(SLC_BLOCK)))
    forced = (jb == 0) | (jb == cur) | (jb == cur - 1)
    v = jnp.where(forced, jnp.inf, jnp.where(jb <= cur, imp, -jnp.inf))
    v_scr[...] = v

    jbf = lax.broadcasted_iota(jnp.int32, (nb, TQ), 0)

    def rank_body(i, rank):
        vi = v_scr[pl.ds(i, 1), :]
        ge = jnp.where(vi >= v, 1.0, 0.0)
        gt = jnp.where(vi > v, 1.0, 0.0)
        return rank + jnp.where(jbf > i, ge, gt)

    n_live = jnp.minimum(2 * qi + 2, nb)
    rank = lax.fori_loop(0, n_live, rank_body, jnp.zeros((nb, TQ), F32))
    selb = jnp.where(rank < min(SLC_TOPK, nb), 0.0, NEG)
    for j in range(nb):
        sel_scr[j] = jnp.broadcast_to(selb[j:j + 1, :], (8, TQ))

    def lanes8(x):
        return jnp.concatenate([x] * HPG, axis=1)

    nwt = WINDOW // TQ + 1
    kt0 = jnp.maximum(qi - WINDOW // TQ, 0)
    kw0 = pl.multiple_of(kt0 * TQ, TQ)
    sw = _dot(kw_ref[0, 0, pl.ds(kw0, nwt * TQ), :], Q)
    diff = tok1 - (kw0 + lax.broadcasted_iota(jnp.int32, (nwt * TQ, 1), 0))
    sw = sw + lanes8(jnp.where((diff >= 0) & (diff < WINDOW), 0.0, NEG))
    pw = jnp.exp2(sw - jnp.max(sw, axis=0, keepdims=True)).astype(BF16)
    vwin = jnp.concatenate([vwT_ref[0, kt0 + j] for j in range(nwt)], axis=1)
    ow = _dot(vwin, pw)
    o_w = ow[:HD] / ow[HD:HD + 1]

    tps = KS // TQ
    bps = KS // SLC_BLOCK
    m_scr[...] = jnp.full((1, wl), NEG, F32)
    acc_scr[...] = jnp.zeros((VROWS, wl), F32)
    st_hi = qi // tps
    kpos_l = lax.broadcasted_iota(jnp.int32, (KS, 1), 0)

    def slc_body(i, carry):
        st = st_hi - i
        k0 = pl.multiple_of(st * KS, KS)
        s = _dot(ks_ref[0, 0, pl.ds(k0, KS), :], Q)
        sb = jnp.concatenate([jnp.concatenate([sel_scr[st * bps + j]] * (SLC_BLOCK // 8), axis=0)
                              for j in range(bps)], axis=0)
        s = s + lanes8(jnp.where(k0 + kpos_l <= tok1, sb, NEG))
        m_old = m_scr[...]
        m_new = jnp.maximum(m_old, jnp.max(s, axis=0, keepdims=True))
        p = jnp.exp2(s - m_new).astype(BF16)
        vs = jnp.concatenate([vsT_ref[0, st * tps + j] for j in range(tps)], axis=1)
        acc_scr[...] = acc_scr[...] * jnp.exp2(m_old - m_new) + _dot(vs, p)
        m_scr[...] = m_new
        return carry

    lax.fori_loop(0, st_hi + 1, slc_body, 0)
    o_s = acc_scr[0:HD, :] / acc_scr[HD:HD + 1, :]

    gts = g_ref[0, 0]
    for hh in range(HPG):
        sl = slice(hh * TQ, (hh + 1) * TQ)
        o = (gts[hh:hh + 1] * o_c[:, sl] + gts[HPG + hh:HPG + hh + 1] * o_s[:, sl]
             + gts[2 * HPG + hh:2 * HPG + hh + 1] * o_w[:, sl])
        o_ref[0, hh * HD:(hh + 1) * HD, :] = o.astype(BF16)


def _nsa(qT, kc, vcT, kn, vT, gT, bmapT, bsz, seq):
    nq = seq // TQ
    nb = seq // SLC_BLOCK
    ncp = kc.shape[2]
    wl = HPG * TQ
    gT4 = gT.reshape(bsz, N_GROUPS, 3 * HPG, seq)
    assert seq % KS == 0 and seq >= WINDOW + TQ
    return pl.pallas_call(
        functools.partial(_nsa_kernel, nb=nb),
        out_shape=jax.ShapeDtypeStruct((bsz, N_HEADS * HD, seq), BF16),
        grid=(bsz, N_GROUPS, nq),
        in_specs=[pl.BlockSpec((1, HPG * HD, TQ), lambda b, g, i: (b, g, i)),
                  pl.BlockSpec((1, 1, ncp, HD), lambda b, g, i: (b, g, 0, 0)),
                  pl.BlockSpec((1, 1, HD, ncp), lambda b, g, i: (b, g, 0, 0)),
                  pl.BlockSpec((1, 1, seq, HD), lambda b, g, i: (b, 2 + g, 0, 0)),
                  pl.BlockSpec((1, 1, seq, HD), lambda b, g, i: (b, 4 + g, 0, 0)),
                  pl.BlockSpec((1, nq, VROWS, LANES), lambda b, g, i: (b, 0, g, 0)),
                  pl.BlockSpec((1, nq, VROWS, LANES), lambda b, g, i: (b, 0, 2 + g, 0)),
                  pl.BlockSpec((1, 1, 3 * HPG, TQ), lambda b, g, i: (b, g, 0, i)),
                  pl.BlockSpec(bmapT.shape, lambda b, g, i: (0, 0))],
        out_specs=pl.BlockSpec((1, HPG * HD, TQ), lambda b, g, i: (b, g, i)),
        scratch_shapes=[pltpu.VMEM((nb, TQ), F32),
                        pltpu.VMEM((nb, 8, TQ), F32),
                        pltpu.VMEM((1, wl), F32),
                        pltpu.VMEM((VROWS, wl), F32)],
        compiler_params=_cparams(("parallel", "parallel", "arbitrary")),
        name="nsa_attention",
    )(qT, kc, vcT, kn, kn, vT, vT, gT4, bmapT)


def _ssd_kernel(z_ref, xs_ref, bc_ref, dt_ref, cwx_ref, cwb_ref, cbx_ref, cbb_ref, dtb_ref, alog_ref,
                dsk_ref, nw_ref, o_ref, xpx_scr, xpb_scr, st_scr):
    c = pl.program_id(1)
    L = SSM_CHUNK
    pad = 8
    npair = SSM_HEADS // 2
    ppg = npair // SSM_GROUPS

    @pl.when(c == 0)
    def _():
        xpx_scr[0:pad, :] = jnp.zeros((pad, xpx_scr.shape[1]), F32)
        xpb_scr[0:pad, :] = jnp.zeros((pad, xpb_scr.shape[1]), F32)
        st_scr[...] = jnp.zeros(st_scr.shape, F32)

    def conv_silu(x_ref, w_ref, b_ref, scr):
        scr[pad:pad + L, :] = x_ref[...].astype(F32)
        acc = b_ref[...] + w_ref[0:1, :] * scr[pl.ds(pad - (SSM_CONV - 1), L), :]
        for k in range(1, SSM_CONV):
            acc = acc + w_ref[k:k + 1, :] * scr[pl.ds(pad - (SSM_CONV - 1) + k, L), :]
        scr[0:pad, :] = scr[L:L + pad, :]
        return _silu(acc)

    xs = conv_silu(xs_ref, cwx_ref, cbx_ref, xpx_scr)
    bcm = conv_silu(bc_ref, cwb_ref, cbb_ref, xpb_scr).astype(BF16)
    xs_b = xs.astype(BF16)

    dtv = dt_ref[...] + dtb_ref[...]
    dt = jnp.maximum(dtv, 0.0) + jnp.log1p(jnp.exp(-jnp.abs(dtv)))
    a = -jnp.exp(alog_ref[...])
    row = lax.broadcasted_iota(jnp.int32, (L, L), 0)
    col = lax.broadcasted_iota(jnp.int32, (L, L), 1)
    causal = col <= row
    a_cum = jnp.dot(causal.astype(F32), dt * a, precision=HIGHEST, preferred_element_type=F32)
    a_cumT = a_cum.T
    dtT = dt.T
    a_last = a_cum[L - 1:L, :]
    a_lastT = a_cumT[:, L - 1:L]
    wT = dtT * jnp.exp(a_lastT - a_cumT)
    e_cum = jnp.exp(a_cum)
    e_last = jnp.exp(a_last)
    lo = lax.broadcasted_iota(jnp.int32, (1, LANES), 1) < SSM_HD

    ys = []
    for g in range(SSM_GROUPS):
        Bg = bcm[:, g * SSM_STATE:(g + 1) * SSM_STATE]
        Cg = bcm[:, (SSM_GROUPS + g) * SSM_STATE:(SSM_GROUPS + g + 1) * SSM_STATE]
        cb = _dot_nt(Cg, Bg)
        BgT = Bg.astype(F32).T
        for jp in range(ppg):
            pj = g * ppg + jp
            h0, h1 = 2 * pj, 2 * pj + 1
            xp = xs_b[:, pj * LANES:(pj + 1) * LANES]

            def head_mats(h):
                ecol = jnp.broadcast_to(a_cum[:, h:h + 1], (L, L))
                dec = jnp.exp(jnp.where(causal, ecol - a_cumT[h:h + 1, :], NEG))
                gm = (cb * dec * dtT[h:h + 1, :]).astype(BF16)
                bw = (BgT * wT[h:h + 1, :]).astype(BF16)
                return gm, bw

            g0, b0 = head_mats(h0)
            g1, b1 = head_mats(h1)
            y_d = jnp.where(lo, _dot(g0, xp), _dot(g1, xp))
            st_prev = st_scr[pj]
            escale = jnp.where(lo, jnp.broadcast_to(e_cum[:, h0:h0 + 1], (L, LANES)),
                               jnp.broadcast_to(e_cum[:, h1:h1 + 1], (L, LANES)))
            y_o = _dot(Cg, st_prev.astype(BF16)) * escale
            st_new = jnp.where(lo, _dot(b0, xp), _dot(b1, xp))
            cdec = jnp.where(lo, jnp.broadcast_to(e_last[:, h0:h0 + 1], (1, LANES)),
                             jnp.broadcast_to(e_last[:, h1:h1 + 1], (1, LANES)))
            st_scr[pj] = st_prev * cdec + st_new
            sl = slice(pj * LANES, (pj + 1) * LANES)
            ys.append(y_d + y_o + dsk_ref[:, sl] * xs[:, sl])
    y = jnp.concatenate(ys, axis=1)
    y = y * _silu(z_ref[...].astype(F32))
    gw = SSM_INNER // SSM_GROUPS
    for g in range(SSM_GROUPS):
        yg = y[:, g * gw:(g + 1) * gw]
        ms = jnp.mean(yg * yg, axis=-1, keepdims=True)
        o_ref[:, g * gw:(g + 1) * gw] = (yg * lax.rsqrt(ms + EPS) * nw_ref[:, g * gw:(g + 1) * gw]).astype(BF16)


def _ssd(zxm, dt, cwx, cwb, cbx, cbb, dtb, alog, dsk, nw, bsz, seq):
    L = SSM_CHUNK
    nchunk = seq // L
    nbc = 2 * SSM_GROUPS * SSM_STATE
    full = lambda a: pl.BlockSpec(a.shape, lambda b, c: (0,) * a.ndim)
    rowblk = lambda width, idx: pl.BlockSpec((L, width), lambda b, c: (b * nchunk + c, idx))
    return pl.pallas_call(
        _ssd_kernel,
        out_shape=jax.ShapeDtypeStruct((bsz * seq, SSM_INNER), BF16),
        grid=(bsz, nchunk),
        in_specs=[rowblk(SSM_INNER, 0), rowblk(SSM_INNER, 2), rowblk(nbc, 6), rowblk(LANES, 0),
                  full(cwx), full(cwb), full(cbx), full(cbb), full(dtb), full(alog), full(dsk), full(nw)],
        out_specs=pl.BlockSpec((L, SSM_INNER), lambda b, c: (b * nchunk + c, 0)),
        scratch_shapes=[pltpu.VMEM((L + 8, SSM_INNER), F32),
                        pltpu.VMEM((L + 8, nbc), F32),
                        pltpu.VMEM((SSM_HEADS // 2, SSM_STATE, LANES), F32)],
        compiler_params=_cparams(("parallel", "arbitrary")),
        name="ssd",
    )(zxm, zxm, zxm, dt, cwx, cwb, cbx, cbb, dtb, alog, dsk, nw)


def _outproj_kernel(oT_ref, os_ref, ga_ref, gs_ref, x_ref, mod_ref, wa_ref, ws_ref, wo_ref, o_ref):
    ya = _dot_tn(oT_ref[0], wa_ref[...])
    ys = _dot(os_ref[...], ws_ref[...])
    m = ga_ref[...].astype(F32) * ya + gs_ref[...].astype(F32) * ys
    r = _dot(m.astype(BF16), wo_ref[...])
    o_ref[...] = x_ref[...] + mod_ref[0][2:3] * r


def _outproj(oT, ossm, zxm, x2d, mod, wa, ws, wo, bsz, seq):
    d = x2d.shape[1]
    tm = 512
    tpb = seq // tm
    full = lambda a: pl.BlockSpec(a.shape, lambda i: (0,) * a.ndim)
    return pl.pallas_call(
        _outproj_kernel,
        out_shape=jax.ShapeDtypeStruct((bsz * seq, d), F32),
        grid=(bsz * tpb,),
        in_specs=[pl.BlockSpec((1, N_HEADS * HD, tm), lambda i: (i // tpb, 0, i % tpb)),
                  pl.BlockSpec((tm, SSM_INNER), lambda i: (i, 0)),
                  pl.BlockSpec((tm, d), lambda i: (i, 2)),
                  pl.BlockSpec((tm, d), lambda i: (i, 3)),
                  pl.BlockSpec((tm, d), lambda i: (i, 0)),
                  pl.BlockSpec((1, 6, d), lambda i: (i // tpb, 0, 0)),
                  full(wa), full(ws), full(wo)],
        out_specs=pl.BlockSpec((tm, d), lambda i: (i, 0)),
        compiler_params=_cparams(("parallel",)),
        name="outproj",
    )(oT, ossm, zxm, zxm, x2d, mod, wa, ws, wo)


def _ffn_kernel(x_ref, mod_ref, n2_ref, nf_ref, wg_ref, wu_ref, wd_ref, o_ref, *, final_norm):
    x = x_ref[...]
    mod = mod_ref[0]
    h = _norm_mod(x, n2_ref[...], mod[4:5], mod[3:4]).astype(BF16)
    a = _silu(_dot(h, wg_ref[...])) * _dot(h, wu_ref[...])
    r = _dot(a.astype(BF16), wd_ref[...])
    x2 = x + mod[5:6] * r
    if final_norm:
        ms = jnp.mean(x2 * x2, axis=-1, keepdims=True)
        x2 = x2 * lax.rsqrt(ms + EPS) * nf_ref[...]
    o_ref[...] = x2


def _ffn(x1, mod, n2, nf, wg, wu, wd, final_norm, bsz, seq):
    d = x1.shape[1]
    tm = 512
    tpb = seq // tm
    const = lambda a: pl.BlockSpec(a.shape, lambda i: (0,) * a.ndim, pipeline_mode=pl.Buffered(1))
    return pl.pallas_call(
        functools.partial(_ffn_kernel, final_norm=final_norm),
        out_shape=jax.ShapeDtypeStruct((bsz * seq, d), F32),
        grid=(bsz * tpb,),
        in_specs=[pl.BlockSpec((tm, d), lambda i: (i, 0)),
                  pl.BlockSpec((1, 6, d), lambda i: (i // tpb, 0, 0)),
                  const(n2), const(nf), const(wg), const(wu), const(wd)],
        out_specs=pl.BlockSpec((tm, d), lambda i: (i, 0)),
        compiler_params=_cparams(("parallel",)),
        name="ffn",
    )(x1, mod, n2, nf, wg, wu, wd)


def _rotary_tables(seq):
    inv_freq = jnp.asarray(np.power(ROPE_THETA, -np.arange(0, 2 * ROT_HALF, 2) / (2 * ROT_HALF)).astype(np.float32))
    ang = jnp.arange(seq).astype(F32)[:, None] * inv_freq[None, :]
    cos, sin = jnp.cos(ang), jnp.sin(ang)
    ones = jnp.ones((seq, HD - 2 * ROT_HALF), F32)
    zeros8 = jnp.zeros((seq, ROT_HALF), F32)
    zeros = jnp.zeros((seq, HD - 2 * ROT_HALF), F32)
    cosK = jnp.concatenate([cos, cos, ones], axis=1)
    sinA = jnp.concatenate([-sin, zeros8, zeros], axis=1)
    sinB = jnp.concatenate([zeros8, sin, zeros], axis=1)
    rep = LANES // HD
    tile = lambda t: jnp.concatenate([t] * rep, axis=1)
    return cos.T, sin.T, tile(cosK), tile(sinA), tile(sinB)


def _selection_block_map_T(seq):
    nc = (seq - CMP_BLOCK) // CMP_STRIDE + 1
    nb = seq // SLC_BLOCK
    cs = np.arange(nc) * CMP_STRIDE
    bs = np.arange(nb) * SLC_BLOCK
    ov = np.minimum(cs[:, None] + CMP_BLOCK, bs[None, :] + SLC_BLOCK) - np.maximum(cs[:, None], bs[None, :])
    bmap = (np.clip(ov, 0, None) / CMP_BLOCK).astype(np.float32)
    out = np.zeros((nb, seq // CMP_STRIDE), np.float32)
    out[:, :nc] = bmap.T
    return jnp.asarray(out)


def kernel(x, c, w_ada, b_ada, norm1_w, w_in, cmp_pe_k, cmp_w1_k, cmp_w2_k, cmp_pe_v, cmp_w1_v, cmp_w2_v,
           conv_w, conv_b, dt_bias, a_log, d_skip, ssm_norm_w, w_attn_out, w_ssm_out, w_o, norm2_w,
           w_gate, w_up, w_down, norm_f_w):
    bsz, seq, d = x.shape
    depth = w_ada.shape[0]
    q_w = N_HEADS * HD
    kv_w = N_GROUPS * HD
    xbc_w = SSM_INNER + 2 * SSM_GROUPS * SSM_STATE
    offs = np.cumsum([q_w, 6 * kv_w, 3 * N_HEADS, SSM_INNER, xbc_w, SSM_HEADS])
    tabs = _rotary_tables(seq)
    bmapT = _selection_block_map_T(seq)
    x2d = x.reshape(bsz * seq, d)
    for l in range(depth):
        mod = _ada_mod(c, w_ada[l], b_ada[l]).reshape(bsz, 6, d)
        wi = w_in[l]
        wq = wi[:, :offs[0]]
        wkv = wi[:, offs[0]:offs[1]].reshape(d, 6, N_GROUPS * HD)
        wg = wi[:, offs[1]:offs[2]].reshape(d, N_GROUPS, HPG, 3)
        wz = wi[:, offs[2]:offs[3]]
        wxbc = wi[:, offs[3]:offs[4]]
        wdt = wi[:, offs[4]:offs[5]]
        wmg = wi[:, offs[5]:]
        wqT = wq.T.astype(BF16)
        wkn = jnp.concatenate([wkv[:, 0], wkv[:, 2], wkv[:, 4], wkv[:, 1]], axis=1).astype(BF16)
        wvT = jnp.concatenate([wkv[:, 3], wkv[:, 5]], axis=1).T.astype(BF16)
        wgT = jnp.transpose(wg, (1, 3, 2, 0)).reshape(3 * N_HEADS, d).astype(BF16)
        wdtp = jnp.pad(wdt, ((0, 0), (0, LANES - SSM_HEADS))).astype(BF16)
        wbig = jnp.concatenate([wz, wmg, wxbc], axis=1).astype(BF16)
        nw1 = norm1_w[l].reshape(1, d)

        qT, kn, vT, gT, dt = _inproj_attn(x2d, mod, nw1, wqT, wkn, wvT, wgT, wdtp, tabs, bsz, seq)
        zxm = _inproj_big(x2d, mod, nw1, wbig, bsz, seq)

        kn16 = kn.reshape(bsz, 8, seq // CMP_STRIDE, CMP_STRIDE * HD)
        kc = _compress(kn16, 0, cmp_pe_k[l], cmp_w1_k[l], cmp_w2_k[l], False, bsz, seq)
        vcT = _compress(kn16, 6, cmp_pe_v[l], cmp_w1_v[l], cmp_w2_v[l], True, bsz, seq)
        oT = _nsa(qT, kc, vcT, kn, vT, gT, bmapT, bsz, seq)

        cw = conv_w[l]
        cb = conv_b[l].reshape(1, xbc_w)
        pad_h = lambda v: jnp.pad(v.reshape(1, SSM_HEADS), ((0, 0), (0, LANES - SSM_HEADS)))
        dsk = jnp.repeat(d_skip[l], SSM_HD).reshape(1, SSM_INNER)
        ossm = _ssd(zxm, dt, cw[:, :SSM_INNER], cw[:, SSM_INNER:], cb[:, :SSM_INNER], cb[:, SSM_INNER:],
                    pad_h(dt_bias[l]), pad_h(a_log[l]), dsk, ssm_norm_w[l].reshape(1, SSM_INNER), bsz, seq)

        x1 = _outproj(oT, ossm, zxm, x2d, mod, w_attn_out[l].astype(BF16), w_ssm_out[l].astype(BF16),
                      w_o[l].astype(BF16), bsz, seq)
        x2d = _ffn(x1, mod, norm2_w[l].reshape(1, d), norm_f_w.reshape(1, d), w_gate[l].astype(BF16),
                   w_up[l].astype(BF16), w_down[l].astype(BF16), l == depth - 1, bsz, seq)
    return x2d.reshape(bsz, seq, d)
```

```python
import functools

import numpy as np
import jax
import jax.numpy as jnp
from jax import lax
from jax.experimental import pallas as pl
from jax.experimental.pallas import tpu as pltpu

F32 = jnp.float32
BF16 = jnp.bfloat16
HIGHEST = lax.Precision.HIGHEST

EPS = 1e-6
NEG = -1e30

N_HEADS = 16
N_GROUPS = 2
HPG = N_HEADS // N_GROUPS
HD = 64
ROT_HALF = 8
ROPE_THETA = 500000.0
CMP_BLOCK = 32
CMP_STRIDE = 16
CMP_HIDDEN = 256
SLC_BLOCK = 64
SLC_TOPK = 16
WINDOW = 512

SSM_HEADS = 32
SSM_HD = 64
SSM_GROUPS = 4
SSM_STATE = 128
SSM_CONV = 4
SSM_CHUNK = 128
SSM_INNER = SSM_HEADS * SSM_HD
SSM_PAD = 16

TQ = 128
KS = 512
LANES = 128
VROWS = HD + 16
LOG2E = float(np.log2(np.e))
VMEM_LIMIT = 56 * 1024 * 1024


def _cparams(sem, vmem=VMEM_LIMIT):
    return pltpu.CompilerParams(dimension_semantics=sem, vmem_limit_bytes=vmem)


def _sigmoid(x):
    return 0.5 + 0.5 * jnp.tanh(0.5 * x)


def _silu(x):
    hx = 0.5 * x
    return hx + hx * jnp.tanh(hx)


def _norm_mod(x, w, sc, sh):
    ms = jnp.mean(x * x, axis=-1, keepdims=True)
    return (x * lax.rsqrt(ms + EPS) * w) * (1.0 + sc) + sh


def _dot(a, b):
    return jnp.dot(a, b, preferred_element_type=F32)


def _dot_split3(a, x):
    hi = x.astype(BF16)
    r1 = x - hi.astype(F32)
    mid = r1.astype(BF16)
    lo = (r1 - mid.astype(F32)).astype(BF16)
    return _dot(a, hi) + _dot(a, mid) + _dot(a, lo)


def _dot_nt(a, b):
    return lax.dot_general(a, b, (((1,), (1,)), ((), ())), preferred_element_type=F32)


def _dot_tn(a, b):
    return lax.dot_general(a, b, (((0,), (0,)), ((), ())), preferred_element_type=F32)


def _ada_kernel(c_ref, w_ref, b_ref, o_ref):
    c = c_ref[...]
    o_ref[...] = jnp.dot(_silu(c), w_ref[...], precision=HIGHEST, preferred_element_type=F32) + b_ref[...]


def _ada_mod(c, w, b):
    bsz, d = c.shape
    n = w.shape[1]
    tn = 1024
    return pl.pallas_call(
        _ada_kernel,
        out_shape=jax.ShapeDtypeStruct((bsz, n), F32),
        grid=(n // tn,),
        in_specs=[pl.BlockSpec((bsz, d), lambda j: (0, 0)),
                  pl.BlockSpec((d, tn), lambda j: (0, j)),
                  pl.BlockSpec((1, tn), lambda j: (0, j))],
        out_specs=pl.BlockSpec((bsz, tn), lambda j: (0, j)),
        compiler_params=_cparams(("parallel",)),
        name="ada_mod",
    )(c, w, b.reshape(1, n))


def _inproj_attn_kernel(x_ref, mod_ref, nw_ref, wqT_ref, wkn_ref, wvT_ref, wgT_ref, wdt_ref,
                        cosT_ref, sinT_ref, cosK_ref, sinA_ref, sinB_ref,
                        qT_ref, ksa_ref, kn_ref, vT_ref, gT_ref, dt_ref):
    tm = x_ref.shape[0]
    mod = mod_ref[0]
    h = _norm_mod(x_ref[...], nw_ref[...], mod[1:2], mod[0:1]).astype(BF16)

    qT = _dot_nt(wqT_ref[...], h)
    cT = cosT_ref[...]
    sT = sinT_ref[...]
    scale = HD ** -0.5 * LOG2E
    for hh in range(N_HEADS):
        r0 = hh * HD
        x1 = qT[r0:r0 + ROT_HALF]
        x2 = qT[r0 + ROT_HALF:r0 + 2 * ROT_HALF]
        blk = jnp.concatenate([x1 * cT - x2 * sT, x2 * cT + x1 * sT, qT[r0 + 2 * ROT_HALF:r0 + HD]], axis=0)
        qT_ref[0, r0:r0 + HD, :] = (blk * scale).astype(BF16)

    kn = _dot(h, wkn_ref[...])
    nrot = 8 * HD
    kk = kn[:, :nrot]
    reps = nrot // LANES
    cK = jnp.concatenate([cosK_ref[...]] * reps, axis=1)
    sA = jnp.concatenate([sinA_ref[...]] * reps, axis=1)
    sB = jnp.concatenate([sinB_ref[...]] * reps, axis=1)
    kk = kk * cK + pltpu.roll(kk, nrot - ROT_HALF, 1) * sA + pltpu.roll(kk, ROT_HALF, 1) * sB
    row = lax.broadcasted_iota(jnp.int32, (tm, LANES), 0)
    lane = lax.broadcasted_iota(jnp.int32, (tm, LANES), 1)
    bps = KS // SLC_BLOCK
    blk = jnp.right_shift(row, int(np.log2(SLC_BLOCK)))
    onehot = jnp.where(lane == HD + (blk & (bps - 1)), 1.0, 0.0)
    for g in range(N_GROUPS):
        ksa_ref[0, g] = (kk[:, g * LANES:(g + 1) * LANES] + onehot).astype(BF16)
    for i in range(4):
        kn_ref[0, i] = kk[:, (4 + i) * HD:(5 + i) * HD].astype(BF16)
    for i in range(2):
        kn_ref[0, 4 + i] = kn[:, nrot + i * HD:nrot + (i + 1) * HD].astype(BF16)

    vT = _dot_nt(wvT_ref[...], h).astype(BF16)
    ones = jnp.ones((VROWS - HD, LANES), BF16)
    for j in range(tm // LANES):
        for r in range(4):
            vT_ref[0, j, r * VROWS:r * VROWS + HD, :] = vT[r * HD:(r + 1) * HD, j * LANES:(j + 1) * LANES]
            vT_ref[0, j, r * VROWS + HD:(r + 1) * VROWS, :] = ones

    gT_ref[0] = _sigmoid(_dot_nt(wgT_ref[...], h))
    dt_ref[...] = _dot(h, wdt_ref[...])


def _inproj_attn(x2d, mod, nw, wqT, wkn, wvT, wgT, wdt, tabs, bsz, seq):
    d = x2d.shape[1]
    tm = KS
    tpb = seq // tm
    cosT, sinT, cosK, sinA, sinB = tabs
    full = lambda a: pl.BlockSpec(a.shape, lambda i: (0,) * a.ndim)
    out_shape = (
        jax.ShapeDtypeStruct((bsz, N_HEADS * HD, seq), BF16),
        jax.ShapeDtypeStruct((bsz, N_GROUPS, seq, LANES), BF16),
        jax.ShapeDtypeStruct((bsz, 6, seq, HD), BF16),
        jax.ShapeDtypeStruct((bsz, seq // LANES, 4 * VROWS, LANES), BF16),
        jax.ShapeDtypeStruct((bsz, 3 * N_HEADS, seq), F32),
        jax.ShapeDtypeStruct((bsz * seq, LANES), F32),
    )
    return pl.pallas_call(
        _inproj_attn_kernel,
        out_shape=out_shape,
        grid=(bsz * tpb,),
        in_specs=[pl.BlockSpec((tm, d), lambda i: (i, 0)),
                  pl.BlockSpec((1, 6, d), lambda i: (i // tpb, 0, 0)),
                  full(nw), full(wqT), full(wkn), full(wvT), full(wgT), full(wdt),
                  pl.BlockSpec((ROT_HALF, tm), lambda i: (0, i % tpb)),
                  pl.BlockSpec((ROT_HALF, tm), lambda i: (0, i % tpb)),
                  pl.BlockSpec((tm, LANES), lambda i: (i % tpb, 0)),
                  pl.BlockSpec((tm, LANES), lambda i: (i % tpb, 0)),
                  pl.BlockSpec((tm, LANES), lambda i: (i % tpb, 0))],
        out_specs=(pl.BlockSpec((1, N_HEADS * HD, tm), lambda i: (i // tpb, 0, i % tpb)),
                   pl.BlockSpec((1, N_GROUPS, tm, LANES), lambda i: (i // tpb, 0, i % tpb, 0)),
                   pl.BlockSpec((1, 6, tm, HD), lambda i: (i // tpb, 0, i % tpb, 0)),
                   pl.BlockSpec((1, tm // LANES, 4 * VROWS, LANES), lambda i: (i // tpb, i % tpb, 0, 0)),
                   pl.BlockSpec((1, 3 * N_HEADS, tm), lambda i: (i // tpb, 0, i % tpb)),
                   pl.BlockSpec((tm, LANES), lambda i: (i, 0))),
        compiler_params=_cparams(("parallel",)),
        name="inproj_attn",
    )(x2d, mod, nw, wqT, wkn, wvT, wgT, wdt, cosT, sinT, cosK, sinA, sinB)


def _inproj_big_kernel(x_ref, mod_ref, nw_ref, w_ref, o_ref, h_scr):
    j = pl.program_id(1)

    @pl.when(j == 0)
    def _():
        mod = mod_ref[0]
        h_scr[...] = _norm_mod(x_ref[...], nw_ref[...], mod[1:2], mod[0:1]).astype(BF16)

    acc = _dot(h_scr[...], w_ref[...])
    is_gate = jnp.logical_or(j == 2, j == 3)
    o_ref[...] = jnp.where(is_gate, _sigmoid(acc), acc).astype(BF16)


def _inproj_big(x2d, mod, nw, wbig, bsz, seq):
    d = x2d.shape[1]
    n = wbig.shape[1]
    tm, tn = 1024, 1024
    tpb = seq // tm
    return pl.pallas_call(
        _inproj_big_kernel,
        out_shape=jax.ShapeDtypeStruct((bsz * seq, n), BF16),
        grid=(bsz * tpb, n // tn),
        in_specs=[pl.BlockSpec((tm, d), lambda i, j: (i, 0)),
                  pl.BlockSpec((1, 6, d), lambda i, j: (i // tpb, 0, 0)),
                  pl.BlockSpec((1, d), lambda i, j: (0, 0)),
                  pl.BlockSpec((d, tn), lambda i, j: (0, j))],
        out_specs=pl.BlockSpec((tm, tn), lambda i, j: (i, j)),
        scratch_shapes=[pltpu.VMEM((tm, d), BF16)],
        compiler_params=_cparams(("parallel", "arbitrary")),
        name="inproj_big",
    )(x2d, mod, nw, wbig)


def _compress_kernel(x_ref, pea_ref, peb_ref, w1a_ref, w1b_ref, w2_ref, o_ref, *, transpose_out, nc):
    x = x_ref[0, 0].astype(F32)
    nrow = x.shape[0]
    ya = _dot((x + pea_ref[...]).astype(BF16), w1a_ref[...])
    yb = _dot((x + peb_ref[...]).astype(BF16), w1b_ref[...])
    hp = ya + pltpu.roll(yb, nrow - 1, 0)
    row = lax.broadcasted_iota(jnp.int32, hp.shape, 0)
    hh = jnp.where(row < nc, _silu(hp), 0.0).astype(BF16)
    if transpose_out:
        o_ref[0, 0] = _dot_nt(w2_ref[...], hh).astype(BF16)
    else:
        o_ref[0, 0] = _dot(hh, w2_ref[...]).astype(BF16)


def _compress(kn16, slot0, pe, w1, w2, transpose_out, bsz, seq):
    nrow = seq // CMP_STRIDE
    nc = (seq - CMP_BLOCK) // CMP_STRIDE + 1
    half = CMP_STRIDE * HD
    pea = pe[:CMP_STRIDE].reshape(1, half)
    peb = pe[CMP_STRIDE:].reshape(1, half)
    w1a = w1[:half].astype(BF16)
    w1b = w1[half:].astype(BF16)
    if transpose_out:
        w2p = w2.T.astype(BF16)
        oshape, oblock = (bsz, N_GROUPS, HD, nrow), (1, 1, HD, nrow)
    else:
        w2p = w2.astype(BF16)
        oshape, oblock = (bsz, N_GROUPS, nrow, HD), (1, 1, nrow, HD)
    full = lambda a: pl.BlockSpec(a.shape, lambda b, g: (0,) * a.ndim)
    return pl.pallas_call(
        functools.partial(_compress_kernel, transpose_out=transpose_out, nc=nc),
        out_shape=jax.ShapeDtypeStruct(oshape, BF16),
        grid=(bsz, N_GROUPS),
        in_specs=[pl.BlockSpec((1, 1, nrow, half), lambda b, g: (b, slot0 + g, 0, 0)),
                  full(pea), full(peb), full(w1a), full(w1b), full(w2p)],
        out_specs=pl.BlockSpec(oblock, lambda b, g: (b, g, 0, 0)),
        compiler_params=_cparams(("parallel", "parallel")),
        name="compress_v" if transpose_out else "compress_k",
    )(kn16, pea, peb, w1a, w1b, w2p)


def _nsa_kernel(q_ref, kc_ref, vcT_ref, ksa_ref, kw_ref, vsT_ref, vwT_ref, g_ref, bmapT_ref, o_ref,
                v_scr, selb_scr, m_scr, acc_scr, sa_scr, sb_scr, *, nb):
    qi = pl.program_id(2)
    t0 = qi * TQ
    wl = HPG * TQ
    q = q_ref[0]
    Q = jnp.concatenate([q[hh * HD:(hh + 1) * HD, :] for hh in range(HPG)], axis=1)
    tok1 = t0 + lax.broadcasted_iota(jnp.int32, (1, TQ), 1)
    tok = jnp.concatenate([tok1] * HPG, axis=1)

    ncp = kc_ref.shape[2]
    sc = _dot(kc_ref[0, 0], Q)
    n_idx = lax.broadcasted_iota(jnp.int32, (ncp, 1), 0)
    mask_c = (n_idx * CMP_STRIDE + (CMP_BLOCK - 1)) <= tok
    sc = jnp.where(mask_c, sc, NEG)
    mc = jnp.max(sc, axis=0, keepdims=True)
    ec = jnp.exp2(sc - mc)
    pc = jnp.where(mask_c, ec / jnp.sum(ec, axis=0, keepdims=True), 0.0)
    o_c = _dot(vcT_ref[0, 0], pc.astype(BF16))

    def lanes8(x):
        return jnp.concatenate([x] * HPG, axis=1)

    nwt = WINDOW // TQ + 1
    kt0 = jnp.maximum(qi - WINDOW // TQ, 0)
    kw0 = pl.multiple_of(kt0 * TQ, TQ)
    sw = _dot(kw_ref[0, 0, pl.ds(kw0, nwt * TQ), :], Q)
    diff = tok1 - (kw0 + lax.broadcasted_iota(jnp.int32, (nwt * TQ, 1), 0))
    sw = sw + lanes8(jnp.where((diff >= 0) & (diff < WINDOW), 0.0, NEG))
    pw = jnp.exp2(sw - jnp.max(sw, axis=0, keepdims=True)).astype(BF16)
    vwin = jnp.concatenate([vwT_ref[0, kt0 + j] for j in range(nwt)], axis=1)
    ow = _dot(vwin, pw)
    o_w = ow[:HD] / ow[HD:HD + 1]

    psum = pc[:, 0:TQ]
    for hh in range(1, HPG):
        psum = psum + pc[:, hh * TQ:(hh + 1) * TQ]
    imp = _dot_split3(bmapT_ref[...], psum)
    jb = lax.broadcasted_iota(jnp.int32, (nb, 1), 0)
    cur = jnp.right_shift(tok1, int(np.log2(SLC_BLOCK)))
    forced = (jb == 0) | (jb == cur) | (jb == cur - 1)
    v = jnp.where(forced, jnp.inf, jnp.where(jb <= cur, imp, -jnp.inf))
    v_scr[...] = v
    jbf = lax.broadcasted_iota(jnp.int32, (nb, TQ), 0)

    def rank_body(i2, rank):
        for u in range(2):
            i = 2 * i2 + u
            vi = v_scr[pl.ds(i, 1), :]
            ge = jnp.where(vi >= v, 1.0, 0.0)
            gt = jnp.where(vi > v, 1.0, 0.0)
            rank = rank + jnp.where(jbf > i, ge, gt)
        return rank

    rank = lax.fori_loop(0, jnp.minimum(qi + 1, nb // 2), rank_body, jnp.zeros((nb, TQ), F32))
    selb_scr[...] = jnp.where(rank < min(SLC_TOPK, nb), 0.0, NEG)

    tps = KS // TQ
    bps = KS // SLC_BLOCK
    m_scr[...] = jnp.full((1, wl), NEG, F32)
    acc_scr[...] = jnp.zeros((VROWS, wl), F32)
    st_hi = qi // tps

    def scores(st, valid):
        sb = selb_scr[pl.ds(pl.multiple_of(st * bps, bps), bps), :]
        sb = jnp.where(valid, sb, NEG)
        ext = lanes8(jnp.concatenate([sb, jnp.zeros((16 - bps, TQ), F32)], axis=0)).astype(BF16)
        qa = jnp.concatenate([Q, ext, jnp.zeros((LANES - HD - 16, wl), BF16)], axis=0)
        return _dot(ksa_ref[0, 0, pl.ds(pl.multiple_of(st * KS, KS), KS), :], qa)

    def update(s, st):
        m_old = m_scr[...]
        m_new = jnp.maximum(m_old, jnp.max(s, axis=0, keepdims=True))
        p = jnp.exp2(s - m_new).astype(BF16)
        vs = jnp.concatenate([vsT_ref[0, st * tps + j] for j in range(tps)], axis=1)
        acc_scr[...] = acc_scr[...] * jnp.exp2(m_old - m_new) + _dot(vs, p)
        m_scr[...] = m_new

    kpos = st_hi * KS + lax.broadcasted_iota(jnp.int32, (KS, 1), 0)
    sa_scr[...] = scores(st_hi, True) + lanes8(jnp.where(kpos <= tok1, 0.0, NEG))

    def pair_body(j, carry):
        ta = st_hi - 2 * j
        tb = ta - 1
        tc = ta - 2
        sb_scr[...] = scores(jnp.maximum(tb, 0), tb >= 0)
        update(sa_scr[...], ta)
        sa_scr[...] = scores(jnp.maximum(tc, 0), tc >= 0)
        update(sb_scr[...], jnp.maximum(tb, 0))
        return carry

    lax.fori_loop(0, (st_hi + 2) // 2, pair_body, 0)
    o_s = acc_scr[0:HD, :] / acc_scr[HD:HD + 1, :]

    gts = g_ref[0, 0]
    for hh in range(HPG):
        sl = slice(hh * TQ, (hh + 1) * TQ)
        o = (gts[hh:hh + 1] * o_c[:, sl] + gts[HPG + hh:HPG + hh + 1] * o_s[:, sl]
             + gts[2 * HPG + hh:2 * HPG + hh + 1] * o_w[:, sl])
        o_ref[0, hh * HD:(hh + 1) * HD, :] = o.astype(BF16)


def _nsa(qT, kc, vcT, ksa, kn, vT, gT, bmapT, bsz, seq):
    nq = seq // TQ
    nb = seq // SLC_BLOCK
    ncp = kc.shape[2]
    wl = HPG * TQ
    gT4 = gT.reshape(bsz, N_GROUPS, 3 * HPG, seq)
    assert seq % KS == 0 and seq >= WINDOW + TQ and KS // SLC_BLOCK == 8
    return pl.pallas_call(
        functools.partial(_nsa_kernel, nb=nb),
        out_shape=jax.ShapeDtypeStruct((bsz, N_HEADS * HD, seq), BF16),
        grid=(bsz, N_GROUPS, nq),
        in_specs=[pl.BlockSpec((1, HPG * HD, TQ), lambda b, g, i: (b, g, i)),
                  pl.BlockSpec((1, 1, ncp, HD), lambda b, g, i: (b, g, 0, 0)),
                  pl.BlockSpec((1, 1, HD, ncp), lambda b, g, i: (b, g, 0, 0)),
                  pl.BlockSpec((1, 1, seq, LANES), lambda b, g, i: (b, g, 0, 0)),
                  pl.BlockSpec((1, 1, seq, HD), lambda b, g, i: (b, 2 + g, 0, 0)),
                  pl.BlockSpec((1, nq, VROWS, LANES), lambda b, g, i: (b, 0, g, 0)),
                  pl.BlockSpec((1, nq, VROWS, LANES), lambda b, g, i: (b, 0, 2 + g, 0)),
                  pl.BlockSpec((1, 1, 3 * HPG, TQ), lambda b, g, i: (b, g, 0, i)),
                  pl.BlockSpec(bmapT.shape, lambda b, g, i: (0, 0))],
        out_specs=pl.BlockSpec((1, HPG * HD, TQ), lambda b, g, i: (b, g, i)),
        scratch_shapes=[pltpu.VMEM((nb, TQ), F32),
                        pltpu.VMEM((nb, TQ), F32),
                        pltpu.VMEM((1, wl), F32),
                        pltpu.VMEM((VROWS, wl), F32),
                        pltpu.VMEM((KS, wl), F32),
                        pltpu.VMEM((KS, wl), F32)],
        compiler_params=_cparams(("parallel", "parallel", "arbitrary")),
        name="nsa_attention",
    )(qT, kc, vcT, ksa, kn, vT, vT, gT4, bmapT)


def _ssd_kernel(z_ref, xs_ref, bc_ref, dt_ref, cwx_ref, cwb_ref, cbx_ref, cbb_ref, dtb_ref, alog_ref,
                dsk_ref, nw_ref, o_ref, xpx_scr, xpb_scr, st_scr):
    c = pl.program_id(1)
    L = SSM_CHUNK
    pad = SSM_PAD
    npair = SSM_HEADS // 2
    ppg = npair // SSM_GROUPS

    @pl.when(c == 0)
    def _():
        xpx_scr[0:pad, :] = jnp.zeros((pad, xpx_scr.shape[1]), BF16)
        xpb_scr[0:pad, :] = jnp.zeros((pad, xpb_scr.shape[1]), BF16)
        st_scr[...] = jnp.zeros(st_scr.shape, F32)

    nshift = SSM_CONV - 1
    srow = lax.broadcasted_iota(jnp.int32, (nshift * L, pad + L), 0)
    scol = lax.broadcasted_iota(jnp.int32, (nshift * L, pad + L), 1)
    shift = jnp.where(scol == (srow & (L - 1)) + jnp.right_shift(srow, int(np.log2(L))) + (pad - nshift),
                      1.0, 0.0).astype(BF16)

    def conv_silu(x_ref, w_ref, b_ref, scr):
        x = x_ref[...]
        scr[pad:pad + L, :] = x
        sh = _dot(shift, scr[...])
        acc = b_ref[...] + w_ref[nshift:nshift + 1, :] * x.astype(F32)
        for k in range(nshift):
            acc = acc + w_ref[k:k + 1, :] * sh[k * L:(k + 1) * L]
        scr[0:pad, :] = scr[L:L + pad, :]
        return _silu(acc)

    xs = conv_silu(xs_ref, cwx_ref, cbx_ref, xpx_scr)
    bcm = conv_silu(bc_ref, cwb_ref, cbb_ref, xpb_scr).astype(BF16)
    xs_b = xs.astype(BF16)

    dtv = dt_ref[...] + dtb_ref[...]
    dt = jnp.maximum(dtv, 0.0) + jnp.log(1.0 + jnp.exp(-jnp.abs(dtv)))
    a = -jnp.exp(alog_ref[...])
    row = lax.broadcasted_iota(jnp.int32, (L, L), 0)
    col = lax.broadcasted_iota(jnp.int32, (L, L), 1)
    causal = col <= row
    a_cum = _dot_split3(jnp.where(causal, 1.0, 0.0).astype(BF16), dt * a)
    a_cumT = a_cum.T
    dtT = dt.T
    a_last = a_cum[L - 1:L, :]
    a_lastT = a_cumT[:, L - 1:L]
    wT = dtT * jnp.exp(a_lastT - a_cumT)
    e_last = jnp.exp(a_last)
    lo = lax.broadcasted_iota(jnp.int32, (1, LANES), 1) < SSM_HD

    ys = []
    for g in range(SSM_GROUPS):
        Bg = bcm[:, g * SSM_STATE:(g + 1) * SSM_STATE]
        Cg = bcm[:, (SSM_GROUPS + g) * SSM_STATE:(SSM_GROUPS + g + 1) * SSM_STATE]
        cb = _dot_nt(Cg, Bg)
        BgT = Bg.astype(F32).T
        for jp in range(ppg):
            pj = g * ppg + jp
            h0, h1 = 2 * pj, 2 * pj + 1
            xp = xs_b[:, pj * LANES:(pj + 1) * LANES]

            def head_mats(h):
                ecol = jnp.broadcast_to(a_cum[:, h:h + 1], (L, L))
                dec = jnp.exp(jnp.where(causal, ecol - a_cumT[h:h + 1, :], NEG))
                gm = (cb * dec * dtT[h:h + 1, :]).astype(BF16)
                bw = (BgT * wT[h:h + 1, :]).astype(BF16)
                return gm, bw, jnp.exp(ecol)

            g0, b0, e0 = head_mats(h0)
            g1, b1, e1 = head_mats(h1)
            y_d = jnp.where(lo, _dot(g0, xp), _dot(g1, xp))
            st_prev = st_scr[pj]
            y_o = _dot(Cg, st_prev.astype(BF16)) * jnp.where(lo, e0, e1)
            st_new = jnp.where(lo, _dot(b0, xp), _dot(b1, xp))
            cdec = jnp.where(lo, jnp.broadcast_to(e_last[:, h0:h0 + 1], (1, LANES)),
                             jnp.broadcast_to(e_last[:, h1:h1 + 1], (1, LANES)))
            st_scr[pj] = st_prev * cdec + st_new
            sl = slice(pj * LANES, (pj + 1) * LANES)
            ys.append(y_d + y_o + dsk_ref[:, sl] * xs[:, sl])
    y = jnp.concatenate(ys, axis=1)
    y = y * _silu(z_ref[...].astype(F32))
    gw = SSM_INNER // SSM_GROUPS
    for g in range(SSM_GROUPS):
        yg = y[:, g * gw:(g + 1) * gw]
        ms = jnp.mean(yg * yg, axis=-1, keepdims=True)
        o_ref[:, g * gw:(g + 1) * gw] = (yg * lax.rsqrt(ms + EPS) * nw_ref[:, g * gw:(g + 1) * gw]).astype(BF16)


def _ssd(zxm, dt, cwx, cwb, cbx, cbb, dtb, alog, dsk, nw, bsz, seq):
    L = SSM_CHUNK
    nchunk = seq // L
    nbc = 2 * SSM_GROUPS * SSM_STATE
    full = lambda a: pl.BlockSpec(a.shape, lambda b, c: (0,) * a.ndim)
    rowblk = lambda width, idx: pl.BlockSpec((L, width), lambda b, c: (b * nchunk + c, idx))
    return pl.pallas_call(
        _ssd_kernel,
        out_shape=jax.ShapeDtypeStruct((bsz * seq, SSM_INNER), BF16),
        grid=(bsz, nchunk),
        in_specs=[rowblk(SSM_INNER, 0), rowblk(SSM_INNER, 2), rowblk(nbc, 6), rowblk(LANES, 0),
                  full(cwx), full(cwb), full(cbx), full(cbb), full(dtb), full(alog), full(dsk), full(nw)],
        out_specs=pl.BlockSpec((L, SSM_INNER), lambda b, c: (b * nchunk + c, 0)),
        scratch_shapes=[pltpu.VMEM((L + SSM_PAD, SSM_INNER), BF16),
                        pltpu.VMEM((L + SSM_PAD, nbc), BF16),
                        pltpu.VMEM((SSM_HEADS // 2, SSM_STATE, LANES), F32)],
        compiler_params=_cparams(("parallel", "arbitrary")),
        name="ssd",
    )(zxm, zxm, zxm, dt, cwx, cwb, cbx, cbb, dtb, alog, dsk, nw)


def _outproj_kernel(oT_ref, os_ref, ga_ref, gs_ref, x_ref, mod_ref, wa_ref, ws_ref, wo_ref, o_ref):
    ya = _dot_tn(oT_ref[0], wa_ref[...])
    ys = _dot(os_ref[...], ws_ref[...])
    m = ga_ref[...].astype(F32) * ya + gs_ref[...].astype(F32) * ys
    r = _dot(m.astype(BF16), wo_ref[...])
    o_ref[...] = x_ref[...] + mod_ref[0][2:3] * r


def _outproj(oT, ossm, zxm, x2d, mod, wa, ws, wo, bsz, seq):
    d = x2d.shape[1]
    tm = 512
    tpb = seq // tm
    full = lambda a: pl.BlockSpec(a.shape, lambda i: (0,) * a.ndim)
    return pl.pallas_call(
        _outproj_kernel,
        out_shape=jax.ShapeDtypeStruct((bsz * seq, d), F32),
        grid=(bsz * tpb,),
        in_specs=[pl.BlockSpec((1, N_HEADS * HD, tm), lambda i: (i // tpb, 0, i % tpb)),
                  pl.BlockSpec((tm, SSM_INNER), lambda i: (i, 0)),
                  pl.BlockSpec((tm, d), lambda i: (i, 2)),
                  pl.BlockSpec((tm, d), lambda i: (i, 3)),
                  pl.BlockSpec((tm, d), lambda i: (i, 0)),
                  pl.BlockSpec((1, 6, d), lambda i: (i // tpb, 0, 0)),
                  full(wa), full(ws), full(wo)],
        out_specs=pl.BlockSpec((tm, d), lambda i: (i, 0)),
        compiler_params=_cparams(("parallel",)),
        name="outproj",
    )(oT, ossm, zxm, zxm, x2d, mod, wa, ws, wo)


def _ffn_kernel(x_ref, mod_ref, n2_ref, nf_ref, wg_ref, wu_ref, wd_ref, o_ref, *, final_norm):
    x = x_ref[...]
    mod = mod_ref[0]
    h = _norm_mod(x, n2_ref[...], mod[4:5], mod[3:4]).astype(BF16)
    a = _silu(_dot(h, wg_ref[...])) * _dot(h, wu_ref[...])
    r = _dot(a.astype(BF16), wd_ref[...])
    x2 = x + mod[5:6] * r
    if final_norm:
        ms = jnp.mean(x2 * x2, axis=-1, keepdims=True)
        x2 = x2 * lax.rsqrt(ms + EPS) * nf_ref[...]
    o_ref[...] = x2


def _ffn(x1, mod, n2, nf, wg, wu, wd, final_norm, bsz, seq):
    d = x1.shape[1]
    tm = 512
    tpb = seq // tm
    const = lambda a: pl.BlockSpec(a.shape, lambda i: (0,) * a.ndim, pipeline_mode=pl.Buffered(1))
    return pl.pallas_call(
        functools.partial(_ffn_kernel, final_norm=final_norm),
        out_shape=jax.ShapeDtypeStruct((bsz * seq, d), F32),
        grid=(bsz * tpb,),
        in_specs=[pl.BlockSpec((tm, d), lambda i: (i, 0)),
                  pl.BlockSpec((1, 6, d), lambda i: (i // tpb, 0, 0)),
                  const(n2), const(nf), const(wg), const(wu), const(wd)],
        out_specs=pl.BlockSpec((tm, d), lambda i: (i, 0)),
        compiler_params=_cparams(("parallel",)),
        name="ffn",
    )(x1, mod, n2, nf, wg, wu, wd)


def _rotary_tables(seq):
    inv_freq = jnp.asarray(np.power(ROPE_THETA, -np.arange(0, 2 * ROT_HALF, 2) / (2 * ROT_HALF)).astype(np.float32))
    ang = jnp.arange(seq).astype(F32)[:, None] * inv_freq[None, :]
    cos, sin = jnp.cos(ang), jnp.sin(ang)
    ones = jnp.ones((seq, HD - 2 * ROT_HALF), F32)
    zeros8 = jnp.zeros((seq, ROT_HALF), F32)
    zeros = jnp.zeros((seq, HD - 2 * ROT_HALF), F32)
    cosK = jnp.concatenate([cos, cos, ones], axis=1)
    sinA = jnp.concatenate([-sin, zeros8, zeros], axis=1)
    sinB = jnp.concatenate([zeros8, sin, zeros], axis=1)
    rep = LANES // HD
    tile = lambda t: jnp.concatenate([t] * rep, axis=1)
    return cos.T, sin.T, tile(cosK), tile(sinA), tile(sinB)


def _selection_block_map_T(seq):
    nc = (seq - CMP_BLOCK) // CMP_STRIDE + 1
    nb = seq // SLC_BLOCK
    cs = np.arange(nc) * CMP_STRIDE
    bs = np.arange(nb) * SLC_BLOCK
    ov = np.minimum(cs[:, None] + CMP_BLOCK, bs[None, :] + SLC_BLOCK) - np.maximum(cs[:, None], bs[None, :])
    bmap = (np.clip(ov, 0, None) / CMP_BLOCK).astype(np.float32)
    out = np.zeros((nb, seq // CMP_STRIDE), np.float32)
    out[:, :nc] = bmap.T
    return jnp.asarray(out)


def kernel(x, c, w_ada, b_ada, norm1_w, w_in, cmp_pe_k, cmp_w1_k, cmp_w2_k, cmp_pe_v, cmp_w1_v, cmp_w2_v,
           conv_w, conv_b, dt_bias, a_log, d_skip, ssm_norm_w, w_attn_out, w_ssm_out, w_o, norm2_w,
           w_gate, w_up, w_down, norm_f_w):
    bsz, seq, d = x.shape
    depth = w_ada.shape[0]
    q_w = N_HEADS * HD
    kv_w = N_GROUPS * HD
    xbc_w = SSM_INNER + 2 * SSM_GROUPS * SSM_STATE
    offs = np.cumsum([q_w, 6 * kv_w, 3 * N_HEADS, SSM_INNER, xbc_w, SSM_HEADS])
    tabs = _rotary_tables(seq)
    bmapT = _selection_block_map_T(seq)
    x2d = x.reshape(bsz * seq, d)
    for l in range(depth):
        mod = _ada_mod(c, w_ada[l], b_ada[l]).reshape(bsz, 6, d)
        wi = w_in[l]
        wq = wi[:, :offs[0]]
        wkv = wi[:, offs[0]:offs[1]].reshape(d, 6, N_GROUPS * HD)
        wg = wi[:, offs[1]:offs[2]].reshape(d, N_GROUPS, HPG, 3)
        wz = wi[:, offs[2]:offs[3]]
        wxbc = wi[:, offs[3]:offs[4]]
        wdt = wi[:, offs[4]:offs[5]]
        wmg = wi[:, offs[5]:]
        wqT = wq.T.astype(BF16)
        zcol = jnp.zeros((d, HD), wi.dtype)
        wkn = jnp.concatenate([wkv[:, 2, :HD], zcol, wkv[:, 2, HD:], zcol, wkv[:, 0], wkv[:, 4], wkv[:, 1]],
                              axis=1).astype(BF16)
        wvT = jnp.concatenate([wkv[:, 3], wkv[:, 5]], axis=1).T.astype(BF16)
        wgT = jnp.transpose(wg, (1, 3, 2, 0)).reshape(3 * N_HEADS, d).astype(BF16)
        wdtp = jnp.pad(wdt, ((0, 0), (0, LANES - SSM_HEADS))).astype(BF16)
        wbig = jnp.concatenate([wz, wmg, wxbc], axis=1).astype(BF16)
        nw1 = norm1_w[l].reshape(1, d)

        qT, ksa, kn, vT, gT, dt = _inproj_attn(x2d, mod, nw1, wqT, wkn, wvT, wgT, wdtp, tabs, bsz, seq)
        zxm = _inproj_big(x2d, mod, nw1, wbig, bsz, seq)

        kn16 = kn.reshape(bsz, 6, seq // CMP_STRIDE, CMP_STRIDE * HD)
        kc = _compress(kn16, 0, cmp_pe_k[l], cmp_w1_k[l], cmp_w2_k[l], False, bsz, seq)
        vcT = _compress(kn16, 4, cmp_pe_v[l], cmp_w1_v[l], cmp_w2_v[l], True, bsz, seq)
        oT = _nsa(qT, kc, vcT, ksa, kn, vT, gT, bmapT.astype(BF16), bsz, seq)

        cw = conv_w[l]
        cb = conv_b[l].reshape(1, xbc_w)
        pad_h = lambda v: jnp.pad(v.reshape(1, SSM_HEADS), ((0, 0), (0, LANES - SSM_HEADS)))
        dsk = jnp.repeat(d_skip[l], SSM_HD).reshape(1, SSM_INNER)
        ossm = _ssd(zxm, dt, cw[:, :SSM_INNER], cw[:, SSM_INNER:], cb[:, :SSM_INNER], cb[:, SSM_INNER:],
                    pad_h(dt_bias[l]), pad_h(a_log[l]), dsk, ssm_norm_w[l].reshape(1, SSM_INNER), bsz, seq)

        x1 = _outproj(oT, ossm, zxm, x2d, mod, w_attn_out[l].astype(BF16), w_ssm_out[l].astype(BF16),
                      w_o[l].astype(BF16), bsz, seq)
        x2d = _ffn(x1, mod, norm2_w[l].reshape(1, d), norm_f_w.reshape(1, d), w_gate[l].astype(BF16),
                   w_up[l].astype(BF16), w_down[l].astype(BF16), l == depth - 1, bsz, seq)
    return x2d.reshape(bsz, seq, d)
```

```python
import functools

import numpy as np
import jax
import jax.numpy as jnp
from jax import lax
from jax.experimental import pallas as pl
from jax.experimental.pallas import tpu as pltpu

F32 = jnp.float32
BF16 = jnp.bfloat16
HIGHEST = lax.Precision.HIGHEST

EPS = 1e-6
NEG = -1e30

N_HEADS = 16
N_GROUPS = 2
HPG = N_HEADS // N_GROUPS
HD = 64
ROT_HALF = 8
ROPE_THETA = 500000.0
CMP_BLOCK = 32
CMP_STRIDE = 16
CMP_HIDDEN = 256
SLC_BLOCK = 64
SLC_TOPK = 16
WINDOW = 512

SSM_HEADS = 32
SSM_HD = 64
SSM_GROUPS = 4
SSM_STATE = 128
SSM_CONV = 4
SSM_CHUNK = 128
SSM_INNER = SSM_HEADS * SSM_HD
SSM_PAD = 16

TQ = 128
KS = 512
LANES = 128
VROWS = HD + 16
LOG2E = float(np.log2(np.e))
VMEM_LIMIT = 56 * 1024 * 1024


def _cparams(sem, vmem=VMEM_LIMIT):
    return pltpu.CompilerParams(dimension_semantics=sem, vmem_limit_bytes=vmem)


def _sigmoid(x):
    return 0.5 + 0.5 * jnp.tanh(0.5 * x)


def _silu(x):
    hx = 0.5 * x
    return hx + hx * jnp.tanh(hx)


def _norm_mod(x, w, sc, sh):
    ms = jnp.mean(x * x, axis=-1, keepdims=True)
    return (x * lax.rsqrt(ms + EPS) * w) * (1.0 + sc) + sh


def _dot(a, b):
    return jnp.dot(a, b, preferred_element_type=F32)


def _dot_split3(a, x):
    hi = x.astype(BF16)
    r1 = x - hi.astype(F32)
    mid = r1.astype(BF16)
    lo = (r1 - mid.astype(F32)).astype(BF16)
    return _dot(a, hi) + _dot(a, mid) + _dot(a, lo)


def _dot_nt(a, b):
    return lax.dot_general(a, b, (((1,), (1,)), ((), ())), preferred_element_type=F32)


def _dot_tn(a, b):
    return lax.dot_general(a, b, (((0,), (0,)), ((), ())), preferred_element_type=F32)


def _ada_kernel(c_ref, w_ref, b_ref, o_ref):
    c = c_ref[...]
    o_ref[...] = jnp.dot(_silu(c), w_ref[...], precision=HIGHEST, preferred_element_type=F32) + b_ref[...]


def _ada_mod(c, w, b):
    bsz, d = c.shape
    n = w.shape[1]
    tn = 1024
    return pl.pallas_call(
        _ada_kernel,
        out_shape=jax.ShapeDtypeStruct((bsz, n), F32),
        grid=(n // tn,),
        in_specs=[pl.BlockSpec((bsz, d), lambda j: (0, 0)),
                  pl.BlockSpec((d, tn), lambda j: (0, j)),
                  pl.BlockSpec((1, tn), lambda j: (0, j))],
        out_specs=pl.BlockSpec((bsz, tn), lambda j: (0, j)),
        compiler_params=_cparams(("parallel",)),
        name="ada_mod",
    )(c, w, b.reshape(1, n))


def _inproj_attn_kernel(x_ref, mod_ref, nw_ref, wqT_ref, wkn_ref, wvT_ref, wgT_ref, wdt_ref,
                        cosT_ref, sinT_ref, cosK_ref, sinA_ref, sinB_ref,
                        qT_ref, ksa_ref, kn_ref, vT_ref, gT_ref, dt_ref):
    tm = x_ref.shape[0]
    mod = mod_ref[0]
    h = _norm_mod(x_ref[...], nw_ref[...], mod[1:2], mod[0:1]).astype(BF16)

    qT = _dot_nt(wqT_ref[...], h)
    cT = cosT_ref[...]
    sT = sinT_ref[...]
    scale = HD ** -0.5 * LOG2E
    for hh in range(N_HEADS):
        r0 = hh * HD
        x1 = qT[r0:r0 + ROT_HALF]
        x2 = qT[r0 + ROT_HALF:r0 + 2 * ROT_HALF]
        blk = jnp.concatenate([x1 * cT - x2 * sT, x2 * cT + x1 * sT, qT[r0 + 2 * ROT_HALF:r0 + HD]], axis=0)
        qT_ref[0, r0:r0 + HD, :] = (blk * scale).astype(BF16)

    kn = _dot(h, wkn_ref[...])
    nrot = 8 * HD
    kk = kn[:, :nrot]
    reps = nrot // LANES
    cK = jnp.concatenate([cosK_ref[...]] * reps, axis=1)
    sA = jnp.concatenate([sinA_ref[...]] * reps, axis=1)
    sB = jnp.concatenate([sinB_ref[...]] * reps, axis=1)
    kk = kk * cK + pltpu.roll(kk, nrot - ROT_HALF, 1) * sA + pltpu.roll(kk, ROT_HALF, 1) * sB
    row = lax.broadcasted_iota(jnp.int32, (tm, LANES), 0)
    lane = lax.broadcasted_iota(jnp.int32, (tm, LANES), 1)
    bps = KS // SLC_BLOCK
    blk = jnp.right_shift(row, int(np.log2(SLC_BLOCK)))
    onehot = jnp.where(lane == HD + (blk & (bps - 1)), 1.0, 0.0)
    for g in range(N_GROUPS):
        ksa_ref[0, g] = (kk[:, g * LANES:(g + 1) * LANES] + onehot).astype(BF16)
    for i in range(4):
        kn_ref[0, i] = kk[:, (4 + i) * HD:(5 + i) * HD].astype(BF16)
    for i in range(2):
        kn_ref[0, 4 + i] = kn[:, nrot + i * HD:nrot + (i + 1) * HD].astype(BF16)

    vT = _dot_nt(wvT_ref[...], h).astype(BF16)
    ones = jnp.ones((VROWS - HD, LANES), BF16)
    for j in range(tm // LANES):
        for r in range(4):
            vT_ref[0, j, r * VROWS:r * VROWS + HD, :] = vT[r * HD:(r + 1) * HD, j * LANES:(j + 1) * LANES]
            vT_ref[0, j, r * VROWS + HD:(r + 1) * VROWS, :] = ones

    gT_ref[0] = _sigmoid(_dot_nt(wgT_ref[...], h))
    dt_ref[...] = _dot(h, wdt_ref[...])


def _inproj_attn(x2d, mod, nw, wqT, wkn, wvT, wgT, wdt, tabs, bsz, seq):
    d = x2d.shape[1]
    tm = KS
    tpb = seq // tm
    cosT, sinT, cosK, sinA, sinB = tabs
    full = lambda a: pl.BlockSpec(a.shape, lambda i: (0,) * a.ndim)
    out_shape = (
        jax.ShapeDtypeStruct((bsz, N_HEADS * HD, seq), BF16),
        jax.ShapeDtypeStruct((bsz, N_GROUPS, seq, LANES), BF16),
        jax.ShapeDtypeStruct((bsz, 6, seq, HD), BF16),
        jax.ShapeDtypeStruct((bsz, seq // LANES, 4 * VROWS, LANES), BF16),
        jax.ShapeDtypeStruct((bsz, 3 * N_HEADS, seq), F32),
        jax.ShapeDtypeStruct((bsz * seq, LANES), F32),
    )
    return pl.pallas_call(
        _inproj_attn_kernel,
        out_shape=out_shape,
        grid=(bsz * tpb,),
        in_specs=[pl.BlockSpec((tm, d), lambda i: (i, 0)),
                  pl.BlockSpec((1, 6, d), lambda i: (i // tpb, 0, 0)),
                  full(nw), full(wqT), full(wkn), full(wvT), full(wgT), full(wdt),
                  pl.BlockSpec((ROT_HALF, tm), lambda i: (0, i % tpb)),
                  pl.BlockSpec((ROT_HALF, tm), lambda i: (0, i % tpb)),
                  pl.BlockSpec((tm, LANES), lambda i: (i % tpb, 0)),
                  pl.BlockSpec((tm, LANES), lambda i: (i % tpb, 0)),
                  pl.BlockSpec((tm, LANES), lambda i: (i % tpb, 0))],
        out_specs=(pl.BlockSpec((1, N_HEADS * HD, tm), lambda i: (i // tpb, 0, i % tpb)),
                   pl.BlockSpec((1, N_GROUPS, tm, LANES), lambda i: (i // tpb, 0, i % tpb, 0)),
                   pl.BlockSpec((1, 6, tm, HD), lambda i: (i // tpb, 0, i % tpb, 0)),
                   pl.BlockSpec((1, tm // LANES, 4 * VROWS, LANES), lambda i: (i // tpb, i % tpb, 0, 0)),
                   pl.BlockSpec((1, 3 * N_HEADS, tm), lambda i: (i // tpb, 0, i % tpb)),
                   pl.BlockSpec((tm, LANES), lambda i: (i, 0))),
        compiler_params=_cparams(("parallel",)),
        name="inproj_attn",
    )(x2d, mod, nw, wqT, wkn, wvT, wgT, wdt, cosT, sinT, cosK, sinA, sinB)


def _inproj_big_kernel(x_ref, mod_ref, nw_ref, w_ref, o_ref, h_scr):
    j = pl.program_id(1)

    @pl.when(j == 0)
    def _():
        mod = mod_ref[0]
        h_scr[...] = _norm_mod(x_ref[...], nw_ref[...], mod[1:2], mod[0:1]).astype(BF16)

    acc = _dot(h_scr[...], w_ref[...])
    is_gate = jnp.logical_or(j == 2, j == 3)
    o_ref[...] = jnp.where(is_gate, _sigmoid(acc), acc).astype(BF16)


def _inproj_big(x2d, mod, nw, wbig, bsz, seq):
    d = x2d.shape[1]
    n = wbig.shape[1]
    tm, tn = 1024, 1024
    tpb = seq // tm
    return pl.pallas_call(
        _inproj_big_kernel,
        out_shape=jax.ShapeDtypeStruct((bsz * seq, n), BF16),
        grid=(bsz * tpb, n // tn),
        in_specs=[pl.BlockSpec((tm, d), lambda i, j: (i, 0)),
                  pl.BlockSpec((1, 6, d), lambda i, j: (i // tpb, 0, 0)),
                  pl.BlockSpec((1, d), lambda i, j: (0, 0)),
                  pl.BlockSpec((d, tn), lambda i, j: (0, j))],
        out_specs=pl.BlockSpec((tm, tn), lambda i, j: (i, j)),
        scratch_shapes=[pltpu.VMEM((tm, d), BF16)],
        compiler_params=_cparams(("parallel", "arbitrary")),
        name="inproj_big",
    )(x2d, mod, nw, wbig)


def _compress_kernel(x_ref, pea_ref, peb_ref, w1a_ref, w1b_ref, w2_ref, o_ref, *, transpose_out, nc):
    x = x_ref[0, 0].astype(F32)
    nrow = x.shape[0]
    ya = _dot((x + pea_ref[...]).astype(BF16), w1a_ref[...])
    yb = _dot((x + peb_ref[...]).astype(BF16), w1b_ref[...])
    hp = ya + pltpu.roll(yb, nrow - 1, 0)
    row = lax.broadcasted_iota(jnp.int32, hp.shape, 0)
    hh = jnp.where(row < nc, _silu(hp), 0.0).astype(BF16)
    if transpose_out:
        o_ref[0, 0] = _dot_nt(w2_ref[...], hh).astype(BF16)
    else:
        o_ref[0, 0] = _dot(hh, w2_ref[...]).astype(BF16)


def _compress(kn16, slot0, pe, w1, w2, transpose_out, bsz, seq):
    nrow = seq // CMP_STRIDE
    nc = (seq - CMP_BLOCK) // CMP_STRIDE + 1
    half = CMP_STRIDE * HD
    pea = pe[:CMP_STRIDE].reshape(1, half)
    peb = pe[CMP_STRIDE:].reshape(1, half)
    w1a = w1[:half].astype(BF16)
    w1b = w1[half:].astype(BF16)
    if transpose_out:
        w2p = w2.T.astype(BF16)
        oshape, oblock = (bsz, N_GROUPS, HD, nrow), (1, 1, HD, nrow)
    else:
        w2p = w2.astype(BF16)
        oshape, oblock = (bsz, N_GROUPS, nrow, HD), (1, 1, nrow, HD)
    full = lambda a: pl.BlockSpec(a.shape, lambda b, g: (0,) * a.ndim)
    return pl.pallas_call(
        functools.partial(_compress_kernel, transpose_out=transpose_out, nc=nc),
        out_shape=jax.ShapeDtypeStruct(oshape, BF16),
        grid=(bsz, N_GROUPS),
        in_specs=[pl.BlockSpec((1, 1, nrow, half), lambda b, g: (b, slot0 + g, 0, 0)),
                  full(pea), full(peb), full(w1a), full(w1b), full(w2p)],
        out_specs=pl.BlockSpec(oblock, lambda b, g: (b, g, 0, 0)),
        compiler_params=_cparams(("parallel", "parallel")),
        name="compress_v" if transpose_out else "compress_k",
    )(kn16, pea, peb, w1a, w1b, w2p)


def _nsa_kernel(q_ref, kc_ref, vcT_ref, ksa_ref, kw_ref, vsT_ref, vwT_ref, g_ref, bmapT_ref, o_ref,
                v_scr, selb_scr, m_scr, acc_scr, sa_scr, sb_scr, ma_scr, mb_scr, sw_scr, mw_scr, *, nb):
    qi = pl.program_id(2)
    t0 = qi * TQ
    wl = HPG * TQ
    q = q_ref[0]
    Q = jnp.concatenate([q[hh * HD:(hh + 1) * HD, :] for hh in range(HPG)], axis=1)
    tok1 = t0 + lax.broadcasted_iota(jnp.int32, (1, TQ), 1)
    tok = jnp.concatenate([tok1] * HPG, axis=1)

    def lanes8(x):
        return jnp.concatenate([x] * HPG, axis=1)

    nwt = WINDOW // TQ + 1
    kt0 = jnp.maximum(qi - WINDOW // TQ, 0)
    kw0 = pl.multiple_of(kt0 * TQ, TQ)
    sw = _dot(kw_ref[0, 0, pl.ds(kw0, nwt * TQ), :], Q)
    diff = tok1 - (kw0 + lax.broadcasted_iota(jnp.int32, (nwt * TQ, 1), 0))
    sw = sw + lanes8(jnp.where((diff >= 0) & (diff < WINDOW), 0.0, NEG))
    sw_scr[...] = sw
    mw_scr[...] = jnp.max(sw, axis=0, keepdims=True)

    ncp = kc_ref.shape[2]
    sc = _dot(kc_ref[0, 0], Q)
    n_idx = lax.broadcasted_iota(jnp.int32, (ncp, 1), 0)
    mask_c = (n_idx * CMP_STRIDE + (CMP_BLOCK - 1)) <= tok
    sc = jnp.where(mask_c, sc, NEG)
    mc = jnp.max(sc, axis=0, keepdims=True)
    ec = jnp.exp2(sc - mc)
    has_blk = tok >= CMP_BLOCK - 1
    pc = ec * jnp.where(has_blk, 1.0 / jnp.sum(ec, axis=0, keepdims=True), 0.0)
    o_c = _dot(vcT_ref[0, 0], pc.astype(BF16))

    psum = pc[:, 0:TQ]
    for hh in range(1, HPG):
        psum = psum + pc[:, hh * TQ:(hh + 1) * TQ]
    imp = _dot_split3(bmapT_ref[...], psum)
    jb = lax.broadcasted_iota(jnp.int32, (nb, 1), 0)
    cur = jnp.right_shift(tok1, int(np.log2(SLC_BLOCK)))
    forced = (jb == 0) | (jb == cur) | (jb == cur - 1)
    v = jnp.where(forced, jnp.inf, jnp.where(jb <= cur, imp, -jnp.inf))
    v_scr[...] = v
    jbf = lax.broadcasted_iota(jnp.int32, (nb, TQ), 0)

    def rank_body(i2, rank):
        for u in range(2):
            i = 2 * i2 + u
            vi = v_scr[pl.ds(i, 1), :]
            ge = jnp.where(vi >= v, 1.0, 0.0)
            gt = jnp.where(vi > v, 1.0, 0.0)
            rank = rank + jnp.where(jbf > i, ge, gt)
        return rank

    n_live = (TQ // SLC_BLOCK) * (qi + 1)
    rank_trips = jnp.where(n_live <= SLC_TOPK, 0, jnp.minimum(n_live, nb) // 2)
    rank = lax.fori_loop(0, rank_trips, rank_body, jnp.zeros((nb, TQ), F32))
    selb_scr[...] = jnp.where(rank < min(SLC_TOPK, nb), 0.0, NEG)

    tps = KS // TQ
    bps = KS // SLC_BLOCK
    m_scr[...] = jnp.full((1, wl), NEG, F32)
    acc_scr[...] = jnp.zeros((VROWS, wl), F32)
    st_hi = qi // tps

    def scores(st, s_scr, mx_scr, bias=None):
        sb = selb_scr[pl.ds(pl.multiple_of(st * bps, bps), bps), :]
        ext = lanes8(jnp.concatenate([sb, jnp.zeros((16 - bps, TQ), F32)], axis=0)).astype(BF16)
        qa = jnp.concatenate([Q, ext, jnp.zeros((LANES - HD - 16, wl), BF16)], axis=0)
        s = _dot(ksa_ref[0, 0, pl.ds(pl.multiple_of(st * KS, KS), KS), :], qa)
        if bias is not None:
            s = s + bias
        s_scr[...] = s
        mx_scr[...] = jnp.max(s, axis=0, keepdims=True)

    def update(s_scr, mx_scr, st):
        m_old = m_scr[...]
        m_new = jnp.maximum(m_old, mx_scr[...])
        p = jnp.exp2(s_scr[...] - m_new).astype(BF16)
        vs = jnp.concatenate([vsT_ref[0, st * tps + j] for j in range(tps)], axis=1)
        acc_scr[...] = acc_scr[...] * jnp.exp2(m_old - m_new) + _dot(vs, p)
        m_scr[...] = m_new

    kpos = st_hi * KS + lax.broadcasted_iota(jnp.int32, (KS, 1), 0)
    scores(st_hi, sa_scr, ma_scr, lanes8(jnp.where(kpos <= tok1, 0.0, NEG)))

    pw = jnp.exp2(sw_scr[...] - mw_scr[...]).astype(BF16)
    vwin = jnp.concatenate([vwT_ref[0, kt0 + j] for j in range(nwt)], axis=1)
    ow = _dot(vwin, pw)
    o_w = ow[:HD] / ow[HD:HD + 1]

    def pair_body(j, carry):
        ta = st_hi - 2 * j
        scores(ta - 1, sb_scr, mb_scr)
        update(sa_scr, ma_scr, ta)
        scores(ta - 2, sa_scr, ma_scr)
        update(sb_scr, mb_scr, ta - 1)
        return carry

    n_pairs = st_hi // 2
    lax.fori_loop(0, n_pairs, pair_body, 0)

    @pl.when(st_hi - 2 * n_pairs == 1)
    def _():
        scores(0, sb_scr, mb_scr)
        update(sa_scr, ma_scr, 1)
        update(sb_scr, mb_scr, 0)

    @pl.when(st_hi - 2 * n_pairs == 0)
    def _():
        update(sa_scr, ma_scr, 0)

    o_s = acc_scr[0:HD, :] / acc_scr[HD:HD + 1, :]

    gts = g_ref[0, 0]
    for hh in range(HPG):
        sl = slice(hh * TQ, (hh + 1) * TQ)
        o = (gts[hh:hh + 1] * o_c[:, sl] + gts[HPG + hh:HPG + hh + 1] * o_s[:, sl]
             + gts[2 * HPG + hh:2 * HPG + hh + 1] * o_w[:, sl])
        o_ref[0, hh * HD:(hh + 1) * HD, :] = o.astype(BF16)


def _nsa(qT, kc, vcT, ksa, kn, vT, gT, bmapT, bsz, seq):
    nq = seq // TQ
    nb = seq // SLC_BLOCK
    ncp = kc.shape[2]
    wl = HPG * TQ
    gT4 = gT.reshape(bsz, N_GROUPS, 3 * HPG, seq)
    assert seq % KS == 0 and seq >= WINDOW + TQ and KS // SLC_BLOCK == 8
    return pl.pallas_call(
        functools.partial(_nsa_kernel, nb=nb),
        out_shape=jax.ShapeDtypeStruct((bsz, N_HEADS * HD, seq), BF16),
        grid=(bsz, N_GROUPS, nq),
        in_specs=[pl.BlockSpec((1, HPG * HD, TQ), lambda b, g, i: (b, g, i)),
                  pl.BlockSpec((1, 1, ncp, HD), lambda b, g, i: (b, g, 0, 0)),
                  pl.BlockSpec((1, 1, HD, ncp), lambda b, g, i: (b, g, 0, 0)),
                  pl.BlockSpec((1, 1, seq, LANES), lambda b, g, i: (b, g, 0, 0)),
                  pl.BlockSpec((1, 1, seq, HD), lambda b, g, i: (b, 2 + g, 0, 0)),
                  pl.BlockSpec((1, nq, VROWS, LANES), lambda b, g, i: (b, 0, g, 0)),
                  pl.BlockSpec((1, nq, VROWS, LANES), lambda b, g, i: (b, 0, 2 + g, 0)),
                  pl.BlockSpec((1, 1, 3 * HPG, TQ), lambda b, g, i: (b, g, 0, i)),
                  pl.BlockSpec(bmapT.shape, lambda b, g, i: (0, 0))],
        out_specs=pl.BlockSpec((1, HPG * HD, TQ), lambda b, g, i: (b, g, i)),
        scratch_shapes=[pltpu.VMEM((nb, TQ), F32),
                        pltpu.VMEM((nb, TQ), F32),
                        pltpu.VMEM((1, wl), F32),
                        pltpu.VMEM((VROWS, wl), F32),
                        pltpu.VMEM((KS, wl), F32),
                        pltpu.VMEM((KS, wl), F32),
                        pltpu.VMEM((1, wl), F32),
                        pltpu.VMEM((1, wl), F32),
                        pltpu.VMEM((WINDOW + TQ, wl), F32),
                        pltpu.VMEM((1, wl), F32)],
        compiler_params=_cparams(("parallel", "parallel", "arbitrary")),
        name="nsa_attention",
    )(qT, kc, vcT, ksa, kn, vT, vT, gT4, bmapT)


def _ssd_kernel(z_ref, xs_ref, bc_ref, dt_ref, cwx_ref, cwb_ref, cbx_ref, cbb_ref, dtb_ref, alog_ref,
                dsk_ref, nw_ref, o_ref, xpx_scr, xpb_scr, st_scr):
    c = pl.program_id(1)
    L = SSM_CHUNK
    pad = SSM_PAD
    npair = SSM_HEADS // 2
    ppg = npair // SSM_GROUPS

    @pl.when(c == 0)
    def _():
        xpx_scr[0:pad, :] = jnp.zeros((pad, xpx_scr.shape[1]), BF16)
        xpb_scr[0:pad, :] = jnp.zeros((pad, xpb_scr.shape[1]), BF16)
        st_scr[...] = jnp.zeros(st_scr.shape, F32)

    nshift = SSM_CONV - 1
    srow = lax.broadcasted_iota(jnp.int32, (nshift * L, pad + L), 0)
    scol = lax.broadcasted_iota(jnp.int32, (nshift * L, pad + L), 1)
    shift = jnp.where(scol == (srow & (L - 1)) + jnp.right_shift(srow, int(np.log2(L))) + (pad - nshift),
                      1.0, 0.0).astype(BF16)

    def conv_silu(x_ref, w_ref, b_ref, scr):
        x = x_ref[...]
        scr[pad:pad + L, :] = x
        sh = _dot(shift, scr[...])
        acc = b_ref[...] + w_ref[nshift:nshift + 1, :] * x.astype(F32)
        for k in range(nshift):
            acc = acc + w_ref[k:k + 1, :] * sh[k * L:(k + 1) * L]
        scr[0:pad, :] = scr[L:L + pad, :]
        return _silu(acc)

    xs = conv_silu(xs_ref, cwx_ref, cbx_ref, xpx_scr)
    bcm = conv_silu(bc_ref, cwb_ref, cbb_ref, xpb_scr).astype(BF16)
    xs_b = xs.astype(BF16)

    dtv = dt_ref[...] + dtb_ref[...]
    dt = jnp.maximum(dtv, 0.0) + jnp.log(1.0 + jnp.exp(-jnp.abs(dtv)))
    a = -jnp.exp(alog_ref[...])
    row = lax.broadcasted_iota(jnp.int32, (L, L), 0)
    col = lax.broadcasted_iota(jnp.int32, (L, L), 1)
    causal = col <= row
    a_cum = _dot_split3(jnp.where(causal, 1.0, 0.0).astype(BF16), dt * a)
    a_cumT = a_cum.T
    dtT = dt.T
    a_last = a_cum[L - 1:L, :]
    a_lastT = a_cumT[:, L - 1:L]
    wT = dtT * jnp.exp(a_lastT - a_cumT)
    e_last = jnp.exp(a_last)
    lo = lax.broadcasted_iota(jnp.int32, (1, LANES), 1) < SSM_HD

    ys = []
    for g in range(SSM_GROUPS):
        Bg = bcm[:, g * SSM_STATE:(g + 1) * SSM_STATE]
        Cg = bcm[:, (SSM_GROUPS + g) * SSM_STATE:(SSM_GROUPS + g + 1) * SSM_STATE]
        cb = _dot_nt(Cg, Bg)
        BgT = Bg.astype(F32).T
        for jp in range(ppg):
            pj = g * ppg + jp
            h0, h1 = 2 * pj, 2 * pj + 1
            xp = xs_b[:, pj * LANES:(pj + 1) * LANES]

            def head_mats(h):
                ecol = jnp.broadcast_to(a_cum[:, h:h + 1], (L, L))
                dec = jnp.exp(jnp.where(causal, ecol - a_cumT[h:h + 1, :], NEG))
                gm = (cb * dec * dtT[h:h + 1, :]).astype(BF16)
                bw = (BgT * wT[h:h + 1, :]).astype(BF16)
                return gm, bw, jnp.exp(ecol)

            g0, b0, e0 = head_mats(h0)
            g1, b1, e1 = head_mats(h1)
            y_d = jnp.where(lo, _dot(g0, xp), _dot(g1, xp))
            st_prev = st_scr[pj]
            y_o = _dot(Cg, st_prev.astype(BF16)) * jnp.where(lo, e0, e1)
            st_new = jnp.where(lo, _dot(b0, xp), _dot(b1, xp))
            cdec = jnp.where(lo, jnp.broadcast_to(e_last[:, h0:h0 + 1], (1, LANES)),
                             jnp.broadcast_to(e_last[:, h1:h1 + 1], (1, LANES)))
            st_scr[pj] = st_prev * cdec + st_new
            sl = slice(pj * LANES, (pj + 1) * LANES)
            ys.append(y_d + y_o + dsk_ref[:, sl] * xs[:, sl])
    y = jnp.concatenate(ys, axis=1)
    y = y * _silu(z_ref[...].astype(F32))
    gw = SSM_INNER // SSM_GROUPS
    for g in range(SSM_GROUPS):
        yg = y[:, g * gw:(g + 1) * gw]
        ms = jnp.mean(yg * yg, axis=-1, keepdims=True)
        o_ref[:, g * gw:(g + 1) * gw] = (yg * lax.rsqrt(ms + EPS) * nw_ref[:, g * gw:(g + 1) * gw]).astype(BF16)


def _ssd(zxm, dt, cwx, cwb, cbx, cbb, dtb, alog, dsk, nw, bsz, seq):
    L = SSM_CHUNK
    nchunk = seq // L
    nbc = 2 * SSM_GROUPS * SSM_STATE
    full = lambda a: pl.BlockSpec(a.shape, lambda b, c: (0,) * a.ndim)
    rowblk = lambda width, idx: pl.BlockSpec((L, width), lambda b, c: (b * nchunk + c, idx))
    return pl.pallas_call(
        _ssd_kernel,
        out_shape=jax.ShapeDtypeStruct((bsz * seq, SSM_INNER), BF16),
        grid=(bsz, nchunk),
        in_specs=[rowblk(SSM_INNER, 0), rowblk(SSM_INNER, 2), rowblk(nbc, 6), rowblk(LANES, 0),
                  full(cwx), full(cwb), full(cbx), full(cbb), full(dtb), full(alog), full(dsk), full(nw)],
        out_specs=pl.BlockSpec((L, SSM_INNER), lambda b, c: (b * nchunk + c, 0)),
        scratch_shapes=[pltpu.VMEM((L + SSM_PAD, SSM_INNER), BF16),
                        pltpu.VMEM((L + SSM_PAD, nbc), BF16),
                        pltpu.VMEM((SSM_HEADS // 2, SSM_STATE, LANES), F32)],
        compiler_params=_cparams(("parallel", "arbitrary")),
        name="ssd",
    )(zxm, zxm, zxm, dt, cwx, cwb, cbx, cbb, dtb, alog, dsk, nw)


def _outproj_kernel(oT_ref, os_ref, ga_ref, gs_ref, x_ref, mod_ref, wa_ref, ws_ref, wo_ref, o_ref):
    ya = _dot_tn(oT_ref[0], wa_ref[...])
    ys = _dot(os_ref[...], ws_ref[...])
    m = ga_ref[...].astype(F32) * ya + gs_ref[...].astype(F32) * ys
    r = _dot(m.astype(BF16), wo_ref[...])
    o_ref[...] = x_ref[...] + mod_ref[0][2:3] * r


def _outproj(oT, ossm, zxm, x2d, mod, wa, ws, wo, bsz, seq):
    d = x2d.shape[1]
    tm = 512
    tpb = seq // tm
    full = lambda a: pl.BlockSpec(a.shape, lambda i: (0,) * a.ndim)
    return pl.pallas_call(
        _outproj_kernel,
        out_shape=jax.ShapeDtypeStruct((bsz * seq, d), F32),
        grid=(bsz * tpb,),
        in_specs=[pl.BlockSpec((1, N_HEADS * HD, tm), lambda i: (i // tpb, 0, i % tpb)),
                  pl.BlockSpec((tm, SSM_INNER), lambda i: (i, 0)),
                  pl.BlockSpec((tm, d), lambda i: (i, 2)),
                  pl.BlockSpec((tm, d), lambda i: (i, 3)),
                  pl.BlockSpec((tm, d), lambda i: (i, 0)),
                  pl.BlockSpec((1, 6, d), lambda i: (i // tpb, 0, 0)),
                  full(wa), full(ws), full(wo)],
        out_specs=pl.BlockSpec((tm, d), lambda i: (i, 0)),
        compiler_params=_cparams(("parallel",)),
        name="outproj",
    )(oT, ossm, zxm, zxm, x2d, mod, wa, ws, wo)


def _ffn_kernel(x_ref, mod_ref, n2_ref, nf_ref, wg_ref, wu_ref, wd_ref, o_ref, *, final_norm):
    x = x_ref[...]
    mod = mod_ref[0]
    h = _norm_mod(x, n2_ref[...], mod[4:5], mod[3:4]).astype(BF16)
    a = _silu(_dot(h, wg_ref[...])) * _dot(h, wu_ref[...])
    r = _dot(a.astype(BF16), wd_ref[...])
    x2 = x + mod[5:6] * r
    if final_norm:
        ms = jnp.mean(x2 * x2, axis=-1, keepdims=True)
        x2 = x2 * lax.rsqrt(ms + EPS) * nf_ref[...]
    o_ref[...] = x2


def _ffn(x1, mod, n2, nf, wg, wu, wd, final_norm, bsz, seq):
    d = x1.shape[1]
    tm = 512
    tpb = seq // tm
    const = lambda a: pl.BlockSpec(a.shape, lambda i: (0,) * a.ndim, pipeline_mode=pl.Buffered(1))
    return pl.pallas_call(
        functools.partial(_ffn_kernel, final_norm=final_norm),
        out_shape=jax.ShapeDtypeStruct((bsz * seq, d), F32),
        grid=(bsz * tpb,),
        in_specs=[pl.BlockSpec((tm, d), lambda i: (i, 0)),
                  pl.BlockSpec((1, 6, d), lambda i: (i // tpb, 0, 0)),
                  const(n2), const(nf), const(wg), const(wu), const(wd)],
        out_specs=pl.BlockSpec((tm, d), lambda i: (i, 0)),
        compiler_params=_cparams(("parallel",)),
        name="ffn",
    )(x1, mod, n2, nf, wg, wu, wd)


def _rotary_tables(seq):
    inv_freq = jnp.asarray(np.power(ROPE_THETA, -np.arange(0, 2 * ROT_HALF, 2) / (2 * ROT_HALF)).astype(np.float32))
    ang = jnp.arange(seq).astype(F32)[:, None] * inv_freq[None, :]
    cos, sin = jnp.cos(ang), jnp.sin(ang)
    ones = jnp.ones((seq, HD - 2 * ROT_HALF), F32)
    zeros8 = jnp.zeros((seq, ROT_HALF), F32)
    zeros = jnp.zeros((seq, HD - 2 * ROT_HALF), F32)
    cosK = jnp.concatenate([cos, cos, ones], axis=1)
    sinA = jnp.concatenate([-sin, zeros8, zeros], axis=1)
    sinB = jnp.concatenate([zeros8, sin, zeros], axis=1)
    rep = LANES // HD
    tile = lambda t: jnp.concatenate([t] * rep, axis=1)
    return cos.T, sin.T, tile(cosK), tile(sinA), tile(sinB)


def _selection_block_map_T(seq):
    nc = (seq - CMP_BLOCK) // CMP_STRIDE + 1
    nb = seq // SLC_BLOCK
    cs = np.arange(nc) * CMP_STRIDE
    bs = np.arange(nb) * SLC_BLOCK
    ov = np.minimum(cs[:, None] + CMP_BLOCK, bs[None, :] + SLC_BLOCK) - np.maximum(cs[:, None], bs[None, :])
    bmap = (np.clip(ov, 0, None) / CMP_BLOCK).astype(np.float32)
    out = np.zeros((nb, seq // CMP_STRIDE), np.float32)
    out[:, :nc] = bmap.T
    return jnp.asarray(out)


def kernel(x, c, w_ada, b_ada, norm1_w, w_in, cmp_pe_k, cmp_w1_k, cmp_w2_k, cmp_pe_v, cmp_w1_v, cmp_w2_v,
           conv_w, conv_b, dt_bias, a_log, d_skip, ssm_norm_w, w_attn_out, w_ssm_out, w_o, norm2_w,
           w_gate, w_up, w_down, norm_f_w):
    bsz, seq, d = x.shape
    depth = w_ada.shape[0]
    q_w = N_HEADS * HD
    kv_w = N_GROUPS * HD
    xbc_w = SSM_INNER + 2 * SSM_GROUPS * SSM_STATE
    offs = np.cumsum([q_w, 6 * kv_w, 3 * N_HEADS, SSM_INNER, xbc_w, SSM_HEADS])
    tabs = _rotary_tables(seq)
    bmapT = _selection_block_map_T(seq)
    x2d = x.reshape(bsz * seq, d)
    for l in range(depth):
        mod = _ada_mod(c, w_ada[l], b_ada[l]).reshape(bsz, 6, d)
        wi = w_in[l]
        wq = wi[:, :offs[0]]
        wkv = wi[:, offs[0]:offs[1]].reshape(d, 6, N_GROUPS * HD)
        wg = wi[:, offs[1]:offs[2]].reshape(d, N_GROUPS, HPG, 3)
        wz = wi[:, offs[2]:offs[3]]
        wxbc = wi[:, offs[3]:offs[4]]
        wdt = wi[:, offs[4]:offs[5]]
        wmg = wi[:, offs[5]:]
        wqT = wq.T.astype(BF16)
        zcol = jnp.zeros((d, HD), wi.dtype)
        wkn = jnp.concatenate([wkv[:, 2, :HD], zcol, wkv[:, 2, HD:], zcol, wkv[:, 0], wkv[:, 4], wkv[:, 1]],
                              axis=1).astype(BF16)
        wvT = jnp.concatenate([wkv[:, 3], wkv[:, 5]], axis=1).T.astype(BF16)
        wgT = jnp.transpose(wg, (1, 3, 2, 0)).reshape(3 * N_HEADS, d).astype(BF16)
        wdtp = jnp.pad(wdt, ((0, 0), (0, LANES - SSM_HEADS))).astype(BF16)
        wbig = jnp.concatenate([wz, wmg, wxbc], axis=1).astype(BF16)
        nw1 = norm1_w[l].reshape(1, d)

        qT, ksa, kn, vT, gT, dt = _inproj_attn(x2d, mod, nw1, wqT, wkn, wvT, wgT, wdtp, tabs, bsz, seq)
        zxm = _inproj_big(x2d, mod, nw1, wbig, bsz, seq)

        kn16 = kn.reshape(bsz, 6, seq // CMP_STRIDE, CMP_STRIDE * HD)
        kc = _compress(kn16, 0, cmp_pe_k[l], cmp_w1_k[l], cmp_w2_k[l], False, bsz, seq)
        vcT = _compress(kn16, 4, cmp_pe_v[l], cmp_w1_v[l], cmp_w2_v[l], True, bsz, seq)
        oT = _nsa(qT, kc, vcT, ksa, kn, vT, gT, bmapT.astype(BF16), bsz, seq)

        cw = conv_w[l]
        cb = conv_b[l].reshape(1, xbc_w)
        pad_h = lambda v: jnp.pad(v.reshape(1, SSM_HEADS), ((0, 0), (0, LANES - SSM_HEADS)))
        dsk = jnp.repeat(d_skip[l], SSM_HD).reshape(1, SSM_INNER)
        ossm = _ssd(zxm, dt, cw[:, :SSM_INNER], cw[:, SSM_INNER:], cb[:, :SSM_INNER], cb[:, SSM_INNER:],
                    pad_h(dt_bias[l]), pad_h(a_log[l]), dsk, ssm_norm_w[l].reshape(1, SSM_INNER), bsz, seq)

        x1 = _outproj(oT, ossm, zxm, x2d, mod, w_attn_out[l].astype(BF16), w_ssm_out[l].astype(BF16),
                      w_o[l].astype(BF16), bsz, seq)
        x2d = _ffn(x1, mod, norm2_w[l].reshape(1, d), norm_f_w.reshape(1, d), w_gate[l].astype(BF16),
                   w_up[l].astype(BF16), w_down[l].astype(BF16), l == depth - 1, bsz, seq)
    return x2d.reshape(bsz, seq, d)
```

```python
import functools

import numpy as np
import jax
import jax.numpy as jnp
from jax import lax
from jax.experimental import pallas as pl
from jax.experimental.pallas import tpu as pltpu

F32 = jnp.float32
BF16 = jnp.bfloat16
HIGHEST = lax.Precision.HIGHEST

EPS = 1e-6
NEG = -1e30

N_HEADS = 16
N_GROUPS = 2
HPG = N_HEADS // N_GROUPS
HD = 64
ROT_HALF = 8
ROPE_THETA = 500000.0
CMP_BLOCK = 32
CMP_STRIDE = 16
CMP_HIDDEN = 256
SLC_BLOCK = 64
SLC_TOPK = 16
WINDOW = 512

SSM_HEADS = 32
SSM_HD = 64
SSM_GROUPS = 4
SSM_STATE = 128
SSM_CONV = 4
SSM_CHUNK = 128
SSM_INNER = SSM_HEADS * SSM_HD
SSM_PAD = 16

TQ = 128
NSA_TILES_PER_STEP = 2
SSD_CHUNKS_PER_STEP = 2
KS = 512
LANES = 128
VROWS = HD + 16
LOG2E = float(np.log2(np.e))
VMEM_LIMIT = 56 * 1024 * 1024


def _cparams(sem, vmem=VMEM_LIMIT):
    return pltpu.CompilerParams(dimension_semantics=sem, vmem_limit_bytes=vmem)


def _sigmoid(x):
    return 0.5 + 0.5 * jnp.tanh(0.5 * x)


def _silu(x):
    hx = 0.5 * x
    return hx + hx * jnp.tanh(hx)


def _norm_mod(x, w, sc, sh):
    ms = jnp.mean(x * x, axis=-1, keepdims=True)
    return (x * lax.rsqrt(ms + EPS) * w) * (1.0 + sc) + sh


def _dot(a, b):
    return jnp.dot(a, b, preferred_element_type=F32)


def _dot_split3(a, x):
    hi = x.astype(BF16)
    r1 = x - hi.astype(F32)
    mid = r1.astype(BF16)
    lo = (r1 - mid.astype(F32)).astype(BF16)
    return _dot(a, hi) + _dot(a, mid) + _dot(a, lo)


def _dot_nt(a, b):
    return lax.dot_general(a, b, (((1,), (1,)), ((), ())), preferred_element_type=F32)


def _dot_tn(a, b):
    return lax.dot_general(a, b, (((0,), (0,)), ((), ())), preferred_element_type=F32)


def _ada_kernel(c_ref, w_ref, b_ref, o_ref):
    c = c_ref[...]
    o_ref[...] = jnp.dot(_silu(c), w_ref[...], precision=HIGHEST, preferred_element_type=F32) + b_ref[...]


def _ada_mod(c, w, b):
    bsz, d = c.shape
    n = w.shape[1]
    tn = 1024
    return pl.pallas_call(
        _ada_kernel,
        out_shape=jax.ShapeDtypeStruct((bsz, n), F32),
        grid=(n // tn,),
        in_specs=[pl.BlockSpec((bsz, d), lambda j: (0, 0)),
                  pl.BlockSpec((d, tn), lambda j: (0, j)),
                  pl.BlockSpec((1, tn), lambda j: (0, j))],
        out_specs=pl.BlockSpec((bsz, tn), lambda j: (0, j)),
        compiler_params=_cparams(("parallel",)),
        name="ada_mod",
    )(c, w, b.reshape(1, n))


def _inproj_attn_kernel(x_ref, mod_ref, nw_ref, wqT_ref, wkn_ref, wvT_ref, wgT_ref, wdt_ref,
                        cosT_ref, sinT_ref, cosK_ref, sinA_ref, sinB_ref,
                        qT_ref, ksa_ref, kn_ref, vT_ref, gT_ref, dt_ref):
    tm = x_ref.shape[0]
    mod = mod_ref[0]
    h = _norm_mod(x_ref[...], nw_ref[...], mod[1:2], mod[0:1]).astype(BF16)

    qT = _dot_nt(wqT_ref[...], h)
    cT = cosT_ref[...]
    sT = sinT_ref[...]
    scale = HD ** -0.5 * LOG2E
    for hh in range(N_HEADS):
        r0 = hh * HD
        x1 = qT[r0:r0 + ROT_HALF]
        x2 = qT[r0 + ROT_HALF:r0 + 2 * ROT_HALF]
        blk = jnp.concatenate([x1 * cT - x2 * sT, x2 * cT + x1 * sT, qT[r0 + 2 * ROT_HALF:r0 + HD]], axis=0)
        blk = (blk * scale).astype(BF16)
        for j in range(tm // TQ):
            qT_ref[0, j, r0:r0 + HD, :] = blk[:, j * TQ:(j + 1) * TQ]

    kn = _dot(h, wkn_ref[...])
    nrot = 8 * HD
    kk = kn[:, :nrot]
    reps = nrot // LANES
    cK = jnp.concatenate([cosK_ref[...]] * reps, axis=1)
    sA = jnp.concatenate([sinA_ref[...]] * reps, axis=1)
    sB = jnp.concatenate([sinB_ref[...]] * reps, axis=1)
    kk = kk * cK + pltpu.roll(kk, nrot - ROT_HALF, 1) * sA + pltpu.roll(kk, ROT_HALF, 1) * sB
    row = lax.broadcasted_iota(jnp.int32, (tm, LANES), 0)
    lane = lax.broadcasted_iota(jnp.int32, (tm, LANES), 1)
    bps = KS // SLC_BLOCK
    blk = jnp.right_shift(row, int(np.log2(SLC_BLOCK)))
    onehot = jnp.where(lane == HD + (blk & (bps - 1)), 1.0, 0.0)
    for g in range(N_GROUPS):
        ksa_ref[0, g] = (kk[:, g * LANES:(g + 1) * LANES] + onehot).astype(BF16)
    for i in range(4):
        kn_ref[0, i] = kk[:, (4 + i) * HD:(5 + i) * HD].astype(BF16)
    for i in range(2):
        kn_ref[0, 4 + i] = kn[:, nrot + i * HD:nrot + (i + 1) * HD].astype(BF16)

    vT = _dot_nt(wvT_ref[...], h).astype(BF16)
    ones = jnp.ones((VROWS - HD, LANES), BF16)
    for j in range(tm // LANES):
        for r in range(4):
            vT_ref[0, j, r * VROWS:r * VROWS + HD, :] = vT[r * HD:(r + 1) * HD, j * LANES:(j + 1) * LANES]
            vT_ref[0, j, r * VROWS + HD:(r + 1) * VROWS, :] = ones

    gates = _sigmoid(_dot_nt(wgT_ref[...], h))
    for j in range(tm // TQ):
        for g in range(N_GROUPS):
            gT_ref[0, j, g] = gates[g * 3 * HPG:(g + 1) * 3 * HPG, j * TQ:(j + 1) * TQ]
    dt_ref[...] = _dot(h, wdt_ref[...])


def _inproj_attn(x2d, mod, nw, wqT, wkn, wvT, wgT, wdt, tabs, bsz, seq):
    d = x2d.shape[1]
    tm = KS
    tpb = seq // tm
    cosT, sinT, cosK, sinA, sinB = tabs
    full = lambda a: pl.BlockSpec(a.shape, lambda i: (0,) * a.ndim)
    out_shape = (
        jax.ShapeDtypeStruct((bsz, seq // TQ, N_HEADS * HD, TQ), BF16),
        jax.ShapeDtypeStruct((bsz, N_GROUPS, seq, LANES), BF16),
        jax.ShapeDtypeStruct((bsz, 6, seq, HD), BF16),
        jax.ShapeDtypeStruct((bsz, seq // LANES, 4 * VROWS, LANES), BF16),
        jax.ShapeDtypeStruct((bsz, seq // TQ, N_GROUPS, 3 * HPG, TQ), F32),
        jax.ShapeDtypeStruct((bsz * seq, LANES), F32),
    )
    return pl.pallas_call(
        _inproj_attn_kernel,
        out_shape=out_shape,
        grid=(bsz * tpb,),
        in_specs=[pl.BlockSpec((tm, d), lambda i: (i, 0)),
                  pl.BlockSpec((1, 6, d), lambda i: (i // tpb, 0, 0)),
                  full(nw), full(wqT), full(wkn), full(wvT), full(wgT), full(wdt),
                  pl.BlockSpec((ROT_HALF, tm), lambda i: (0, i % tpb)),
                  pl.BlockSpec((ROT_HALF, tm), lambda i: (0, i % tpb)),
                  pl.BlockSpec((tm, LANES), lambda i: (i % tpb, 0)),
                  pl.BlockSpec((tm, LANES), lambda i: (i % tpb, 0)),
                  pl.BlockSpec((tm, LANES), lambda i: (i % tpb, 0))],
        out_specs=(pl.BlockSpec((1, tm // TQ, N_HEADS * HD, TQ), lambda i: (i // tpb, i % tpb, 0, 0)),
                   pl.BlockSpec((1, N_GROUPS, tm, LANES), lambda i: (i // tpb, 0, i % tpb, 0)),
                   pl.BlockSpec((1, 6, tm, HD), lambda i: (i // tpb, 0, i % tpb, 0)),
                   pl.BlockSpec((1, tm // LANES, 4 * VROWS, LANES), lambda i: (i // tpb, i % tpb, 0, 0)),
                   pl.BlockSpec((1, tm // TQ, N_GROUPS, 3 * HPG, TQ), lambda i: (i // tpb, i % tpb, 0, 0, 0)),
                   pl.BlockSpec((tm, LANES), lambda i: (i, 0))),
        compiler_params=_cparams(("parallel",)),
        name="inproj_attn",
    )(x2d, mod, nw, wqT, wkn, wvT, wgT, wdt, cosT, sinT, cosK, sinA, sinB)


def _inproj_big_kernel(x_ref, mod_ref, nw_ref, w_ref, o_ref, h_scr, *, gate_lo, gate_hi):
    j = pl.program_id(1)

    @pl.when(j == 0)
    def _():
        mod = mod_ref[0]
        h_scr[...] = _norm_mod(x_ref[...], nw_ref[...], mod[1:2], mod[0:1]).astype(BF16)

    acc = _dot(h_scr[...], w_ref[...])
    tn = acc.shape[1]
    col = j * tn + lax.broadcasted_iota(jnp.int32, (1, tn), 1)
    is_gate = (col >= gate_lo) & (col < gate_hi)
    o_ref[...] = jnp.where(is_gate, _sigmoid(acc), acc).astype(BF16)


def _inproj_big(x2d, mod, nw, wbig, bsz, seq):
    d = x2d.shape[1]
    n = wbig.shape[1]
    tm, tn = 1024, n // 4
    assert n % 4 == 0 and tn % LANES == 0
    tpb = seq // tm
    return pl.pallas_call(
        functools.partial(_inproj_big_kernel, gate_lo=SSM_INNER, gate_hi=SSM_INNER + 2 * d),
        out_shape=jax.ShapeDtypeStruct((bsz * seq, n), BF16),
        grid=(bsz * tpb, n // tn),
        in_specs=[pl.BlockSpec((tm, d), lambda i, j: (i, 0)),
                  pl.BlockSpec((1, 6, d), lambda i, j: (i // tpb, 0, 0)),
                  pl.BlockSpec((1, d), lambda i, j: (0, 0)),
                  pl.BlockSpec((d, tn), lambda i, j: (0, j))],
        out_specs=pl.BlockSpec((tm, tn), lambda i, j: (i, j)),
        scratch_shapes=[pltpu.VMEM((tm, d), BF16)],
        compiler_params=_cparams(("parallel", "arbitrary")),
        name="inproj_big",
    )(x2d, mod, nw, wbig)


def _compress_kernel(x_ref, pea_ref, peb_ref, w1a_ref, w1b_ref, w2_ref, o_ref, *, transpose_out, nc):
    x = x_ref[0, 0].astype(F32)
    nrow = x.shape[0]
    ya = _dot((x + pea_ref[...]).astype(BF16), w1a_ref[...])
    yb = _dot((x + peb_ref[...]).astype(BF16), w1b_ref[...])
    hp = ya + pltpu.roll(yb, nrow - 1, 0)
    row = lax.broadcasted_iota(jnp.int32, hp.shape, 0)
    hh = jnp.where(row < nc, _silu(hp), 0.0).astype(BF16)
    if transpose_out:
        o_ref[0, 0] = _dot_nt(w2_ref[...], hh).astype(BF16)
    else:
        o_ref[0, 0] = _dot(hh, w2_ref[...]).astype(BF16)


def _compress(kn16, slot0, pe, w1, w2, transpose_out, bsz, seq):
    nrow = seq // CMP_STRIDE
    nc = (seq - CMP_BLOCK) // CMP_STRIDE + 1
    half = CMP_STRIDE * HD
    pea = pe[:CMP_STRIDE].reshape(1, half)
    peb = pe[CMP_STRIDE:].reshape(1, half)
    w1a = w1[:half].astype(BF16)
    w1b = w1[half:].astype(BF16)
    if transpose_out:
        w2p = w2.T.astype(BF16)
        oshape, oblock = (bsz, N_GROUPS, HD, nrow), (1, 1, HD, nrow)
    else:
        w2p = w2.astype(BF16)
        oshape, oblock = (bsz, N_GROUPS, nrow, HD), (1, 1, nrow, HD)
    full = lambda a: pl.BlockSpec(a.shape, lambda b, g: (0,) * a.ndim)
    return pl.pallas_call(
        functools.partial(_compress_kernel, transpose_out=transpose_out, nc=nc),
        out_shape=jax.ShapeDtypeStruct(oshape, BF16),
        grid=(bsz, N_GROUPS),
        in_specs=[pl.BlockSpec((1, 1, nrow, half), lambda b, g: (b, slot0 + g, 0, 0)),
                  full(pea), full(peb), full(w1a), full(w1b), full(w2p)],
        out_specs=pl.BlockSpec(oblock, lambda b, g: (b, g, 0, 0)),
        compiler_params=_cparams(("parallel", "parallel")),
        name="compress_v" if transpose_out else "compress_k",
    )(kn16, pea, peb, w1a, w1b, w2p)


def _nsa_kernel(*refs, nb, tiles):
    def tile(u, carry):
        _nsa_tile(u, pl.program_id(2) * tiles + u, *refs, nb=nb)
        return carry

    lax.fori_loop(0, tiles, tile, 0)


def _nsa_tile(u, qi, q_ref, kc_ref, vcT_ref, ksa_ref, kw_ref, vsT_ref, vwT_ref, g_ref, bmapT_ref, o_ref,
              v_scr, selb_scr, m_scr, acc_scr, sa_scr, sb_scr, ma_scr, mb_scr, sw_scr, mw_scr, *, nb):
    t0 = qi * TQ
    wl = HPG * TQ
    q = q_ref[0, u]
    Q = jnp.concatenate([q[hh * HD:(hh + 1) * HD, :] for hh in range(HPG)], axis=1)
    tok1 = t0 + lax.broadcasted_iota(jnp.int32, (1, TQ), 1)
    tok = jnp.concatenate([tok1] * HPG, axis=1)

    def lanes8(x):
        return jnp.concatenate([x] * HPG, axis=1)

    nwt = WINDOW // TQ + 1
    kt0 = jnp.maximum(qi - WINDOW // TQ, 0)
    kw0 = pl.multiple_of(kt0 * TQ, TQ)
    sw = _dot(kw_ref[0, 0, pl.ds(kw0, nwt * TQ), :], Q)
    diff = tok1 - (kw0 + lax.broadcasted_iota(jnp.int32, (nwt * TQ, 1), 0))
    sw = sw + lanes8(jnp.where((diff >= 0) & (diff < WINDOW), 0.0, NEG))
    sw_scr[...] = sw
    mw_scr[...] = jnp.max(sw, axis=0, keepdims=True)

    ncp = kc_ref.shape[2]
    sc = _dot(kc_ref[0, 0], Q)
    n_idx = lax.broadcasted_iota(jnp.int32, (ncp, 1), 0)
    mask_c = (n_idx * CMP_STRIDE + (CMP_BLOCK - 1)) <= tok
    sc = jnp.where(mask_c, sc, NEG)
    mc = jnp.max(sc, axis=0, keepdims=True)
    ec = jnp.exp2(sc - mc)
    has_blk = tok >= CMP_BLOCK - 1
    pc = ec * jnp.where(has_blk, 1.0 / jnp.sum(ec, axis=0, keepdims=True), 0.0)
    o_c = _dot(vcT_ref[0, 0], pc.astype(BF16))

    psum = pc[:, 0:TQ]
    for hh in range(1, HPG):
        psum = psum + pc[:, hh * TQ:(hh + 1) * TQ]
    imp = _dot_split3(bmapT_ref[...], psum)
    jb = lax.broadcasted_iota(jnp.int32, (nb, 1), 0)
    cur = jnp.right_shift(tok1, int(np.log2(SLC_BLOCK)))
    forced = (jb == 0) | (jb == cur) | (jb == cur - 1)
    v = jnp.where(forced, jnp.inf, jnp.where(jb <= cur, imp, -jnp.inf))
    v_scr[...] = v
    jbf = lax.broadcasted_iota(jnp.int32, (nb, TQ), 0)

    def rank_body(i2, rank):
        for u in range(2):
            i = 2 * i2 + u
            vi = v_scr[pl.ds(i, 1), :]
            ge = jnp.where(vi >= v, 1.0, 0.0)
            gt = jnp.where(vi > v, 1.0, 0.0)
            rank = rank + jnp.where(jbf > i, ge, gt)
        return rank

    n_live = (TQ // SLC_BLOCK) * (qi + 1)
    rank_trips = jnp.where(n_live <= SLC_TOPK, 0, jnp.minimum(n_live, nb) // 2)
    rank = lax.fori_loop(0, rank_trips, rank_body, jnp.zeros((nb, TQ), F32))
    selb_scr[...] = jnp.where(rank < min(SLC_TOPK, nb), 0.0, NEG)

    tps = KS // TQ
    bps = KS // SLC_BLOCK
    m_scr[...] = jnp.full((1, wl), NEG, F32)
    acc_scr[...] = jnp.zeros((VROWS, wl), F32)
    st_hi = qi // tps

    def scores(st, s_scr, mx_scr, bias=None):
        sb = selb_scr[pl.ds(pl.multiple_of(st * bps, bps), bps), :]
        ext = lanes8(jnp.concatenate([sb, jnp.zeros((16 - bps, TQ), F32)], axis=0)).astype(BF16)
        qa = jnp.concatenate([Q, ext, jnp.zeros((LANES - HD - 16, wl), BF16)], axis=0)
        s = _dot(ksa_ref[0, 0, pl.ds(pl.multiple_of(st * KS, KS), KS), :], qa)
        if bias is not None:
            s = s + bias
        s_scr[...] = s
        mx_scr[...] = jnp.max(s, axis=0, keepdims=True)

    def update(s_scr, mx_scr, st):
        m_old = m_scr[...]
        m_new = jnp.maximum(m_old, mx_scr[...])
        p = jnp.exp2(s_scr[...] - m_new).astype(BF16)
        vs = jnp.concatenate([vsT_ref[0, st * tps + j] for j in range(tps)], axis=1)
        acc_scr[...] = acc_scr[...] * jnp.exp2(m_old - m_new) + _dot(vs, p)
        m_scr[...] = m_new

    kpos = st_hi * KS + lax.broadcasted_iota(jnp.int32, (KS, 1), 0)
    scores(st_hi, sa_scr, ma_scr, lanes8(jnp.where(kpos <= tok1, 0.0, NEG)))

    pw = jnp.exp2(sw_scr[...] - mw_scr[...]).astype(BF16)
    vwin = jnp.concatenate([vwT_ref[0, kt0 + j] for j in range(nwt)], axis=1)
    ow = _dot(vwin, pw)
    o_w = ow[:HD] / ow[HD:HD + 1]

    def pair_body(j, carry):
        ta = st_hi - 2 * j
        scores(ta - 1, sb_scr, mb_scr)
        update(sa_scr, ma_scr, ta)
        scores(ta - 2, sa_scr, ma_scr)
        update(sb_scr, mb_scr, ta - 1)
        return carry

    n_pairs = st_hi // 2
    lax.fori_loop(0, n_pairs, pair_body, 0)

    @pl.when(st_hi - 2 * n_pairs == 1)
    def _():
        scores(0, sb_scr, mb_scr)
        update(sa_scr, ma_scr, 1)
        update(sb_scr, mb_scr, 0)

    @pl.when(st_hi - 2 * n_pairs == 0)
    def _():
        update(sa_scr, ma_scr, 0)

    o_s = acc_scr[0:HD, :] / acc_scr[HD:HD + 1, :]

    gts = g_ref[0, u, 0]
    for hh in range(HPG):
        sl = slice(hh * TQ, (hh + 1) * TQ)
        o = (gts[hh:hh + 1] * o_c[:, sl] + gts[HPG + hh:HPG + hh + 1] * o_s[:, sl]
             + gts[2 * HPG + hh:2 * HPG + hh + 1] * o_w[:, sl])
        o_ref[0, u, hh * HD:(hh + 1) * HD, :] = o.astype(BF16)


def _nsa(qT, kc, vcT, ksa, kn, vT, gT, bmapT, bsz, seq):
    nq = seq // TQ
    nb = seq // SLC_BLOCK
    ncp = kc.shape[2]
    wl = HPG * TQ
    tiles = NSA_TILES_PER_STEP
    assert seq % KS == 0 and seq >= WINDOW + TQ and KS // SLC_BLOCK == 8 and nq % tiles == 0
    return pl.pallas_call(
        functools.partial(_nsa_kernel, nb=nb, tiles=tiles),
        out_shape=jax.ShapeDtypeStruct((bsz, nq, N_HEADS * HD, TQ), BF16),
        grid=(bsz, N_GROUPS, nq // tiles),
        in_specs=[pl.BlockSpec((1, tiles, HPG * HD, TQ), lambda b, g, i: (b, i, g, 0)),
                  pl.BlockSpec((1, 1, ncp, HD), lambda b, g, i: (b, g, 0, 0)),
                  pl.BlockSpec((1, 1, HD, ncp), lambda b, g, i: (b, g, 0, 0)),
                  pl.BlockSpec((1, 1, seq, LANES), lambda b, g, i: (b, g, 0, 0)),
                  pl.BlockSpec((1, 1, seq, HD), lambda b, g, i: (b, 2 + g, 0, 0)),
                  pl.BlockSpec((1, nq, VROWS, LANES), lambda b, g, i: (b, 0, g, 0)),
                  pl.BlockSpec((1, nq, VROWS, LANES), lambda b, g, i: (b, 0, 2 + g, 0)),
                  pl.BlockSpec((1, tiles, 1, 3 * HPG, TQ), lambda b, g, i: (b, i, g, 0, 0)),
                  pl.BlockSpec(bmapT.shape, lambda b, g, i: (0, 0))],
        out_specs=pl.BlockSpec((1, tiles, HPG * HD, TQ), lambda b, g, i: (b, i, g, 0)),
        scratch_shapes=[pltpu.VMEM((nb, TQ), F32),
                        pltpu.VMEM((nb, TQ), F32),
                        pltpu.VMEM((1, wl), F32),
                        pltpu.VMEM((VROWS, wl), F32),
                        pltpu.VMEM((KS, wl), F32),
                        pltpu.VMEM((KS, wl), F32),
                        pltpu.VMEM((1, wl), F32),
                        pltpu.VMEM((1, wl), F32),
                        pltpu.VMEM((WINDOW + TQ, wl), F32),
                        pltpu.VMEM((1, wl), F32)],
        compiler_params=_cparams(("parallel", "parallel", "arbitrary")),
        name="nsa_attention",
    )(qT, kc, vcT, ksa, kn, vT, vT, gT, bmapT)


def _ssd_kernel(*refs, chunks):
    def chunk(u, carry):
        _ssd_chunk(u, *refs)
        return carry

    lax.fori_loop(0, chunks, chunk, 0)


def _ssd_chunk(u, z_ref, xs_ref, bc_ref, dt_ref, cwx_ref, cwb_ref, cbx_ref, cbb_ref, dtb_ref, alog_ref,
               dsk_ref, nw_ref, o_ref, xpx_scr, xpb_scr, st_scr):
    L = SSM_CHUNK
    rows = pl.ds(pl.multiple_of(u * L, L), L)
    pad = SSM_PAD
    npair = SSM_HEADS // 2
    ppg = npair // SSM_GROUPS

    @pl.when((pl.program_id(1) == 0) & (u == 0))
    def _():
        xpx_scr[0:pad, :] = jnp.zeros((pad, xpx_scr.shape[1]), BF16)
        xpb_scr[0:pad, :] = jnp.zeros((pad, xpb_scr.shape[1]), BF16)
        st_scr[...] = jnp.zeros(st_scr.shape, F32)

    nshift = SSM_CONV - 1
    srow = lax.broadcasted_iota(jnp.int32, (nshift * L, pad + L), 0)
    scol = lax.broadcasted_iota(jnp.int32, (nshift * L, pad + L), 1)
    shift = jnp.where(scol == (srow & (L - 1)) + jnp.right_shift(srow, int(np.log2(L))) + (pad - nshift),
                      1.0, 0.0).astype(BF16)

    def conv_silu(x_ref, w_ref, b_ref, scr):
        x = x_ref[rows, :]
        scr[pad:pad + L, :] = x
        sh = _dot(shift, scr[...])
        acc = b_ref[...] + w_ref[nshift:nshift + 1, :] * x.astype(F32)
        for k in range(nshift):
            acc = acc + w_ref[k:k + 1, :] * sh[k * L:(k + 1) * L]
        scr[0:pad, :] = scr[L:L + pad, :]
        return _silu(acc)

    xs = conv_silu(xs_ref, cwx_ref, cbx_ref, xpx_scr)
    bcm = conv_silu(bc_ref, cwb_ref, cbb_ref, xpb_scr).astype(BF16)
    xs_b = xs.astype(BF16)

    dtv = dt_ref[rows, :] + dtb_ref[...]
    dt = jnp.maximum(dtv, 0.0) + jnp.log(1.0 + jnp.exp(-jnp.abs(dtv)))
    a = -jnp.exp(alog_ref[...]) * LOG2E
    row = lax.broadcasted_iota(jnp.int32, (L, L), 0)
    col = lax.broadcasted_iota(jnp.int32, (L, L), 1)
    causal = col <= row
    a_cum = _dot_split3(jnp.where(causal, 1.0, 0.0).astype(BF16), dt * a)
    a_cumT = a_cum.T
    dtT = dt.T
    a_last = a_cum[L - 1:L, :]
    a_lastT = a_cumT[:, L - 1:L]
    wT = dtT * jnp.exp2(a_lastT - a_cumT)
    e_last = jnp.exp2(a_last)
    lo = lax.broadcasted_iota(jnp.int32, (1, LANES), 1) < SSM_HD

    ys = []
    for g in range(SSM_GROUPS):
        Bg = bcm[:, g * SSM_STATE:(g + 1) * SSM_STATE]
        Cg = bcm[:, (SSM_GROUPS + g) * SSM_STATE:(SSM_GROUPS + g + 1) * SSM_STATE]
        cb = _dot_nt(Cg, Bg)
        BgT = Bg.astype(F32).T
        for jp in range(ppg):
            pj = g * ppg + jp
            h0, h1 = 2 * pj, 2 * pj + 1
            xp = xs_b[:, pj * LANES:(pj + 1) * LANES]

            def head_mats(h):
                ecol = jnp.broadcast_to(a_cum[:, h:h + 1], (L, L))
                dec = jnp.exp2(jnp.where(causal, ecol - a_cumT[h:h + 1, :], NEG))
                gm = (cb * dec * dtT[h:h + 1, :]).astype(BF16)
                bw = (BgT * wT[h:h + 1, :]).astype(BF16)
                return gm, bw, jnp.exp2(ecol)

            g0, b0, e0 = head_mats(h0)
            g1, b1, e1 = head_mats(h1)
            y_d = jnp.where(lo, _dot(g0, xp), _dot(g1, xp))
            st_prev = st_scr[pj]
            y_o = _dot(Cg, st_prev.astype(BF16)) * jnp.where(lo, e0, e1)
            st_new = jnp.where(lo, _dot(b0, xp), _dot(b1, xp))
            cdec = jnp.where(lo, jnp.broadcast_to(e_last[:, h0:h0 + 1], (1, LANES)),
                             jnp.broadcast_to(e_last[:, h1:h1 + 1], (1, LANES)))
            st_scr[pj] = st_prev * cdec + st_new
            sl = slice(pj * LANES, (pj + 1) * LANES)
            ys.append(y_d + y_o + dsk_ref[:, sl] * xs[:, sl])
    y = jnp.concatenate(ys, axis=1)
    y = y * _silu(z_ref[rows, :].astype(F32))
    gw = SSM_INNER // SSM_GROUPS
    for g in range(SSM_GROUPS):
        yg = y[:, g * gw:(g + 1) * gw]
        ms = jnp.mean(yg * yg, axis=-1, keepdims=True)
        o_ref[rows, g * gw:(g + 1) * gw] = (yg * lax.rsqrt(ms + EPS) * nw_ref[:, g * gw:(g + 1) * gw]).astype(BF16)


def _ssd(zxm, dt, cwx, cwb, cbx, cbb, dtb, alog, dsk, nw, bsz, seq):
    L = SSM_CHUNK
    chunks = SSD_CHUNKS_PER_STEP
    rows = chunks * L
    nstep = seq // rows
    assert seq % rows == 0
    nbc = 2 * SSM_GROUPS * SSM_STATE
    full = lambda a: pl.BlockSpec(a.shape, lambda b, c: (0,) * a.ndim)
    rowblk = lambda width, idx: pl.BlockSpec((rows, width), lambda b, c: (b * nstep + c, idx))
    return pl.pallas_call(
        functools.partial(_ssd_kernel, chunks=chunks),
        out_shape=jax.ShapeDtypeStruct((bsz * seq, SSM_INNER), BF16),
        grid=(bsz, nstep),
        in_specs=[rowblk(SSM_INNER, 0), rowblk(SSM_INNER, 2), rowblk(nbc, 6), rowblk(LANES, 0),
                  full(cwx), full(cwb), full(cbx), full(cbb), full(dtb), full(alog), full(dsk), full(nw)],
        out_specs=pl.BlockSpec((rows, SSM_INNER), lambda b, c: (b * nstep + c, 0)),
        scratch_shapes=[pltpu.VMEM((L + SSM_PAD, SSM_INNER), BF16),
                        pltpu.VMEM((L + SSM_PAD, nbc), BF16),
                        pltpu.VMEM((SSM_HEADS // 2, SSM_STATE, LANES), F32)],
        compiler_params=_cparams(("parallel", "arbitrary")),
        name="ssd",
    )(zxm, zxm, zxm, dt, cwx, cwb, cbx, cbb, dtb, alog, dsk, nw)


def _outproj_kernel(oT_ref, os_ref, ga_ref, gs_ref, x_ref, mod_ref, wa_ref, ws_ref, wo_ref, o_ref):
    oT = jnp.concatenate([oT_ref[0, j] for j in range(oT_ref.shape[1])], axis=1)
    ya = _dot_tn(oT, wa_ref[...])
    ys = _dot(os_ref[...], ws_ref[...])
    m = ga_ref[...].astype(F32) * ya + gs_ref[...].astype(F32) * ys
    r = _dot(m.astype(BF16), wo_ref[...])
    o_ref[...] = x_ref[...] + mod_ref[0][2:3] * r


def _outproj(oT, ossm, zxm, x2d, mod, wa, ws, wo, bsz, seq):
    d = x2d.shape[1]
    tm = 512
    tpb = seq // tm
    full = lambda a: pl.BlockSpec(a.shape, lambda i: (0,) * a.ndim)
    return pl.pallas_call(
        _outproj_kernel,
        out_shape=jax.ShapeDtypeStruct((bsz * seq, d), F32),
        grid=(bsz * tpb,),
        in_specs=[pl.BlockSpec((1, tm // TQ, N_HEADS * HD, TQ), lambda i: (i // tpb, i % tpb, 0, 0)),
                  pl.BlockSpec((tm, SSM_INNER), lambda i: (i, 0)),
                  pl.BlockSpec((tm, d), lambda i: (i, 2)),
                  pl.BlockSpec((tm, d), lambda i: (i, 3)),
                  pl.BlockSpec((tm, d), lambda i: (i, 0)),
                  pl.BlockSpec((1, 6, d), lambda i: (i // tpb, 0, 0)),
                  full(wa), full(ws), full(wo)],
        out_specs=pl.BlockSpec((tm, d), lambda i: (i, 0)),
        compiler_params=_cparams(("parallel",)),
        name="outproj",
    )(oT, ossm, zxm, zxm, x2d, mod, wa, ws, wo)


def _ffn_kernel(x_ref, mod_ref, n2_ref, nf_ref, wg_ref, wu_ref, wd_ref, o_ref, *, final_norm):
    x = x_ref[...]
    mod = mod_ref[0]
    h = _norm_mod(x, n2_ref[...], mod[4:5], mod[3:4]).astype(BF16)
    a = _silu(_dot(h, wg_ref[...])) * _dot(h, wu_ref[...])
    r = _dot(a.astype(BF16), wd_ref[...])
    x2 = x + mod[5:6] * r
    if final_norm:
        ms = jnp.mean(x2 * x2, axis=-1, keepdims=True)
        x2 = x2 * lax.rsqrt(ms + EPS) * nf_ref[...]
    o_ref[...] = x2


def _ffn(x1, mod, n2, nf, wg, wu, wd, final_norm, bsz, seq):
    d = x1.shape[1]
    tm = 512
    tpb = seq // tm
    const = lambda a: pl.BlockSpec(a.shape, lambda i: (0,) * a.ndim, pipeline_mode=pl.Buffered(1))
    return pl.pallas_call(
        functools.partial(_ffn_kernel, final_norm=final_norm),
        out_shape=jax.ShapeDtypeStruct((bsz * seq, d), F32),
        grid=(bsz * tpb,),
        in_specs=[pl.BlockSpec((tm, d), lambda i: (i, 0)),
                  pl.BlockSpec((1, 6, d), lambda i: (i // tpb, 0, 0)),
                  const(n2), const(nf), const(wg), const(wu), const(wd)],
        out_specs=pl.BlockSpec((tm, d), lambda i: (i, 0)),
        compiler_params=_cparams(("parallel",)),
        name="ffn",
    )(x1, mod, n2, nf, wg, wu, wd)


def _rotary_tables(seq):
    inv_freq = jnp.asarray(np.power(ROPE_THETA, -np.arange(0, 2 * ROT_HALF, 2) / (2 * ROT_HALF)).astype(np.float32))
    ang = jnp.arange(seq).astype(F32)[:, None] * inv_freq[None, :]
    cos, sin = jnp.cos(ang), jnp.sin(ang)
    ones = jnp.ones((seq, HD - 2 * ROT_HALF), F32)
    zeros8 = jnp.zeros((seq, ROT_HALF), F32)
    zeros = jnp.zeros((seq, HD - 2 * ROT_HALF), F32)
    cosK = jnp.concatenate([cos, cos, ones], axis=1)
    sinA = jnp.concatenate([-sin, zeros8, zeros], axis=1)
    sinB = jnp.concatenate([zeros8, sin, zeros], axis=1)
    rep = LANES // HD
    tile = lambda t: jnp.concatenate([t] * rep, axis=1)
    return cos.T, sin.T, tile(cosK), tile(sinA), tile(sinB)


def _selection_block_map_T(seq):
    nc = (seq - CMP_BLOCK) // CMP_STRIDE + 1
    nb = seq // SLC_BLOCK
    cs = np.arange(nc) * CMP_STRIDE
    bs = np.arange(nb) * SLC_BLOCK
    ov = np.minimum(cs[:, None] + CMP_BLOCK, bs[None, :] + SLC_BLOCK) - np.maximum(cs[:, None], bs[None, :])
    bmap = (np.clip(ov, 0, None) / CMP_BLOCK).astype(np.float32)
    out = np.zeros((nb, seq // CMP_STRIDE), np.float32)
    out[:, :nc] = bmap.T
    return jnp.asarray(out)


def kernel(x, c, w_ada, b_ada, norm1_w, w_in, cmp_pe_k, cmp_w1_k, cmp_w2_k, cmp_pe_v, cmp_w1_v, cmp_w2_v,
           conv_w, conv_b, dt_bias, a_log, d_skip, ssm_norm_w, w_attn_out, w_ssm_out, w_o, norm2_w,
           w_gate, w_up, w_down, norm_f_w):
    bsz, seq, d = x.shape
    depth = w_ada.shape[0]
    q_w = N_HEADS * HD
    kv_w = N_GROUPS * HD
    xbc_w = SSM_INNER + 2 * SSM_GROUPS * SSM_STATE
    offs = np.cumsum([q_w, 6 * kv_w, 3 * N_HEADS, SSM_INNER, xbc_w, SSM_HEADS])
    tabs = _rotary_tables(seq)
    bmapT = _selection_block_map_T(seq)
    x2d = x.reshape(bsz * seq, d)
    for l in range(depth):
        mod = _ada_mod(c, w_ada[l], b_ada[l]).reshape(bsz, 6, d)
        wi = w_in[l]
        wq = wi[:, :offs[0]]
        wkv = wi[:, offs[0]:offs[1]].reshape(d, 6, N_GROUPS * HD)
        wg = wi[:, offs[1]:offs[2]].reshape(d, N_GROUPS, HPG, 3)
        wz = wi[:, offs[2]:offs[3]]
        wxbc = wi[:, offs[3]:offs[4]]
        wdt = wi[:, offs[4]:offs[5]]
        wmg = wi[:, offs[5]:]
        wqT = wq.T.astype(BF16)
        zcol = jnp.zeros((d, HD), wi.dtype)
        wkn = jnp.concatenate([wkv[:, 2, :HD], zcol, wkv[:, 2, HD:], zcol, wkv[:, 0], wkv[:, 4], wkv[:, 1]],
                              axis=1).astype(BF16)
        wvT = jnp.concatenate([wkv[:, 3], wkv[:, 5]], axis=1).T.astype(BF16)
        wgT = jnp.transpose(wg, (1, 3, 2, 0)).reshape(3 * N_HEADS, d).astype(BF16)
        wdtp = jnp.pad(wdt, ((0, 0), (0, LANES - SSM_HEADS))).astype(BF16)
        wbig = jnp.concatenate([wz, wmg, wxbc], axis=1).astype(BF16)
        nw1 = norm1_w[l].reshape(1, d)

        qT, ksa, kn, vT, gT, dt = _inproj_attn(x2d, mod, nw1, wqT, wkn, wvT, wgT, wdtp, tabs, bsz, seq)
        zxm = _inproj_big(x2d, mod, nw1, wbig, bsz, seq)

        kn16 = kn.reshape(bsz, 6, seq // CMP_STRIDE, CMP_STRIDE * HD)
        kc = _compress(kn16, 0, cmp_pe_k[l], cmp_w1_k[l], cmp_w2_k[l], False, bsz, seq)
        vcT = _compress(kn16, 4, cmp_pe_v[l], cmp_w1_v[l], cmp_w2_v[l], True, bsz, seq)
        oT = _nsa(qT, kc, vcT, ksa, kn, vT, gT, bmapT.astype(BF16), bsz, seq)

        cw = conv_w[l]
        cb = conv_b[l].reshape(1, xbc_w)
        pad_h = lambda v: jnp.pad(v.reshape(1, SSM_HEADS), ((0, 0), (0, LANES - SSM_HEADS)))
        dsk = jnp.repeat(d_skip[l], SSM_HD).reshape(1, SSM_INNER)
        ossm = _ssd(zxm, dt, cw[:, :SSM_INNER], cw[:, SSM_INNER:], cb[:, :SSM_INNER], cb[:, SSM_INNER:],
                    pad_h(dt_bias[l]), pad_h(a_log[l]), dsk, ssm_norm_w[l].reshape(1, SSM_INNER), bsz, seq)

        x1 = _outproj(oT, ossm, zxm, x2d, mod, w_attn_out[l].astype(BF16), w_ssm_out[l].astype(BF16),
                      w_o[l].astype(BF16), bsz, seq)
        x2d = _ffn(x1, mod, norm2_w[l].reshape(1, d), norm_f_w.reshape(1, d), w_gate[l].astype(BF16),
                   w_up[l].astype(BF16), w_down[l].astype(BF16), l == depth - 1, bsz, seq)
    return x2d.reshape(bsz, seq, d)
```

```python
import functools

import numpy as np
import jax
import jax.numpy as jnp
from jax import lax
from jax.experimental import pallas as pl
from jax.experimental.pallas import tpu as pltpu

F32 = jnp.float32
BF16 = jnp.bfloat16
HIGHEST = lax.Precision.HIGHEST

EPS = 1e-6
NEG = -1e30

N_HEADS = 16
N_GROUPS = 2
HPG = N_HEADS // N_GROUPS
HD = 64
ROT_HALF = 8
ROPE_THETA = 500000.0
CMP_BLOCK = 32
CMP_STRIDE = 16
CMP_HIDDEN = 256
SLC_BLOCK = 64
SLC_TOPK = 16
WINDOW = 512

SSM_HEADS = 32
SSM_HD = 64
SSM_GROUPS = 4
SSM_STATE = 128
SSM_CONV = 4
SSM_CHUNK = 128
SSM_INNER = SSM_HEADS * SSM_HD
SSM_PAD = 16

TQ = 128
NSA_TILES_PER_STEP = 2
SSD_CHUNKS_PER_STEP = 2
KS = 512
LANES = 128
VROWS = HD + 16
LOG2E = float(np.log2(np.e))
VMEM_LIMIT = 56 * 1024 * 1024


def _cparams(sem, vmem=VMEM_LIMIT):
    return pltpu.CompilerParams(dimension_semantics=sem, vmem_limit_bytes=vmem)


def _sigmoid(x):
    return 0.5 + 0.5 * jnp.tanh(0.5 * x)


def _silu(x):
    hx = 0.5 * x
    return hx + hx * jnp.tanh(hx)


def _norm_mod(x, w, sc, sh):
    ms = jnp.mean(x * x, axis=-1, keepdims=True)
    return (x * lax.rsqrt(ms + EPS) * w) * (1.0 + sc) + sh


def _dot(a, b):
    return jnp.dot(a, b, preferred_element_type=F32)


def _dot_split3(a, x):
    hi = x.astype(BF16)
    r1 = x - hi.astype(F32)
    mid = r1.astype(BF16)
    lo = (r1 - mid.astype(F32)).astype(BF16)
    return _dot(a, hi) + _dot(a, mid) + _dot(a, lo)


def _dot_nt(a, b):
    return lax.dot_general(a, b, (((1,), (1,)), ((), ())), preferred_element_type=F32)


def _dot_tn(a, b):
    return lax.dot_general(a, b, (((0,), (0,)), ((), ())), preferred_element_type=F32)


def _ada_kernel(c_ref, w_ref, b_ref, o_ref):
    c = c_ref[...]
    o_ref[...] = jnp.dot(_silu(c), w_ref[...], precision=HIGHEST, preferred_element_type=F32) + b_ref[...]


def _ada_mod(c, w, b):
    bsz, d = c.shape
    n = w.shape[1]
    tn = 1024
    return pl.pallas_call(
        _ada_kernel,
        out_shape=jax.ShapeDtypeStruct((bsz, n), F32),
        grid=(n // tn,),
        in_specs=[pl.BlockSpec((bsz, d), lambda j: (0, 0)),
                  pl.BlockSpec((d, tn), lambda j: (0, j)),
                  pl.BlockSpec((1, tn), lambda j: (0, j))],
        out_specs=pl.BlockSpec((bsz, tn), lambda j: (0, j)),
        compiler_params=_cparams(("parallel",)),
        name="ada_mod",
    )(c, w, b.reshape(1, n))


def _inproj_attn_kernel(x_ref, mod_ref, nw_ref, wqT_ref, wkn_ref, wvT_ref, wgT_ref, wdt_ref,
                        cosT_ref, sinT_ref, cosK_ref, sinA_ref, sinB_ref, perm_ref,
                        qT_ref, ksa_ref, kn_ref, kcv_ref, vT_ref, gT_ref, dt_ref):
    tm = x_ref.shape[0]
    mod = mod_ref[0]
    h = _norm_mod(x_ref[...], nw_ref[...], mod[1:2], mod[0:1]).astype(BF16)

    qT = _dot_nt(wqT_ref[...], h)
    cT = cosT_ref[...]
    sT = sinT_ref[...]
    scale = HD ** -0.5 * LOG2E
    for hh in range(N_HEADS):
        r0 = hh * HD
        x1 = qT[r0:r0 + ROT_HALF]
        x2 = qT[r0 + ROT_HALF:r0 + 2 * ROT_HALF]
        blk = jnp.concatenate([x1 * cT - x2 * sT, x2 * cT + x1 * sT, qT[r0 + 2 * ROT_HALF:r0 + HD]], axis=0)
        blk = (blk * scale).astype(BF16)
        for j in range(tm // TQ):
            qT_ref[0, j, r0:r0 + HD, :] = blk[:, j * TQ:(j + 1) * TQ]

    kn = _dot(h, wkn_ref[...])
    nrot = 8 * HD
    kk = kn[:, :nrot]
    reps = nrot // LANES
    cK = jnp.concatenate([cosK_ref[...]] * reps, axis=1)
    sA = jnp.concatenate([sinA_ref[...]] * reps, axis=1)
    sB = jnp.concatenate([sinB_ref[...]] * reps, axis=1)
    kk = kk * cK + pltpu.roll(kk, nrot - ROT_HALF, 1) * sA + pltpu.roll(kk, ROT_HALF, 1) * sB
    row = lax.broadcasted_iota(jnp.int32, (tm, LANES), 0)
    lane = lax.broadcasted_iota(jnp.int32, (tm, LANES), 1)
    bps = KS // SLC_BLOCK
    blk = jnp.right_shift(row, int(np.log2(SLC_BLOCK)))
    onehot = jnp.where(lane == HD + (blk & (bps - 1)), 1.0, 0.0)
    for g in range(N_GROUPS):
        ksa_ref[0, g] = (kk[:, g * LANES:(g + 1) * LANES] + onehot).astype(BF16)
    for i in range(N_GROUPS):
        kn_ref[0, i] = kk[:, (6 + i) * HD:(7 + i) * HD].astype(BF16)

    nr = tm // CMP_STRIDE
    lo = lax.broadcasted_iota(jnp.int32, (nr, LANES), 1) < HD
    for i, pk in enumerate((kk[:, 4 * HD:6 * HD], kn[:, nrot:nrot + 2 * HD])):
        pm = _dot(perm_ref[...], pk.astype(BF16))
        for j in range(CMP_STRIDE // 2):
            a = pm[2 * j * nr:(2 * j + 1) * nr]
            b = pm[(2 * j + 1) * nr:(2 * j + 2) * nr]
            kcv_ref[0, 2 * i, :, j * LANES:(j + 1) * LANES] = jnp.where(lo, a, pltpu.roll(b, HD, 1)).astype(BF16)
            kcv_ref[0, 2 * i + 1, :, j * LANES:(j + 1) * LANES] = jnp.where(lo, pltpu.roll(a, HD, 1), b).astype(BF16)

    vT = _dot_nt(wvT_ref[...], h).astype(BF16)
    ones = jnp.ones((VROWS - HD, LANES), BF16)
    for j in range(tm // LANES):
        for r in range(4):
            vT_ref[0, j, r * VROWS:r * VROWS + HD, :] = vT[r * HD:(r + 1) * HD, j * LANES:(j + 1) * LANES]
            vT_ref[0, j, r * VROWS + HD:(r + 1) * VROWS, :] = ones

    gates = _sigmoid(_dot_nt(wgT_ref[...], h))
    for j in range(tm // TQ):
        for g in range(N_GROUPS):
            gT_ref[0, j, g] = gates[g * 3 * HPG:(g + 1) * 3 * HPG, j * TQ:(j + 1) * TQ]
    dt_ref[...] = _dot(h, wdt_ref[...])


def _inproj_attn(x2d, mod, nw, wqT, wkn, wvT, wgT, wdt, tabs, bsz, seq):
    d = x2d.shape[1]
    tm = KS
    tpb = seq // tm
    cosT, sinT, cosK, sinA, sinB = tabs
    full = lambda a: pl.BlockSpec(a.shape, lambda i: (0,) * a.ndim)
    orow = np.arange(tm)
    src = CMP_STRIDE * (orow % (tm // CMP_STRIDE)) + orow // (tm // CMP_STRIDE)
    perm = jnp.asarray(np.arange(tm)[None, :] == src[:, None], BF16)
    out_shape = (
        jax.ShapeDtypeStruct((bsz, seq // TQ, N_HEADS * HD, TQ), BF16),
        jax.ShapeDtypeStruct((bsz, N_GROUPS, seq, LANES), BF16),
        jax.ShapeDtypeStruct((bsz, N_GROUPS, seq, HD), BF16),
        jax.ShapeDtypeStruct((bsz, 2 * N_GROUPS, seq // CMP_STRIDE, CMP_STRIDE * HD), BF16),
        jax.ShapeDtypeStruct((bsz, seq // LANES, 4 * VROWS, LANES), BF16),
        jax.ShapeDtypeStruct((bsz, seq // TQ, N_GROUPS, 3 * HPG, TQ), F32),
        jax.ShapeDtypeStruct((bsz * seq, LANES), F32),
    )
    return pl.pallas_call(
        _inproj_attn_kernel,
        out_shape=out_shape,
        grid=(bsz * tpb,),
        in_specs=[pl.BlockSpec((tm, d), lambda i: (i, 0)),
                  pl.BlockSpec((1, 6, d), lambda i: (i // tpb, 0, 0)),
                  full(nw), full(wqT), full(wkn), full(wvT), full(wgT), full(wdt),
                  pl.BlockSpec((ROT_HALF, tm), lambda i: (0, i % tpb)),
                  pl.BlockSpec((ROT_HALF, tm), lambda i: (0, i % tpb)),
                  pl.BlockSpec((tm, LANES), lambda i: (i % tpb, 0)),
                  pl.BlockSpec((tm, LANES), lambda i: (i % tpb, 0)),
                  pl.BlockSpec((tm, LANES), lambda i: (i % tpb, 0)),
                  full(perm)],
        out_specs=(pl.BlockSpec((1, tm // TQ, N_HEADS * HD, TQ), lambda i: (i // tpb, i % tpb, 0, 0)),
                   pl.BlockSpec((1, N_GROUPS, tm, LANES), lambda i: (i // tpb, 0, i % tpb, 0)),
                   pl.BlockSpec((1, N_GROUPS, tm, HD), lambda i: (i // tpb, 0, i % tpb, 0)),
                   pl.BlockSpec((1, 2 * N_GROUPS, tm // CMP_STRIDE, CMP_STRIDE * HD),
                                lambda i: (i // tpb, 0, i % tpb, 0)),
                   pl.BlockSpec((1, tm // LANES, 4 * VROWS, LANES), lambda i: (i // tpb, i % tpb, 0, 0)),
                   pl.BlockSpec((1, tm // TQ, N_GROUPS, 3 * HPG, TQ), lambda i: (i // tpb, i % tpb, 0, 0, 0)),
                   pl.BlockSpec((tm, LANES), lambda i: (i, 0))),
        compiler_params=_cparams(("parallel",)),
        name="inproj_attn",
    )(x2d, mod, nw, wqT, wkn, wvT, wgT, wdt, cosT, sinT, cosK, sinA, sinB, perm)


def _inproj_big_kernel(x_ref, mod_ref, nw_ref, w_ref, o_ref, h_scr, *, gate_lo, gate_hi):
    j = pl.program_id(1)

    @pl.when(j == 0)
    def _():
        mod = mod_ref[0]
        h_scr[...] = _norm_mod(x_ref[...], nw_ref[...], mod[1:2], mod[0:1]).astype(BF16)

    acc = _dot(h_scr[...], w_ref[...])
    tn = acc.shape[1]
    col = j * tn + lax.broadcasted_iota(jnp.int32, (1, tn), 1)
    is_gate = (col >= gate_lo) & (col < gate_hi)
    o_ref[...] = jnp.where(is_gate, _sigmoid(acc), acc).astype(BF16)


def _inproj_big(x2d, mod, nw, wbig, bsz, seq):
    d = x2d.shape[1]
    n = wbig.shape[1]
    tm, tn = 1024, n // 4
    assert n % 4 == 0 and tn % LANES == 0
    tpb = seq // tm
    return pl.pallas_call(
        functools.partial(_inproj_big_kernel, gate_lo=SSM_INNER, gate_hi=SSM_INNER + 2 * d),
        out_shape=jax.ShapeDtypeStruct((bsz * seq, n), BF16),
        grid=(bsz * tpb, n // tn),
        in_specs=[pl.BlockSpec((tm, d), lambda i, j: (i, 0)),
                  pl.BlockSpec((1, 6, d), lambda i, j: (i // tpb, 0, 0)),
                  pl.BlockSpec((1, d), lambda i, j: (0, 0)),
                  pl.BlockSpec((d, tn), lambda i, j: (0, j))],
        out_specs=pl.BlockSpec((tm, tn), lambda i, j: (i, j)),
        scratch_shapes=[pltpu.VMEM((tm, d), BF16)],
        compiler_params=_cparams(("parallel", "arbitrary")),
        name="inproj_big",
    )(x2d, mod, nw, wbig)


def _compress_kernel(x_ref, pea_ref, peb_ref, w1a_ref, w1b_ref, w2_ref, o_ref, *, transpose_out, nc):
    x = x_ref[0, 0].astype(F32)
    nrow = x.shape[0]
    ya = _dot((x + pea_ref[...]).astype(BF16), w1a_ref[...])
    yb = _dot((x + peb_ref[...]).astype(BF16), w1b_ref[...])
    hp = ya + pltpu.roll(yb, nrow - 1, 0)
    row = lax.broadcasted_iota(jnp.int32, hp.shape, 0)
    hh = jnp.where(row < nc, _silu(hp), 0.0).astype(BF16)
    if transpose_out:
        o_ref[0, 0] = _dot_nt(w2_ref[...], hh).astype(BF16)
    else:
        o_ref[0, 0] = _dot(hh, w2_ref[...]).astype(BF16)


def _compress(kn16, slot0, pe, w1, w2, transpose_out, bsz, seq):
    nrow = seq // CMP_STRIDE
    nc = (seq - CMP_BLOCK) // CMP_STRIDE + 1
    half = CMP_STRIDE * HD
    pea = pe[:CMP_STRIDE].reshape(1, half)
    peb = pe[CMP_STRIDE:].reshape(1, half)
    w1a = w1[:half].astype(BF16)
    w1b = w1[half:].astype(BF16)
    if transpose_out:
        w2p = w2.T.astype(BF16)
        oshape, oblock = (bsz, N_GROUPS, HD, nrow), (1, 1, HD, nrow)
    else:
        w2p = w2.astype(BF16)
        oshape, oblock = (bsz, N_GROUPS, nrow, HD), (1, 1, nrow, HD)
    full = lambda a: pl.BlockSpec(a.shape, lambda b, g: (0,) * a.ndim)
    return pl.pallas_call(
        functools.partial(_compress_kernel, transpose_out=transpose_out, nc=nc),
        out_shape=jax.ShapeDtypeStruct(oshape, BF16),
        grid=(bsz, N_GROUPS),
        in_specs=[pl.BlockSpec((1, 1, nrow, half), lambda b, g: (b, slot0 + g, 0, 0)),
                  full(pea), full(peb), full(w1a), full(w1b), full(w2p)],
        out_specs=pl.BlockSpec(oblock, lambda b, g: (b, g, 0, 0)),
        compiler_params=_cparams(("parallel", "parallel")),
        name="compress_v" if transpose_out else "compress_k",
    )(kn16, pea, peb, w1a, w1b, w2p)


def _nsa_kernel(*refs, nb, tiles):
    def tile(u, carry):
        _nsa_tile(u, pl.program_id(2) * tiles + u, *refs, nb=nb)
        return carry

    lax.fori_loop(0, tiles, tile, 0)


def _nsa_tile(u, qi, q_ref, kc_ref, vcT_ref, ksa_ref, kw_ref, vsT_ref, vwT_ref, g_ref, bmapT_ref, o_ref,
              v_scr, selb_scr, m_scr, acc_scr, sa_scr, sb_scr, ma_scr, mb_scr, sw_scr, mw_scr, *, nb):
    t0 = qi * TQ
    wl = HPG * TQ
    q = q_ref[0, u]
    Q = jnp.concatenate([q[hh * HD:(hh + 1) * HD, :] for hh in range(HPG)], axis=1)
    tok1 = t0 + lax.broadcasted_iota(jnp.int32, (1, TQ), 1)
    tok = jnp.concatenate([tok1] * HPG, axis=1)

    def lanes8(x):
        return jnp.concatenate([x] * HPG, axis=1)

    nwt = WINDOW // TQ + 1
    kt0 = jnp.maximum(qi - WINDOW // TQ, 0)
    kw0 = pl.multiple_of(kt0 * TQ, TQ)
    sw = _dot(kw_ref[0, 0, pl.ds(kw0, nwt * TQ), :], Q)
    diff = tok1 - (kw0 + lax.broadcasted_iota(jnp.int32, (nwt * TQ, 1), 0))
    sw = sw + lanes8(jnp.where((diff >= 0) & (diff < WINDOW), 0.0, NEG))
    sw_scr[...] = sw
    mw_scr[...] = jnp.max(sw, axis=0, keepdims=True)

    ncp = kc_ref.shape[2]
    sc = _dot(kc_ref[0, 0], Q)
    n_idx = lax.broadcasted_iota(jnp.int32, (ncp, 1), 0)
    mask_c = (n_idx * CMP_STRIDE + (CMP_BLOCK - 1)) <= tok
    sc = jnp.where(mask_c, sc, NEG)
    mc = jnp.max(sc, axis=0, keepdims=True)
    ec = jnp.exp2(sc - mc)
    has_blk = tok >= CMP_BLOCK - 1
    pc = ec * jnp.where(has_blk, 1.0 / jnp.sum(ec, axis=0, keepdims=True), 0.0)
    o_c = _dot(vcT_ref[0, 0], pc.astype(BF16))

    psum = pc[:, 0:TQ]
    for hh in range(1, HPG):
        psum = psum + pc[:, hh * TQ:(hh + 1) * TQ]
    imp = _dot_split3(bmapT_ref[...], psum)
    jb = lax.broadcasted_iota(jnp.int32, (nb, 1), 0)
    cur = jnp.right_shift(tok1, int(np.log2(SLC_BLOCK)))
    forced = (jb == 0) | (jb == cur) | (jb == cur - 1)
    v = jnp.where(forced, jnp.inf, jnp.where(jb <= cur, imp, -jnp.inf))
    v_scr[...] = v
    jbf = lax.broadcasted_iota(jnp.int32, (nb, TQ), 0)

    def rank_body(i2, rank):
        for u in range(2):
            i = 2 * i2 + u
            vi = v_scr[pl.ds(i, 1), :]
            ge = jnp.where(vi >= v, 1.0, 0.0)
            gt = jnp.where(vi > v, 1.0, 0.0)
            rank = rank + jnp.where(jbf > i, ge, gt)
        return rank

    n_live = (TQ // SLC_BLOCK) * (qi + 1)
    rank_trips = jnp.where(n_live <= SLC_TOPK, 0, jnp.minimum(n_live, nb) // 2)
    rank = lax.fori_loop(0, rank_trips, rank_body, jnp.zeros((nb, TQ), F32))
    selb_scr[...] = jnp.where(rank < min(SLC_TOPK, nb), 0.0, NEG)

    tps = KS // TQ
    bps = KS // SLC_BLOCK
    m_scr[...] = jnp.full((1, wl), NEG, F32)
    acc_scr[...] = jnp.zeros((VROWS, wl), F32)
    st_hi = qi // tps

    def scores(st, s_scr, mx_scr, bias=None):
        sb = selb_scr[pl.ds(pl.multiple_of(st * bps, bps), bps), :]
        ext = lanes8(jnp.concatenate([sb, jnp.zeros((16 - bps, TQ), F32)], axis=0)).astype(BF16)
        qa = jnp.concatenate([Q, ext, jnp.zeros((LANES - HD - 16, wl), BF16)], axis=0)
        s = _dot(ksa_ref[0, 0, pl.ds(pl.multiple_of(st * KS, KS), KS), :], qa)
        if bias is not None:
            s = s + bias
        s_scr[...] = s
        mx_scr[...] = jnp.max(s, axis=0, keepdims=True)

    def update(s_scr, mx_scr, st):
        m_old = m_scr[...]
        m_new = jnp.maximum(m_old, mx_scr[...])
        p = jnp.exp2(s_scr[...] - m_new).astype(BF16)
        vs = jnp.concatenate([vsT_ref[0, st * tps + j] for j in range(tps)], axis=1)
        acc_scr[...] = acc_scr[...] * jnp.exp2(m_old - m_new) + _dot(vs, p)
        m_scr[...] = m_new

    kpos = st_hi * KS + lax.broadcasted_iota(jnp.int32, (KS, 1), 0)
    scores(st_hi, sa_scr, ma_scr, lanes8(jnp.where(kpos <= tok1, 0.0, NEG)))

    pw = jnp.exp2(sw_scr[...] - mw_scr[...]).astype(BF16)
    vwin = jnp.concatenate([vwT_ref[0, kt0 + j] for j in range(nwt)], axis=1)
    ow = _dot(vwin, pw)
    o_w = ow[:HD] / ow[HD:HD + 1]

    def pair_body(j, carry):
        ta = st_hi - 2 * j
        scores(ta - 1, sb_scr, mb_scr)
        update(sa_scr, ma_scr, ta)
        scores(ta - 2, sa_scr, ma_scr)
        update(sb_scr, mb_scr, ta - 1)
        return carry

    n_pairs = st_hi // 2
    lax.fori_loop(0, n_pairs, pair_body, 0)

    @pl.when(st_hi - 2 * n_pairs == 1)
    def _():
        scores(0, sb_scr, mb_scr)
        update(sa_scr, ma_scr, 1)
        update(sb_scr, mb_scr, 0)

    @pl.when(st_hi - 2 * n_pairs == 0)
    def _():
        update(sa_scr, ma_scr, 0)

    o_s = acc_scr[0:HD, :] / acc_scr[HD:HD + 1, :]

    gts = g_ref[0, u, 0]
    for hh in range(HPG):
        sl = slice(hh * TQ, (hh + 1) * TQ)
        o = (gts[hh:hh + 1] * o_c[:, sl] + gts[HPG + hh:HPG + hh + 1] * o_s[:, sl]
             + gts[2 * HPG + hh:2 * HPG + hh + 1] * o_w[:, sl])
        o_ref[0, u, hh * HD:(hh + 1) * HD, :] = o.astype(BF16)


def _nsa(qT, kc, vcT, ksa, kn, vT, gT, bmapT, bsz, seq):
    nq = seq // TQ
    nb = seq // SLC_BLOCK
    ncp = kc.shape[2]
    wl = HPG * TQ
    tiles = NSA_TILES_PER_STEP
    assert seq % KS == 0 and seq >= WINDOW + TQ and KS // SLC_BLOCK == 8 and nq % tiles == 0
    return pl.pallas_call(
        functools.partial(_nsa_kernel, nb=nb, tiles=tiles),
        out_shape=jax.ShapeDtypeStruct((bsz, nq, N_HEADS * HD, TQ), BF16),
        grid=(bsz, N_GROUPS, nq // tiles),
        in_specs=[pl.BlockSpec((1, tiles, HPG * HD, TQ), lambda b, g, i: (b, i, g, 0)),
                  pl.BlockSpec((1, 1, ncp, HD), lambda b, g, i: (b, g, 0, 0)),
                  pl.BlockSpec((1, 1, HD, ncp), lambda b, g, i: (b, g, 0, 0)),
                  pl.BlockSpec((1, 1, seq, LANES), lambda b, g, i: (b, g, 0, 0)),
                  pl.BlockSpec((1, 1, seq, HD), lambda b, g, i: (b, g, 0, 0)),
                  pl.BlockSpec((1, nq, VROWS, LANES), lambda b, g, i: (b, 0, g, 0)),
                  pl.BlockSpec((1, nq, VROWS, LANES), lambda b, g, i: (b, 0, 2 + g, 0)),
                  pl.BlockSpec((1, tiles, 1, 3 * HPG, TQ), lambda b, g, i: (b, i, g, 0, 0)),
                  pl.BlockSpec(bmapT.shape, lambda b, g, i: (0, 0))],
        out_specs=pl.BlockSpec((1, tiles, HPG * HD, TQ), lambda b, g, i: (b, i, g, 0)),
        scratch_shapes=[pltpu.VMEM((nb, TQ), F32),
                        pltpu.VMEM((nb, TQ), F32),
                        pltpu.VMEM((1, wl), F32),
                        pltpu.VMEM((VROWS, wl), F32),
                        pltpu.VMEM((KS, wl), F32),
                        pltpu.VMEM((KS, wl), F32),
                        pltpu.VMEM((1, wl), F32),
                        pltpu.VMEM((1, wl), F32),
                        pltpu.VMEM((WINDOW + TQ, wl), F32),
                        pltpu.VMEM((1, wl), F32)],
        compiler_params=_cparams(("parallel", "parallel", "arbitrary")),
        name="nsa_attention",
    )(qT, kc, vcT, ksa, kn, vT, vT, gT, bmapT)


def _ssd_kernel(*refs, chunks):
    def chunk(u, carry):
        _ssd_chunk(u, *refs)
        return carry

    lax.fori_loop(0, chunks, chunk, 0)


def _ssd_chunk(u, z_ref, xs_ref, bc_ref, dt_ref, cwx_ref, cwb_ref, cbx_ref, cbb_ref, dtb_ref, alog_ref,
               dsk_ref, nw_ref, o_ref, xpx_scr, xpb_scr, st_scr):
    L = SSM_CHUNK
    rows = pl.ds(pl.multiple_of(u * L, L), L)
    pad = SSM_PAD
    npair = SSM_HEADS // 2
    ppg = npair // SSM_GROUPS

    @pl.when((pl.program_id(1) == 0) & (u == 0))
    def _():
        xpx_scr[0:pad, :] = jnp.zeros((pad, xpx_scr.shape[1]), BF16)
        xpb_scr[0:pad, :] = jnp.zeros((pad, xpb_scr.shape[1]), BF16)
        st_scr[...] = jnp.zeros(st_scr.shape, F32)

    nshift = SSM_CONV - 1
    srow = lax.broadcasted_iota(jnp.int32, (nshift * L, pad + L), 0)
    scol = lax.broadcasted_iota(jnp.int32, (nshift * L, pad + L), 1)
    shift = jnp.where(scol == (srow & (L - 1)) + jnp.right_shift(srow, int(np.log2(L))) + (pad - nshift),
                      1.0, 0.0).astype(BF16)

    def conv_silu(x_ref, w_ref, b_ref, scr):
        x = x_ref[rows, :]
        scr[pad:pad + L, :] = x
        sh = _dot(shift, scr[...])
        acc = b_ref[...] + w_ref[nshift:nshift + 1, :] * x.astype(F32)
        for k in range(nshift):
            acc = acc + w_ref[k:k + 1, :] * sh[k * L:(k + 1) * L]
        scr[0:pad, :] = scr[L:L + pad, :]
        return _silu(acc)

    xs = conv_silu(xs_ref, cwx_ref, cbx_ref, xpx_scr)
    bcm = conv_silu(bc_ref, cwb_ref, cbb_ref, xpb_scr).astype(BF16)
    xs_b = xs.astype(BF16)

    dtv = dt_ref[rows, :] + dtb_ref[...]
    dt = jnp.maximum(dtv, 0.0) + jnp.log(1.0 + jnp.exp(-jnp.abs(dtv)))
    a = -jnp.exp(alog_ref[...]) * LOG2E
    row = lax.broadcasted_iota(jnp.int32, (L, L), 0)
    col = lax.broadcasted_iota(jnp.int32, (L, L), 1)
    causal = col <= row
    a_cum = _dot_split3(jnp.where(causal, 1.0, 0.0).astype(BF16), dt * a)
    a_cumT = a_cum.T
    dtT = dt.T
    a_last = a_cum[L - 1:L, :]
    a_lastT = a_cumT[:, L - 1:L]
    wT = dtT * jnp.exp2(a_lastT - a_cumT)
    e_last = jnp.exp2(a_last)
    lo = lax.broadcasted_iota(jnp.int32, (1, LANES), 1) < SSM_HD

    ys = []
    for g in range(SSM_GROUPS):
        Bg = bcm[:, g * SSM_STATE:(g + 1) * SSM_STATE]
        Cg = bcm[:, (SSM_GROUPS + g) * SSM_STATE:(SSM_GROUPS + g + 1) * SSM_STATE]
        cb = _dot_nt(Cg, Bg)
        BgT = Bg.astype(F32).T
        for jp in range(ppg):
            pj = g * ppg + jp
            h0, h1 = 2 * pj, 2 * pj + 1
            xp = xs_b[:, pj * LANES:(pj + 1) * LANES]

            def head_mats(h):
                ecol = jnp.broadcast_to(a_cum[:, h:h + 1], (L, L))
                dec = jnp.exp2(jnp.where(causal, ecol - a_cumT[h:h + 1, :], NEG))
                gm = (cb * dec * dtT[h:h + 1, :]).astype(BF16)
                bw = (BgT * wT[h:h + 1, :]).astype(BF16)
                return gm, bw, jnp.exp2(ecol)

            g0, b0, e0 = head_mats(h0)
            g1, b1, e1 = head_mats(h1)
            y_d = jnp.where(lo, _dot(g0, xp), _dot(g1, xp))
            st_prev = st_scr[pj]
            y_o = _dot(Cg, st_prev.astype(BF16)) * jnp.where(lo, e0, e1)
            st_new = jnp.where(lo, _dot(b0, xp), _dot(b1, xp))
            cdec = jnp.where(lo, jnp.broadcast_to(e_last[:, h0:h0 + 1], (1, LANES)),
                             jnp.broadcast_to(e_last[:, h1:h1 + 1], (1, LANES)))
            st_scr[pj] = st_prev * cdec + st_new
            sl = slice(pj * LANES, (pj + 1) * LANES)
            ys.append(y_d + y_o + dsk_ref[:, sl] * xs[:, sl])
    y = jnp.concatenate(ys, axis=1)
    y = y * _silu(z_ref[rows, :].astype(F32))
    gw = SSM_INNER // SSM_GROUPS
    for g in range(SSM_GROUPS):
        yg = y[:, g * gw:(g + 1) * gw]
        ms = jnp.mean(yg * yg, axis=-1, keepdims=True)
        o_ref[rows, g * gw:(g + 1) * gw] = (yg * lax.rsqrt(ms + EPS) * nw_ref[:, g * gw:(g + 1) * gw]).astype(BF16)


def _ssd(zxm, dt, cwx, cwb, cbx, cbb, dtb, alog, dsk, nw, bsz, seq):
    L = SSM_CHUNK
    chunks = SSD_CHUNKS_PER_STEP
    rows = chunks * L
    nstep = seq // rows
    assert seq % rows == 0
    nbc = 2 * SSM_GROUPS * SSM_STATE
    full = lambda a: pl.BlockSpec(a.shape, lambda b, c: (0,) * a.ndim)
    rowblk = lambda width, idx: pl.BlockSpec((rows, width), lambda b, c: (b * nstep + c, idx))
    return pl.pallas_call(
        functools.partial(_ssd_kernel, chunks=chunks),
        out_shape=jax.ShapeDtypeStruct((bsz * seq, SSM_INNER), BF16),
        grid=(bsz, nstep),
        in_specs=[rowblk(SSM_INNER, 0), rowblk(SSM_INNER, 2), rowblk(nbc, 6), rowblk(LANES, 0),
                  full(cwx), full(cwb), full(cbx), full(cbb), full(dtb), full(alog), full(dsk), full(nw)],
        out_specs=pl.BlockSpec((rows, SSM_INNER), lambda b, c: (b * nstep + c, 0)),
        scratch_shapes=[pltpu.VMEM((L + SSM_PAD, SSM_INNER), BF16),
                        pltpu.VMEM((L + SSM_PAD, nbc), BF16),
                        pltpu.VMEM((SSM_HEADS // 2, SSM_STATE, LANES), F32)],
        compiler_params=_cparams(("parallel", "arbitrary")),
        name="ssd",
    )(zxm, zxm, zxm, dt, cwx, cwb, cbx, cbb, dtb, alog, dsk, nw)


def _outproj_kernel(oT_ref, os_ref, ga_ref, gs_ref, x_ref, mod_ref, wa_ref, ws_ref, wo_ref, o_ref):
    oT = jnp.concatenate([oT_ref[0, j] for j in range(oT_ref.shape[1])], axis=1)
    ya = _dot_tn(oT, wa_ref[...])
    ys = _dot(os_ref[...], ws_ref[...])
    m = ga_ref[...].astype(F32) * ya + gs_ref[...].astype(F32) * ys
    r = _dot(m.astype(BF16), wo_ref[...])
    o_ref[...] = x_ref[...] + mod_ref[0][2:3] * r


def _outproj(oT, ossm, zxm, x2d, mod, wa, ws, wo, bsz, seq):
    d = x2d.shape[1]
    tm = 512
    tpb = seq // tm
    full = lambda a: pl.BlockSpec(a.shape, lambda i: (0,) * a.ndim)
    return pl.pallas_call(
        _outproj_kernel,
        out_shape=jax.ShapeDtypeStruct((bsz * seq, d), F32),
        grid=(bsz * tpb,),
        in_specs=[pl.BlockSpec((1, tm // TQ, N_HEADS * HD, TQ), lambda i: (i // tpb, i % tpb, 0, 0)),
                  pl.BlockSpec((tm, SSM_INNER), lambda i: (i, 0)),
                  pl.BlockSpec((tm, d), lambda i: (i, 2)),
                  pl.BlockSpec((tm, d), lambda i: (i, 3)),
                  pl.BlockSpec((tm, d), lambda i: (i, 0)),
                  pl.BlockSpec((1, 6, d), lambda i: (i // tpb, 0, 0)),
                  full(wa), full(ws), full(wo)],
        out_specs=pl.BlockSpec((tm, d), lambda i: (i, 0)),
        compiler_params=_cparams(("parallel",)),
        name="outproj",
    )(oT, ossm, zxm, zxm, x2d, mod, wa, ws, wo)


def _ffn_kernel(x_ref, mod_ref, n2_ref, nf_ref, wg_ref, wu_ref, wd_ref, o_ref, *, final_norm):
    x = x_ref[...]
    mod = mod_ref[0]
    h = _norm_mod(x, n2_ref[...], mod[4:5], mod[3:4]).astype(BF16)
    a = _silu(_dot(h, wg_ref[...])) * _dot(h, wu_ref[...])
    r = _dot(a.astype(BF16), wd_ref[...])
    x2 = x + mod[5:6] * r
    if final_norm:
        ms = jnp.mean(x2 * x2, axis=-1, keepdims=True)
        x2 = x2 * lax.rsqrt(ms + EPS) * nf_ref[...]
    o_ref[...] = x2


def _ffn(x1, mod, n2, nf, wg, wu, wd, final_norm, bsz, seq):
    d = x1.shape[1]
    tm = 512
    tpb = seq // tm
    const = lambda a: pl.BlockSpec(a.shape, lambda i: (0,) * a.ndim, pipeline_mode=pl.Buffered(1))
    return pl.pallas_call(
        functools.partial(_ffn_kernel, final_norm=final_norm),
        out_shape=jax.ShapeDtypeStruct((bsz * seq, d), F32),
        grid=(bsz * tpb,),
        in_specs=[pl.BlockSpec((tm, d), lambda i: (i, 0)),
                  pl.BlockSpec((1, 6, d), lambda i: (i // tpb, 0, 0)),
                  const(n2), const(nf), const(wg), const(wu), const(wd)],
        out_specs=pl.BlockSpec((tm, d), lambda i: (i, 0)),
        compiler_params=_cparams(("parallel",)),
        name="ffn",
    )(x1, mod, n2, nf, wg, wu, wd)


def _rotary_tables(seq):
    inv_freq = jnp.asarray(np.power(ROPE_THETA, -np.arange(0, 2 * ROT_HALF, 2) / (2 * ROT_HALF)).astype(np.float32))
    ang = jnp.arange(seq).astype(F32)[:, None] * inv_freq[None, :]
    cos, sin = jnp.cos(ang), jnp.sin(ang)
    ones = jnp.ones((seq, HD - 2 * ROT_HALF), F32)
    zeros8 = jnp.zeros((seq, ROT_HALF), F32)
    zeros = jnp.zeros((seq, HD - 2 * ROT_HALF), F32)
    cosK = jnp.concatenate([cos, cos, ones], axis=1)
    sinA = jnp.concatenate([-sin, zeros8, zeros], axis=1)
    sinB = jnp.concatenate([zeros8, sin, zeros], axis=1)
    rep = LANES // HD
    tile = lambda t: jnp.concatenate([t] * rep, axis=1)
    return cos.T, sin.T, tile(cosK), tile(sinA), tile(sinB)


def _selection_block_map_T(seq):
    nc = (seq - CMP_BLOCK) // CMP_STRIDE + 1
    nb = seq // SLC_BLOCK
    cs = np.arange(nc) * CMP_STRIDE
    bs = np.arange(nb) * SLC_BLOCK
    ov = np.minimum(cs[:, None] + CMP_BLOCK, bs[None, :] + SLC_BLOCK) - np.maximum(cs[:, None], bs[None, :])
    bmap = (np.clip(ov, 0, None) / CMP_BLOCK).astype(np.float32)
    out = np.zeros((nb, seq // CMP_STRIDE), np.float32)
    out[:, :nc] = bmap.T
    return jnp.asarray(out)


def kernel(x, c, w_ada, b_ada, norm1_w, w_in, cmp_pe_k, cmp_w1_k, cmp_w2_k, cmp_pe_v, cmp_w1_v, cmp_w2_v,
           conv_w, conv_b, dt_bias, a_log, d_skip, ssm_norm_w, w_attn_out, w_ssm_out, w_o, norm2_w,
           w_gate, w_up, w_down, norm_f_w):
    bsz, seq, d = x.shape
    depth = w_ada.shape[0]
    q_w = N_HEADS * HD
    kv_w = N_GROUPS * HD
    xbc_w = SSM_INNER + 2 * SSM_GROUPS * SSM_STATE
    offs = np.cumsum([q_w, 6 * kv_w, 3 * N_HEADS, SSM_INNER, xbc_w, SSM_HEADS])
    tabs = _rotary_tables(seq)
    bmapT = _selection_block_map_T(seq)
    x2d = x.reshape(bsz * seq, d)
    for l in range(depth):
        mod = _ada_mod(c, w_ada[l], b_ada[l]).reshape(bsz, 6, d)
        wi = w_in[l]
        wq = wi[:, :offs[0]]
        wkv = wi[:, offs[0]:offs[1]].reshape(d, 6, N_GROUPS * HD)
        wg = wi[:, offs[1]:offs[2]].reshape(d, N_GROUPS, HPG, 3)
        wz = wi[:, offs[2]:offs[3]]
        wxbc = wi[:, offs[3]:offs[4]]
        wdt = wi[:, offs[4]:offs[5]]
        wmg = wi[:, offs[5]:]
        wqT = wq.T.astype(BF16)
        zcol = jnp.zeros((d, HD), wi.dtype)
        wkn = jnp.concatenate([wkv[:, 2, :HD], zcol, wkv[:, 2, HD:], zcol, wkv[:, 0], wkv[:, 4], wkv[:, 1]],
                              axis=1).astype(BF16)
        wvT = jnp.concatenate([wkv[:, 3], wkv[:, 5]], axis=1).T.astype(BF16)
        wgT = jnp.transpose(wg, (1, 3, 2, 0)).reshape(3 * N_HEADS, d).astype(BF16)
        wdtp = jnp.pad(wdt, ((0, 0), (0, LANES - SSM_HEADS))).astype(BF16)
        wbig = jnp.concatenate([wz, wmg, wxbc], axis=1).astype(BF16)
        nw1 = norm1_w[l].reshape(1, d)

        qT, ksa, kn, kcv, vT, gT, dt = _inproj_attn(x2d, mod, nw1, wqT, wkn, wvT, wgT, wdtp, tabs, bsz, seq)
        zxm = _inproj_big(x2d, mod, nw1, wbig, bsz, seq)

        kc = _compress(kcv, 0, cmp_pe_k[l], cmp_w1_k[l], cmp_w2_k[l], False, bsz, seq)
        vcT = _compress(kcv, N_GROUPS, cmp_pe_v[l], cmp_w1_v[l], cmp_w2_v[l], True, bsz, seq)
        oT = _nsa(qT, kc, vcT, ksa, kn, vT, gT, bmapT.astype(BF16), bsz, seq)

        cw = conv_w[l]
        cb = conv_b[l].reshape(1, xbc_w)
        pad_h = lambda v: jnp.pad(v.reshape(1, SSM_HEADS), ((0, 0), (0, LANES - SSM_HEADS)))
        dsk = jnp.repeat(d_skip[l], SSM_HD).reshape(1, SSM_INNER)
        ossm = _ssd(zxm, dt, cw[:, :SSM_INNER], cw[:, SSM_INNER:], cb[:, :SSM_INNER], cb[:, SSM_INNER:],
                    pad_h(dt_bias[l]), pad_h(a_log[l]), dsk, ssm_norm_w[l].reshape(1, SSM_INNER), bsz, seq)

        x1 = _outproj(oT, ossm, zxm, x2d, mod, w_attn_out[l].astype(BF16), w_ssm_out[l].astype(BF16),
                      w_o[l].astype(BF16), bsz, seq)
        x2d = _ffn(x1, mod, norm2_w[l].reshape(1, d), norm_f_w.reshape(1, d), w_gate[l].astype(BF16),
                   w_up[l].astype(BF16), w_down[l].astype(BF16), l == depth - 1, bsz, seq)
    return x2d.reshape(bsz, seq, d)
```

```python
import functools

import numpy as np
import jax
import jax.numpy as jnp
from jax import lax
from jax.experimental import pallas as pl
from jax.experimental.pallas import tpu as pltpu

F32 = jnp.float32
BF16 = jnp.bfloat16
HIGHEST = lax.Precision.HIGHEST

EPS = 1e-6
NEG = -1e30

N_HEADS = 16
N_GROUPS = 2
HPG = N_HEADS // N_GROUPS
HD = 64
ROT_HALF = 8
ROPE_THETA = 500000.0
CMP_BLOCK = 32
CMP_STRIDE = 16
SLC_BLOCK = 64
SLC_TOPK = 16
WINDOW = 512

SSM_HEADS = 32
SSM_HD = 64
SSM_GROUPS = 4
SSM_STATE = 128
SSM_CONV = 4
SSM_CHUNK = 128
SSM_INNER = SSM_HEADS * SSM_HD
SSM_PAD = 16

TQ = 128
NSA_TILES_PER_STEP = 4
SSD_CHUNKS_PER_STEP = 4
KS = 512
LANES = 128
VROWS = HD + 16
LOG2E = float(np.log2(np.e))
VMEM_LIMIT = 56 * 1024 * 1024


def _cparams(sem, vmem=VMEM_LIMIT):
    return pltpu.CompilerParams(dimension_semantics=sem, vmem_limit_bytes=vmem)


def _sigmoid(x):
    return 0.5 + 0.5 * jnp.tanh(0.5 * x)


def _silu(x):
    hx = 0.5 * x
    return hx + hx * jnp.tanh(hx)


def _norm_mod(x, w, sc, sh):
    ms = jnp.mean(x * x, axis=-1, keepdims=True)
    return (x * lax.rsqrt(ms + EPS) * w) * (1.0 + sc) + sh


def _dot(a, b):
    return jnp.dot(a, b, preferred_element_type=F32)


def _dot_split3(a, x):
    hi = x.astype(BF16)
    r1 = x - hi.astype(F32)
    mid = r1.astype(BF16)
    lo = (r1 - mid.astype(F32)).astype(BF16)
    return _dot(a, hi) + _dot(a, mid) + _dot(a, lo)


def _dot_nt(a, b):
    return lax.dot_general(a, b, (((1,), (1,)), ((), ())), preferred_element_type=F32)


def _dot_tn(a, b):
    return lax.dot_general(a, b, (((0,), (0,)), ((), ())), preferred_element_type=F32)


def _ada_kernel(c_ref, w_ref, b_ref, o_ref):
    c = c_ref[...]
    o_ref[...] = jnp.dot(_silu(c), w_ref[...], precision=HIGHEST, preferred_element_type=F32) + b_ref[...]


def _ada_mod(c, w, b):
    bsz, d = c.shape
    n = w.shape[1]
    tn = 1024
    return pl.pallas_call(
        _ada_kernel,
        out_shape=jax.ShapeDtypeStruct((bsz, n), F32),
        grid=(n // tn,),
        in_specs=[pl.BlockSpec((bsz, d), lambda j: (0, 0)),
                  pl.BlockSpec((d, tn), lambda j: (0, j)),
                  pl.BlockSpec((1, tn), lambda j: (0, j))],
        out_specs=pl.BlockSpec((bsz, tn), lambda j: (0, j)),
        compiler_params=_cparams(("parallel",)),
        name="ada_mod",
    )(c, w, b.reshape(1, n))


def _inproj_attn_kernel(x_ref, mod_ref, nw_ref, wqT_ref, wkn_ref, wvT_ref, wgT_ref, wdt_ref,
                        cosT_ref, sinT_ref, cosK_ref, sinA_ref, sinB_ref, perm_ref,
                        qT_ref, ksa_ref, kn_ref, kcv_ref, vT_ref, gT_ref, dt_ref):
    tm = x_ref.shape[0]
    mod = mod_ref[0]
    h = _norm_mod(x_ref[...], nw_ref[...], mod[1:2], mod[0:1]).astype(BF16)

    qT = _dot_nt(wqT_ref[...], h)
    cT = cosT_ref[...]
    sT = sinT_ref[...]
    scale = HD ** -0.5 * LOG2E
    for hh in range(N_HEADS):
        r0 = hh * HD
        x1 = qT[r0:r0 + ROT_HALF]
        x2 = qT[r0 + ROT_HALF:r0 + 2 * ROT_HALF]
        blk = jnp.concatenate([x1 * cT - x2 * sT, x2 * cT + x1 * sT, qT[r0 + 2 * ROT_HALF:r0 + HD]], axis=0)
        blk = (blk * scale).astype(BF16)
        for j in range(tm // TQ):
            qT_ref[0, j, r0:r0 + HD, :] = blk[:, j * TQ:(j + 1) * TQ]

    kn = _dot(h, wkn_ref[...])
    nrot = 8 * HD
    kk = kn[:, :nrot]
    reps = nrot // LANES
    cK = jnp.concatenate([cosK_ref[...]] * reps, axis=1)
    sA = jnp.concatenate([sinA_ref[...]] * reps, axis=1)
    sB = jnp.concatenate([sinB_ref[...]] * reps, axis=1)
    kk = kk * cK + pltpu.roll(kk, nrot - ROT_HALF, 1) * sA + pltpu.roll(kk, ROT_HALF, 1) * sB
    row = lax.broadcasted_iota(jnp.int32, (tm, LANES), 0)
    lane = lax.broadcasted_iota(jnp.int32, (tm, LANES), 1)
    bps = KS // SLC_BLOCK
    blk = jnp.right_shift(row, int(np.log2(SLC_BLOCK)))
    onehot = jnp.where(lane == HD + (blk & (bps - 1)), 1.0, 0.0)
    for g in range(N_GROUPS):
        ksa_ref[0, g] = (kk[:, g * LANES:(g + 1) * LANES] + onehot).astype(BF16)
    for i in range(N_GROUPS):
        kn_ref[0, i] = kk[:, (6 + i) * HD:(7 + i) * HD].astype(BF16)

    nr = tm // CMP_STRIDE
    lo = lax.broadcasted_iota(jnp.int32, (nr, LANES), 1) < HD
    for i, pk in enumerate((kk[:, 4 * HD:6 * HD], kn[:, nrot:nrot + 2 * HD])):
        pm = _dot(perm_ref[...], pk.astype(BF16))
        for j in range(CMP_STRIDE // 2):
            a = pm[2 * j * nr:(2 * j + 1) * nr]
            b = pm[(2 * j + 1) * nr:(2 * j + 2) * nr]
            kcv_ref[0, 2 * i, :, j * LANES:(j + 1) * LANES] = jnp.where(lo, a, pltpu.roll(b, HD, 1)).astype(BF16)
            kcv_ref[0, 2 * i + 1, :, j * LANES:(j + 1) * LANES] = jnp.where(lo, pltpu.roll(a, HD, 1), b).astype(BF16)

    vT = _dot_nt(wvT_ref[...], h).astype(BF16)
    ones = jnp.ones((VROWS - HD, LANES), BF16)
    for j in range(tm // LANES):
        for r in range(4):
            vT_ref[0, j, r * VROWS:r * VROWS + HD, :] = vT[r * HD:(r + 1) * HD, j * LANES:(j + 1) * LANES]
            vT_ref[0, j, r * VROWS + HD:(r + 1) * VROWS, :] = ones

    gates = _sigmoid(_dot_nt(wgT_ref[...], h))
    for j in range(tm // TQ):
        for g in range(N_GROUPS):
            gT_ref[0, j, g] = gates[g * 3 * HPG:(g + 1) * 3 * HPG, j * TQ:(j + 1) * TQ]
    dt_ref[...] = _dot(h, wdt_ref[...])


def _inproj_attn(x2d, mod, nw, wqT, wkn, wvT, wgT, wdt, tabs, bsz, seq):
    d = x2d.shape[1]
    tm = KS
    tpb = seq // tm
    cosT, sinT, cosK, sinA, sinB = tabs
    full = lambda a: pl.BlockSpec(a.shape, lambda i: (0,) * a.ndim)
    orow = np.arange(tm)
    src = CMP_STRIDE * (orow % (tm // CMP_STRIDE)) + orow // (tm // CMP_STRIDE)
    perm = jnp.asarray(np.arange(tm)[None, :] == src[:, None], BF16)
    out_shape = (
        jax.ShapeDtypeStruct((bsz, seq // TQ, N_HEADS * HD, TQ), BF16),
        jax.ShapeDtypeStruct((bsz, N_GROUPS, seq, LANES), BF16),
        jax.ShapeDtypeStruct((bsz, N_GROUPS, seq, HD), BF16),
        jax.ShapeDtypeStruct((bsz, 2 * N_GROUPS, seq // CMP_STRIDE, CMP_STRIDE * HD), BF16),
        jax.ShapeDtypeStruct((bsz, seq // LANES, 4 * VROWS, LANES), BF16),
        jax.ShapeDtypeStruct((bsz, seq // TQ, N_GROUPS, 3 * HPG, TQ), F32),
        jax.ShapeDtypeStruct((bsz * seq, LANES), F32),
    )
    return pl.pallas_call(
        _inproj_attn_kernel,
        out_shape=out_shape,
        grid=(bsz * tpb,),
        in_specs=[pl.BlockSpec((tm, d), lambda i: (i, 0)),
                  pl.BlockSpec((1, 6, d), lambda i: (i // tpb, 0, 0)),
                  full(nw), full(wqT), full(wkn), full(wvT), full(wgT), full(wdt),
                  pl.BlockSpec((ROT_HALF, tm), lambda i: (0, i % tpb)),
                  pl.BlockSpec((ROT_HALF, tm), lambda i: (0, i % tpb)),
                  pl.BlockSpec((tm, LANES), lambda i: (i % tpb, 0)),
                  pl.BlockSpec((tm, LANES), lambda i: (i % tpb, 0)),
                  pl.BlockSpec((tm, LANES), lambda i: (i % tpb, 0)),
                  full(perm)],
        out_specs=(pl.BlockSpec((1, tm // TQ, N_HEADS * HD, TQ), lambda i: (i // tpb, i % tpb, 0, 0)),
                   pl.BlockSpec((1, N_GROUPS, tm, LANES), lambda i: (i // tpb, 0, i % tpb, 0)),
                   pl.BlockSpec((1, N_GROUPS, tm, HD), lambda i: (i // tpb, 0, i % tpb, 0)),
                   pl.BlockSpec((1, 2 * N_GROUPS, tm // CMP_STRIDE, CMP_STRIDE * HD),
                                lambda i: (i // tpb, 0, i % tpb, 0)),
                   pl.BlockSpec((1, tm // LANES, 4 * VROWS, LANES), lambda i: (i // tpb, i % tpb, 0, 0)),
                   pl.BlockSpec((1, tm // TQ, N_GROUPS, 3 * HPG, TQ), lambda i: (i // tpb, i % tpb, 0, 0, 0)),
                   pl.BlockSpec((tm, LANES), lambda i: (i, 0))),
        compiler_params=_cparams(("parallel",)),
        name="inproj_attn",
    )(x2d, mod, nw, wqT, wkn, wvT, wgT, wdt, cosT, sinT, cosK, sinA, sinB, perm)


def _inproj_big_kernel(x_ref, mod_ref, nw_ref, w_ref, o_ref, h_scr, *, gate_lo, gate_hi):
    j = pl.program_id(1)

    @pl.when(j == 0)
    def _():
        mod = mod_ref[0]
        h_scr[...] = _norm_mod(x_ref[...], nw_ref[...], mod[1:2], mod[0:1]).astype(BF16)

    acc = _dot(h_scr[...], w_ref[...])
    tn = acc.shape[1]
    col = j * tn + lax.broadcasted_iota(jnp.int32, (1, tn), 1)
    is_gate = (col >= gate_lo) & (col < gate_hi)
    o_ref[...] = jnp.where(is_gate, _sigmoid(acc), acc).astype(BF16)


def _inproj_big(x2d, mod, nw, wbig, bsz, seq):
    d = x2d.shape[1]
    n = wbig.shape[1]
    tm, tn = 1024, n // 4
    assert n % 4 == 0 and tn % LANES == 0
    tpb = seq // tm
    return pl.pallas_call(
        functools.partial(_inproj_big_kernel, gate_lo=SSM_INNER, gate_hi=SSM_INNER + 2 * d),
        out_shape=jax.ShapeDtypeStruct((bsz * seq, n), BF16),
        grid=(bsz * tpb, n // tn),
        in_specs=[pl.BlockSpec((tm, d), lambda i, j: (i, 0)),
                  pl.BlockSpec((1, 6, d), lambda i, j: (i // tpb, 0, 0)),
                  pl.BlockSpec((1, d), lambda i, j: (0, 0)),
                  pl.BlockSpec((d, tn), lambda i, j: (0, j))],
        out_specs=pl.BlockSpec((tm, tn), lambda i, j: (i, j)),
        scratch_shapes=[pltpu.VMEM((tm, d), BF16)],
        compiler_params=_cparams(("parallel", "arbitrary")),
        name="inproj_big",
    )(x2d, mod, nw, wbig)


def _compress_kernel(x_ref, pea_ref, peb_ref, w1a_ref, w1b_ref, w2_ref, o_ref, *, transpose_out, nc):
    x = x_ref[0, 0].astype(F32)
    nrow = x.shape[0]
    ya = _dot((x + pea_ref[...]).astype(BF16), w1a_ref[...])
    yb = _dot((x + peb_ref[...]).astype(BF16), w1b_ref[...])
    hp = ya + pltpu.roll(yb, nrow - 1, 0)
    row = lax.broadcasted_iota(jnp.int32, hp.shape, 0)
    hh = jnp.where(row < nc, _silu(hp), 0.0).astype(BF16)
    if transpose_out:
        o_ref[0, 0] = _dot_nt(w2_ref[...], hh).astype(BF16)
    else:
        o_ref[0, 0] = _dot(hh, w2_ref[...]).astype(BF16)


def _compress(kn16, slot0, pe, w1, w2, transpose_out, bsz, seq):
    nrow = seq // CMP_STRIDE
    nc = (seq - CMP_BLOCK) // CMP_STRIDE + 1
    half = CMP_STRIDE * HD
    pea = pe[:CMP_STRIDE].reshape(1, half)
    peb = pe[CMP_STRIDE:].reshape(1, half)
    w1a = w1[:half].astype(BF16)
    w1b = w1[half:].astype(BF16)
    if transpose_out:
        w2p = w2.T.astype(BF16)
        oshape, oblock = (bsz, N_GROUPS, HD, nrow), (1, 1, HD, nrow)
    else:
        w2p = w2.astype(BF16)
        oshape, oblock = (bsz, N_GROUPS, nrow, HD), (1, 1, nrow, HD)
    full = lambda a: pl.BlockSpec(a.shape, lambda b, g: (0,) * a.ndim)
    return pl.pallas_call(
        functools.partial(_compress_kernel, transpose_out=transpose_out, nc=nc),
        out_shape=jax.ShapeDtypeStruct(oshape, BF16),
        grid=(bsz, N_GROUPS),
        in_specs=[pl.BlockSpec((1, 1, nrow, half), lambda b, g: (b, slot0 + g, 0, 0)),
                  full(pea), full(peb), full(w1a), full(w1b), full(w2p)],
        out_specs=pl.BlockSpec(oblock, lambda b, g: (b, g, 0, 0)),
        compiler_params=_cparams(("parallel", "parallel")),
        name="compress_v" if transpose_out else "compress_k",
    )(kn16, pea, peb, w1a, w1b, w2p)


def _nsa_kernel(*refs, nb, tiles):
    def tile(u, carry):
        _nsa_tile(u, pl.program_id(2) * tiles + u, *refs, nb=nb)
        return carry

    lax.fori_loop(0, tiles, tile, 0)


def _nsa_tile(u, qi, q_ref, kc_ref, vcT_ref, ksa_ref, kw_ref, vsT_ref, vwT_ref, g_ref, bmapT_ref, o_ref,
              v_scr, selb_scr, m_scr, acc_scr, sa_scr, sb_scr, ma_scr, mb_scr, sw_scr, mw_scr, *, nb):
    t0 = qi * TQ
    wl = HPG * TQ
    q = q_ref[0, u]
    Q = jnp.concatenate([q[hh * HD:(hh + 1) * HD, :] for hh in range(HPG)], axis=1)
    tok1 = t0 + lax.broadcasted_iota(jnp.int32, (1, TQ), 1)
    tok = jnp.concatenate([tok1] * HPG, axis=1)

    def lanes8(x):
        return jnp.concatenate([x] * HPG, axis=1)

    nwt = WINDOW // TQ + 1
    kt0 = jnp.maximum(qi - WINDOW // TQ, 0)
    kw0 = pl.multiple_of(kt0 * TQ, TQ)
    sw = _dot(kw_ref[0, 0, pl.ds(kw0, nwt * TQ), :], Q)
    diff = tok1 - (kw0 + lax.broadcasted_iota(jnp.int32, (nwt * TQ, 1), 0))
    sw = sw + lanes8(jnp.where((diff >= 0) & (diff < WINDOW), 0.0, NEG))
    sw_scr[...] = sw
    mw_scr[...] = jnp.max(sw, axis=0, keepdims=True)

    ncp = kc_ref.shape[2]
    sc = _dot(kc_ref[0, 0], Q)
    n_idx = lax.broadcasted_iota(jnp.int32, (ncp, 1), 0)
    mask_c = (n_idx * CMP_STRIDE + (CMP_BLOCK - 1)) <= tok
    sc = jnp.where(mask_c, sc, NEG)
    mc = jnp.max(sc, axis=0, keepdims=True)
    ec = jnp.exp2(sc - mc)
    has_blk = tok >= CMP_BLOCK - 1
    pc = ec * jnp.where(has_blk, 1.0 / jnp.sum(ec, axis=0, keepdims=True), 0.0)
    o_c = _dot(vcT_ref[0, 0], pc.astype(BF16))

    psum = pc[:, 0:TQ]
    for hh in range(1, HPG):
        psum = psum + pc[:, hh * TQ:(hh + 1) * TQ]
    imp = _dot_split3(bmapT_ref[...], psum)
    jb = lax.broadcasted_iota(jnp.int32, (nb, 1), 0)
    cur = jnp.right_shift(tok1, int(np.log2(SLC_BLOCK)))
    forced = (jb == 0) | (jb == cur) | (jb == cur - 1)
    v = jnp.where(forced, jnp.inf, jnp.where(jb <= cur, imp, -jnp.inf))
    v_scr[...] = v
    jbf = lax.broadcasted_iota(jnp.int32, (nb, TQ), 0)

    def rank_body(i2, rank):
        for u in range(2):
            i = 2 * i2 + u
            vi = v_scr[pl.ds(i, 1), :]
            ge = jnp.where(vi >= v, 1.0, 0.0)
            gt = jnp.where(vi > v, 1.0, 0.0)
            rank = rank + jnp.where(jbf > i, ge, gt)
        return rank

    n_live = (TQ // SLC_BLOCK) * (qi + 1)
    rank_trips = jnp.where(n_live <= SLC_TOPK, 0, jnp.minimum(n_live, nb) // 2)
    rank = lax.fori_loop(0, rank_trips, rank_body, jnp.zeros((nb, TQ), F32))
    selb_scr[...] = jnp.where(rank < min(SLC_TOPK, nb), 0.0, NEG)

    tps = KS // TQ
    bps = KS // SLC_BLOCK
    m_scr[...] = jnp.full((1, wl), NEG, F32)
    acc_scr[...] = jnp.zeros((VROWS, wl), F32)
    st_hi = qi // tps

    def scores(st, s_scr, mx_scr, bias=None):
        sb = selb_scr[pl.ds(pl.multiple_of(st * bps, bps), bps), :]
        ext = lanes8(jnp.concatenate([sb, jnp.zeros((16 - bps, TQ), F32)], axis=0)).astype(BF16)
        qa = jnp.concatenate([Q, ext, jnp.zeros((LANES - HD - 16, wl), BF16)], axis=0)
        s = _dot(ksa_ref[0, 0, pl.ds(pl.multiple_of(st * KS, KS), KS), :], qa)
        if bias is not None:
            s = s + bias
        s_scr[...] = s
        mx_scr[...] = jnp.max(s, axis=0, keepdims=True)

    def update(s_scr, mx_scr, st):
        m_old = m_scr[...]
        m_new = jnp.maximum(m_old, mx_scr[...])
        p = jnp.exp2(s_scr[...] - m_new).astype(BF16)
        vs = jnp.concatenate([vsT_ref[0, st * tps + j] for j in range(tps)], axis=1)
        acc_scr[...] = acc_scr[...] * jnp.exp2(m_old - m_new) + _dot(vs, p)
        m_scr[...] = m_new

    kpos = st_hi * KS + lax.broadcasted_iota(jnp.int32, (KS, 1), 0)
    scores(st_hi, sa_scr, ma_scr, lanes8(jnp.where(kpos <= tok1, 0.0, NEG)))

    pw = jnp.exp2(sw_scr[...] - mw_scr[...]).astype(BF16)
    vwin = jnp.concatenate([vwT_ref[0, kt0 + j] for j in range(nwt)], axis=1)
    ow = _dot(vwin, pw)
    o_w = ow[:HD] / ow[HD:HD + 1]

    def pair_body(j, carry):
        ta = st_hi - 2 * j
        scores(ta - 1, sb_scr, mb_scr)
        update(sa_scr, ma_scr, ta)
        scores(ta - 2, sa_scr, ma_scr)
        update(sb_scr, mb_scr, ta - 1)
        return carry

    n_pairs = st_hi // 2
    lax.fori_loop(0, n_pairs, pair_body, 0)

    @pl.when(st_hi - 2 * n_pairs == 1)
    def _():
        scores(0, sb_scr, mb_scr)
        update(sa_scr, ma_scr, 1)
        update(sb_scr, mb_scr, 0)

    @pl.when(st_hi - 2 * n_pairs == 0)
    def _():
        update(sa_scr, ma_scr, 0)

    o_s = acc_scr[0:HD, :] / acc_scr[HD:HD + 1, :]

    gts = g_ref[0, u, 0]
    for hh in range(HPG):
        sl = slice(hh * TQ, (hh + 1) * TQ)
        o = (gts[hh:hh + 1] * o_c[:, sl] + gts[HPG + hh:HPG + hh + 1] * o_s[:, sl]
             + gts[2 * HPG + hh:2 * HPG + hh + 1] * o_w[:, sl])
        o_ref[0, u, hh * HD:(hh + 1) * HD, :] = o.astype(BF16)


def _nsa(qT, kc, vcT, ksa, kn, vT, gT, bmapT, bsz, seq):
    nq = seq // TQ
    nb = seq // SLC_BLOCK
    ncp = kc.shape[2]
    wl = HPG * TQ
    tiles = NSA_TILES_PER_STEP
    assert seq % KS == 0 and seq >= WINDOW + TQ and KS // SLC_BLOCK == 8 and nq % tiles == 0
    return pl.pallas_call(
        functools.partial(_nsa_kernel, nb=nb, tiles=tiles),
        out_shape=jax.ShapeDtypeStruct((bsz, nq, N_HEADS * HD, TQ), BF16),
        grid=(bsz, N_GROUPS, nq // tiles),
        in_specs=[pl.BlockSpec((1, tiles, HPG * HD, TQ), lambda b, g, i: (b, i, g, 0)),
                  pl.BlockSpec((1, 1, ncp, HD), lambda b, g, i: (b, g, 0, 0)),
                  pl.BlockSpec((1, 1, HD, ncp), lambda b, g, i: (b, g, 0, 0)),
                  pl.BlockSpec((1, 1, seq, LANES), lambda b, g, i: (b, g, 0, 0)),
                  pl.BlockSpec((1, 1, seq, HD), lambda b, g, i: (b, g, 0, 0)),
                  pl.BlockSpec((1, nq, VROWS, LANES), lambda b, g, i: (b, 0, g, 0)),
                  pl.BlockSpec((1, nq, VROWS, LANES), lambda b, g, i: (b, 0, 2 + g, 0)),
                  pl.BlockSpec((1, tiles, 1, 3 * HPG, TQ), lambda b, g, i: (b, i, g, 0, 0)),
                  pl.BlockSpec(bmapT.shape, lambda b, g, i: (0, 0))],
        out_specs=pl.BlockSpec((1, tiles, HPG * HD, TQ), lambda b, g, i: (b, i, g, 0)),
        scratch_shapes=[pltpu.VMEM((nb, TQ), F32),
                        pltpu.VMEM((nb, TQ), F32),
                        pltpu.VMEM((1, wl), F32),
                        pltpu.VMEM((VROWS, wl), F32),
                        pltpu.VMEM((KS, wl), F32),
                        pltpu.VMEM((KS, wl), F32),
                        pltpu.VMEM((1, wl), F32),
                        pltpu.VMEM((1, wl), F32),
                        pltpu.VMEM((WINDOW + TQ, wl), F32),
                        pltpu.VMEM((1, wl), F32)],
        compiler_params=_cparams(("parallel", "parallel", "arbitrary")),
        name="nsa_attention",
    )(qT, kc, vcT, ksa, kn, vT, vT, gT, bmapT)


def _ssd_kernel(*refs, chunks):
    def chunk(u, carry):
        _ssd_chunk(u, *refs)
        return carry

    lax.fori_loop(0, chunks, chunk, 0)


def _ssd_chunk(u, z_ref, xs_ref, bc_ref, dt_ref, cwx_ref, cwb_ref, cbx_ref, cbb_ref, dtb_ref, alog_ref,
               dsk_ref, nw_ref, o_ref, xpx_scr, xpb_scr, st_scr):
    L = SSM_CHUNK
    rows = pl.ds(pl.multiple_of(u * L, L), L)
    pad = SSM_PAD
    npair = SSM_HEADS // 2
    ppg = npair // SSM_GROUPS

    @pl.when((pl.program_id(1) == 0) & (u == 0))
    def _():
        xpx_scr[0:pad, :] = jnp.zeros((pad, xpx_scr.shape[1]), BF16)
        xpb_scr[0:pad, :] = jnp.zeros((pad, xpb_scr.shape[1]), BF16)
        st_scr[...] = jnp.zeros(st_scr.shape, F32)

    nshift = SSM_CONV - 1
    srow = lax.broadcasted_iota(jnp.int32, (nshift * L, pad + L), 0)
    scol = lax.broadcasted_iota(jnp.int32, (nshift * L, pad + L), 1)
    shift = jnp.where(scol == (srow & (L - 1)) + jnp.right_shift(srow, int(np.log2(L))) + (pad - nshift),
                      1.0, 0.0).astype(BF16)

    def conv_silu(x_ref, w_ref, b_ref, scr):
        x = x_ref[rows, :]
        scr[pad:pad + L, :] = x
        sh = _dot(shift, scr[...])
        acc = b_ref[...] + w_ref[nshift:nshift + 1, :] * x.astype(F32)
        for k in range(nshift):
            acc = acc + w_ref[k:k + 1, :] * sh[k * L:(k + 1) * L]
        scr[0:pad, :] = scr[L:L + pad, :]
        return _silu(acc)

    xs = conv_silu(xs_ref, cwx_ref, cbx_ref, xpx_scr)
    bcm = conv_silu(bc_ref, cwb_ref, cbb_ref, xpb_scr).astype(BF16)
    xs_b = xs.astype(BF16)

    dtv = dt_ref[rows, :] + dtb_ref[...]
    dt = jnp.maximum(dtv, 0.0) + jnp.log(1.0 + jnp.exp(-jnp.abs(dtv)))
    a = -jnp.exp(alog_ref[...]) * LOG2E
    row = lax.broadcasted_iota(jnp.int32, (L, L), 0)
    col = lax.broadcasted_iota(jnp.int32, (L, L), 1)
    causal = col <= row
    a_cum = _dot_split3(jnp.where(causal, 1.0, 0.0).astype(BF16), dt * a)
    a_cumT = a_cum.T
    dtT = dt.T
    a_last = a_cum[L - 1:L, :]
    a_lastT = a_cumT[:, L - 1:L]
    wT = dtT * jnp.exp2(a_lastT - a_cumT)
    e_last = jnp.exp2(a_last)
    lo = lax.broadcasted_iota(jnp.int32, (1, LANES), 1) < SSM_HD

    ys = []
    for g in range(SSM_GROUPS):
        Bg = bcm[:, g * SSM_STATE:(g + 1) * SSM_STATE]
        Cg = bcm[:, (SSM_GROUPS + g) * SSM_STATE:(SSM_GROUPS + g + 1) * SSM_STATE]
        cb = _dot_nt(Cg, Bg)
        BgT = Bg.astype(F32).T
        for jp in range(ppg):
            pj = g * ppg + jp
            h0, h1 = 2 * pj, 2 * pj + 1
            xp = xs_b[:, pj * LANES:(pj + 1) * LANES]

            def head_mats(h):
                ecol = jnp.broadcast_to(a_cum[:, h:h + 1], (L, L))
                dec = jnp.exp2(jnp.where(causal, ecol - a_cumT[h:h + 1, :], NEG))
                gm = (cb * dec * dtT[h:h + 1, :]).astype(BF16)
                bw = (BgT * wT[h:h + 1, :]).astype(BF16)
                return gm, bw, jnp.exp2(ecol)

            g0, b0, e0 = head_mats(h0)
            g1, b1, e1 = head_mats(h1)
            y_d = jnp.where(lo, _dot(g0, xp), _dot(g1, xp))
            st_prev = st_scr[pj]
            y_o = _dot(Cg, st_prev.astype(BF16)) * jnp.where(lo, e0, e1)
            st_new = jnp.where(lo, _dot(b0, xp), _dot(b1, xp))
            cdec = jnp.where(lo, jnp.broadcast_to(e_last[:, h0:h0 + 1], (1, LANES)),
                             jnp.broadcast_to(e_last[:, h1:h1 + 1], (1, LANES)))
            st_scr[pj] = st_prev * cdec + st_new
            sl = slice(pj * LANES, (pj + 1) * LANES)
            ys.append(y_d + y_o + dsk_ref[:, sl] * xs[:, sl])
    y = jnp.concatenate(ys, axis=1)
    y = y * _silu(z_ref[rows, :].astype(F32))
    gw = SSM_INNER // SSM_GROUPS
    for g in range(SSM_GROUPS):
        yg = y[:, g * gw:(g + 1) * gw]
        ms = jnp.mean(yg * yg, axis=-1, keepdims=True)
        o_ref[rows, g * gw:(g + 1) * gw] = (yg * lax.rsqrt(ms + EPS) * nw_ref[:, g * gw:(g + 1) * gw]).astype(BF16)


def _ssd(zxm, dt, cwx, cwb, cbx, cbb, dtb, alog, dsk, nw, bsz, seq):
    L = SSM_CHUNK
    chunks = SSD_CHUNKS_PER_STEP
    rows = chunks * L
    nstep = seq // rows
    assert seq % rows == 0
    nbc = 2 * SSM_GROUPS * SSM_STATE
    full = lambda a: pl.BlockSpec(a.shape, lambda b, c: (0,) * a.ndim)
    rowblk = lambda width, idx: pl.BlockSpec((rows, width), lambda b, c: (b * nstep + c, idx))
    return pl.pallas_call(
        functools.partial(_ssd_kernel, chunks=chunks),
        out_shape=jax.ShapeDtypeStruct((bsz * seq, SSM_INNER), BF16),
        grid=(bsz, nstep),
        in_specs=[rowblk(SSM_INNER, 0), rowblk(SSM_INNER, 2), rowblk(nbc, 6), rowblk(LANES, 0),
                  full(cwx), full(cwb), full(cbx), full(cbb), full(dtb), full(alog), full(dsk), full(nw)],
        out_specs=pl.BlockSpec((rows, SSM_INNER), lambda b, c: (b * nstep + c, 0)),
        scratch_shapes=[pltpu.VMEM((L + SSM_PAD, SSM_INNER), BF16),
                        pltpu.VMEM((L + SSM_PAD, nbc), BF16),
                        pltpu.VMEM((SSM_HEADS // 2, SSM_STATE, LANES), F32)],
        compiler_params=_cparams(("parallel", "arbitrary")),
        name="ssd",
    )(zxm, zxm, zxm, dt, cwx, cwb, cbx, cbb, dtb, alog, dsk, nw)


def _outproj_kernel(oT_ref, os_ref, ga_ref, gs_ref, x_ref, mod_ref, wa_ref, ws_ref, wo_ref, o_ref):
    oT = jnp.concatenate([oT_ref[0, j] for j in range(oT_ref.shape[1])], axis=1)
    ya = _dot_tn(oT, wa_ref[...])
    ys = _dot(os_ref[...], ws_ref[...])
    m = ga_ref[...].astype(F32) * ya + gs_ref[...].astype(F32) * ys
    r = _dot(m.astype(BF16), wo_ref[...])
    o_ref[...] = x_ref[...] + mod_ref[0][2:3] * r


def _outproj(oT, ossm, zxm, x2d, mod, wa, ws, wo, bsz, seq):
    d = x2d.shape[1]
    tm = 512
    tpb = seq // tm
    full = lambda a: pl.BlockSpec(a.shape, lambda i: (0,) * a.ndim)
    return pl.pallas_call(
        _outproj_kernel,
        out_shape=jax.ShapeDtypeStruct((bsz * seq, d), F32),
        grid=(bsz * tpb,),
        in_specs=[pl.BlockSpec((1, tm // TQ, N_HEADS * HD, TQ), lambda i: (i // tpb, i % tpb, 0, 0)),
                  pl.BlockSpec((tm, SSM_INNER), lambda i: (i, 0)),
                  pl.BlockSpec((tm, d), lambda i: (i, 2)),
                  pl.BlockSpec((tm, d), lambda i: (i, 3)),
                  pl.BlockSpec((tm, d), lambda i: (i, 0)),
                  pl.BlockSpec((1, 6, d), lambda i: (i // tpb, 0, 0)),
                  full(wa), full(ws), full(wo)],
        out_specs=pl.BlockSpec((tm, d), lambda i: (i, 0)),
        compiler_params=_cparams(("parallel",)),
        name="outproj",
    )(oT, ossm, zxm, zxm, x2d, mod, wa, ws, wo)


def _ffn_kernel(x_ref, mod_ref, n2_ref, nf_ref, wg_ref, wu_ref, wd_ref, o_ref, *, final_norm):
    x = x_ref[...]
    mod = mod_ref[0]
    h = _norm_mod(x, n2_ref[...], mod[4:5], mod[3:4]).astype(BF16)
    a = _silu(_dot(h, wg_ref[...])) * _dot(h, wu_ref[...])
    r = _dot(a.astype(BF16), wd_ref[...])
    x2 = x + mod[5:6] * r
    if final_norm:
        ms = jnp.mean(x2 * x2, axis=-1, keepdims=True)
        x2 = x2 * lax.rsqrt(ms + EPS) * nf_ref[...]
    o_ref[...] = x2


def _ffn(x1, mod, n2, nf, wg, wu, wd, final_norm, bsz, seq):
    d = x1.shape[1]
    tm = 512
    tpb = seq // tm
    const = lambda a: pl.BlockSpec(a.shape, lambda i: (0,) * a.ndim, pipeline_mode=pl.Buffered(1))
    return pl.pallas_call(
        functools.partial(_ffn_kernel, final_norm=final_norm),
        out_shape=jax.ShapeDtypeStruct((bsz * seq, d), F32),
        grid=(bsz * tpb,),
        in_specs=[pl.BlockSpec((tm, d), lambda i: (i, 0)),
                  pl.BlockSpec((1, 6, d), lambda i: (i // tpb, 0, 0)),
                  const(n2), const(nf), const(wg), const(wu), const(wd)],
        out_specs=pl.BlockSpec((tm, d), lambda i: (i, 0)),
        compiler_params=_cparams(("parallel",)),
        name="ffn",
    )(x1, mod, n2, nf, wg, wu, wd)


def _rotary_tables(seq):
    inv_freq = jnp.asarray(np.power(ROPE_THETA, -np.arange(0, 2 * ROT_HALF, 2) / (2 * ROT_HALF)).astype(np.float32))
    ang = jnp.arange(seq).astype(F32)[:, None] * inv_freq[None, :]
    cos, sin = jnp.cos(ang), jnp.sin(ang)
    ones = jnp.ones((seq, HD - 2 * ROT_HALF), F32)
    zeros8 = jnp.zeros((seq, ROT_HALF), F32)
    zeros = jnp.zeros((seq, HD - 2 * ROT_HALF), F32)
    cosK = jnp.concatenate([cos, cos, ones], axis=1)
    sinA = jnp.concatenate([-sin, zeros8, zeros], axis=1)
    sinB = jnp.concatenate([zeros8, sin, zeros], axis=1)
    rep = LANES // HD
    tile = lambda t: jnp.concatenate([t] * rep, axis=1)
    return cos.T, sin.T, tile(cosK), tile(sinA), tile(sinB)


def _selection_block_map_T(seq):
    nc = (seq - CMP_BLOCK) // CMP_STRIDE + 1
    nb = seq // SLC_BLOCK
    cs = np.arange(nc) * CMP_STRIDE
    bs = np.arange(nb) * SLC_BLOCK
    ov = np.minimum(cs[:, None] + CMP_BLOCK, bs[None, :] + SLC_BLOCK) - np.maximum(cs[:, None], bs[None, :])
    bmap = (np.clip(ov, 0, None) / CMP_BLOCK).astype(np.float32)
    out = np.zeros((nb, seq // CMP_STRIDE), np.float32)
    out[:, :nc] = bmap.T
    return jnp.asarray(out)


def kernel(x, c, w_ada, b_ada, norm1_w, w_in, cmp_pe_k, cmp_w1_k, cmp_w2_k, cmp_pe_v, cmp_w1_v, cmp_w2_v,
           conv_w, conv_b, dt_bias, a_log, d_skip, ssm_norm_w, w_attn_out, w_ssm_out, w_o, norm2_w,
           w_gate, w_up, w_down, norm_f_w):
    bsz, seq, d = x.shape
    depth = w_ada.shape[0]
    q_w = N_HEADS * HD
    kv_w = N_GROUPS * HD
    xbc_w = SSM_INNER + 2 * SSM_GROUPS * SSM_STATE
    offs = np.cumsum([q_w, 6 * kv_w, 3 * N_HEADS, SSM_INNER, xbc_w, SSM_HEADS])
    tabs = _rotary_tables(seq)
    bmapT = _selection_block_map_T(seq)
    x2d = x.reshape(bsz * seq, d)
    for l in range(depth):
        mod = _ada_mod(c, w_ada[l], b_ada[l]).reshape(bsz, 6, d)
        wi = w_in[l]
        wq = wi[:, :offs[0]]
        wkv = wi[:, offs[0]:offs[1]].reshape(d, 6, N_GROUPS * HD)
        wg = wi[:, offs[1]:offs[2]].reshape(d, N_GROUPS, HPG, 3)
        wz = wi[:, offs[2]:offs[3]]
        wxbc = wi[:, offs[3]:offs[4]]
        wdt = wi[:, offs[4]:offs[5]]
        wmg = wi[:, offs[5]:]
        wqT = wq.T.astype(BF16)
        zcol = jnp.zeros((d, HD), wi.dtype)
        wkn = jnp.concatenate([wkv[:, 2, :HD], zcol, wkv[:, 2, HD:], zcol, wkv[:, 0], wkv[:, 4], wkv[:, 1]],
                              axis=1).astype(BF16)
        wvT = jnp.concatenate([wkv[:, 3], wkv[:, 5]], axis=1).T.astype(BF16)
        wgT = jnp.transpose(wg, (1, 3, 2, 0)).reshape(3 * N_HEADS, d).astype(BF16)
        wdtp = jnp.pad(wdt, ((0, 0), (0, LANES - SSM_HEADS))).astype(BF16)
        wbig = jnp.concatenate([wz, wmg, wxbc], axis=1).astype(BF16)
        nw1 = norm1_w[l].reshape(1, d)

        qT, ksa, kn, kcv, vT, gT, dt = _inproj_attn(x2d, mod, nw1, wqT, wkn, wvT, wgT, wdtp, tabs, bsz, seq)
        zxm = _inproj_big(x2d, mod, nw1, wbig, bsz, seq)

        kc = _compress(kcv, 0, cmp_pe_k[l], cmp_w1_k[l], cmp_w2_k[l], False, bsz, seq)
        vcT = _compress(kcv, N_GROUPS, cmp_pe_v[l], cmp_w1_v[l], cmp_w2_v[l], True, bsz, seq)
        oT = _nsa(qT, kc, vcT, ksa, kn, vT, gT, bmapT.astype(BF16), bsz, seq)

        cw = conv_w[l]
        cb = conv_b[l].reshape(1, xbc_w)
        pad_h = lambda v: jnp.pad(v.reshape(1, SSM_HEADS), ((0, 0), (0, LANES - SSM_HEADS)))
        dsk = jnp.repeat(d_skip[l], SSM_HD).reshape(1, SSM_INNER)
        ossm = _ssd(zxm, dt, cw[:, :SSM_INNER], cw[:, SSM_INNER:], cb[:, :SSM_INNER], cb[:, SSM_INNER:],
                    pad_h(dt_bias[l]), pad_h(a_log[l]), dsk, ssm_norm_w[l].reshape(1, SSM_INNER), bsz, seq)

        x1 = _outproj(oT, ossm, zxm, x2d, mod, w_attn_out[l].astype(BF16), w_ssm_out[l].astype(BF16),
                      w_o[l].astype(BF16), bsz, seq)
        x2d = _ffn(x1, mod, norm2_w[l].reshape(1, d), norm_f_w.reshape(1, d), w_gate[l].astype(BF16),
                   w_up[l].astype(BF16), w_down[l].astype(BF16), l == depth - 1, bsz, seq)
    return x2d.reshape(bsz, seq, d)
```

```python
import functools

import numpy as np
import jax
import jax.numpy as jnp
from jax import lax
from jax.experimental import pallas as pl
from jax.experimental.pallas import tpu as pltpu

F32 = jnp.float32
BF16 = jnp.bfloat16
HIGHEST = lax.Precision.HIGHEST

EPS = 1e-6
NEG = -1e30

N_HEADS = 16
N_GROUPS = 2
HPG = N_HEADS // N_GROUPS
HD = 64
ROT_HALF = 8
ROPE_THETA = 500000.0
CMP_BLOCK = 32
CMP_STRIDE = 16
SLC_BLOCK = 64
SLC_TOPK = 16
WINDOW = 512

SSM_HEADS = 32
SSM_HD = 64
SSM_GROUPS = 4
SSM_STATE = 128
SSM_CONV = 4
SSM_CHUNK = 128
SSM_INNER = SSM_HEADS * SSM_HD
SSM_PAD = 16

TQ = 256
NSA_TILES_PER_STEP = 2
SSD_CHUNKS_PER_STEP = 4
KS = 512
LANES = 128
VROWS = HD + 16
LOG2E = float(np.log2(np.e))
VMEM_LIMIT = 56 * 1024 * 1024


def _cparams(sem, vmem=VMEM_LIMIT):
    return pltpu.CompilerParams(dimension_semantics=sem, vmem_limit_bytes=vmem)


def _sigmoid(x):
    return 0.5 + 0.5 * jnp.tanh(0.5 * x)


def _silu(x):
    hx = 0.5 * x
    return hx + hx * jnp.tanh(hx)


def _norm_mod(x, w, sc, sh):
    ms = jnp.mean(x * x, axis=-1, keepdims=True)
    return (x * lax.rsqrt(ms + EPS) * w) * (1.0 + sc) + sh


def _dot(a, b):
    return jnp.dot(a, b, preferred_element_type=F32)


def _dot_split3(a, x):
    hi = x.astype(BF16)
    r1 = x - hi.astype(F32)
    mid = r1.astype(BF16)
    lo = (r1 - mid.astype(F32)).astype(BF16)
    return _dot(a, hi) + _dot(a, mid) + _dot(a, lo)


def _dot_nt(a, b):
    return lax.dot_general(a, b, (((1,), (1,)), ((), ())), preferred_element_type=F32)


def _dot_tn(a, b):
    return lax.dot_general(a, b, (((0,), (0,)), ((), ())), preferred_element_type=F32)


def _ada_kernel(c_ref, w_ref, b_ref, o_ref):
    c = c_ref[...]
    o_ref[...] = jnp.dot(_silu(c), w_ref[...], precision=HIGHEST, preferred_element_type=F32) + b_ref[...]


def _ada_mod(c, w, b):
    bsz, d = c.shape
    n = w.shape[1]
    tn = 1024
    return pl.pallas_call(
        _ada_kernel,
        out_shape=jax.ShapeDtypeStruct((bsz, n), F32),
        grid=(n // tn,),
        in_specs=[pl.BlockSpec((bsz, d), lambda j: (0, 0)),
                  pl.BlockSpec((d, tn), lambda j: (0, j)),
                  pl.BlockSpec((1, tn), lambda j: (0, j))],
        out_specs=pl.BlockSpec((bsz, tn), lambda j: (0, j)),
        compiler_params=_cparams(("parallel",)),
        name="ada_mod",
    )(c, w, b.reshape(1, n))


def _inproj_attn_kernel(x_ref, mod_ref, nw_ref, wqT_ref, wkn_ref, wvT_ref, wgT_ref, wdt_ref,
                        cosT_ref, sinT_ref, cosK_ref, sinA_ref, sinB_ref, perm_ref,
                        qT_ref, ksa_ref, kn_ref, kcv_ref, vT_ref, gT_ref, dt_ref):
    tm = x_ref.shape[0]
    mod = mod_ref[0]
    h = _norm_mod(x_ref[...], nw_ref[...], mod[1:2], mod[0:1]).astype(BF16)

    qT = _dot_nt(wqT_ref[...], h)
    cT = cosT_ref[...]
    sT = sinT_ref[...]
    scale = HD ** -0.5 * LOG2E
    for hh in range(N_HEADS):
        r0 = hh * HD
        x1 = qT[r0:r0 + ROT_HALF]
        x2 = qT[r0 + ROT_HALF:r0 + 2 * ROT_HALF]
        blk = jnp.concatenate([x1 * cT - x2 * sT, x2 * cT + x1 * sT, qT[r0 + 2 * ROT_HALF:r0 + HD]], axis=0)
        blk = (blk * scale).astype(BF16)
        for j in range(tm // TQ):
            qT_ref[0, j, r0:r0 + HD, :] = blk[:, j * TQ:(j + 1) * TQ]

    kn = _dot(h, wkn_ref[...])
    nrot = 8 * HD
    kk = kn[:, :nrot]
    reps = nrot // LANES
    cK = jnp.concatenate([cosK_ref[...]] * reps, axis=1)
    sA = jnp.concatenate([sinA_ref[...]] * reps, axis=1)
    sB = jnp.concatenate([sinB_ref[...]] * reps, axis=1)
    kk = kk * cK + pltpu.roll(kk, nrot - ROT_HALF, 1) * sA + pltpu.roll(kk, ROT_HALF, 1) * sB
    row = lax.broadcasted_iota(jnp.int32, (tm, LANES), 0)
    lane = lax.broadcasted_iota(jnp.int32, (tm, LANES), 1)
    bps = KS // SLC_BLOCK
    blk = jnp.right_shift(row, int(np.log2(SLC_BLOCK)))
    onehot = jnp.where(lane == HD + (blk & (bps - 1)), 1.0, 0.0)
    for g in range(N_GROUPS):
        ksa_ref[0, g] = (kk[:, g * LANES:(g + 1) * LANES] + onehot).astype(BF16)
    for i in range(N_GROUPS):
        kn_ref[0, i] = kk[:, (6 + i) * HD:(7 + i) * HD].astype(BF16)

    nr = tm // CMP_STRIDE
    lo = lax.broadcasted_iota(jnp.int32, (nr, LANES), 1) < HD
    for i, pk in enumerate((kk[:, 4 * HD:6 * HD], kn[:, nrot:nrot + 2 * HD])):
        pm = _dot(perm_ref[...], pk.astype(BF16))
        for j in range(CMP_STRIDE // 2):
            a = pm[2 * j * nr:(2 * j + 1) * nr]
            b = pm[(2 * j + 1) * nr:(2 * j + 2) * nr]
            kcv_ref[0, 2 * i, :, j * LANES:(j + 1) * LANES] = jnp.where(lo, a, pltpu.roll(b, HD, 1)).astype(BF16)
            kcv_ref[0, 2 * i + 1, :, j * LANES:(j + 1) * LANES] = jnp.where(lo, pltpu.roll(a, HD, 1), b).astype(BF16)

    vT = _dot_nt(wvT_ref[...], h).astype(BF16)
    ones = jnp.ones((VROWS - HD, LANES), BF16)
    for j in range(tm // LANES):
        for r in range(4):
            vT_ref[0, j, r * VROWS:r * VROWS + HD, :] = vT[r * HD:(r + 1) * HD, j * LANES:(j + 1) * LANES]
            vT_ref[0, j, r * VROWS + HD:(r + 1) * VROWS, :] = ones

    gates = _sigmoid(_dot_nt(wgT_ref[...], h))
    for j in range(tm // TQ):
        for g in range(N_GROUPS):
            gT_ref[0, j, g] = gates[g * 3 * HPG:(g + 1) * 3 * HPG, j * TQ:(j + 1) * TQ]
    dt_ref[...] = _dot(h, wdt_ref[...])


def _inproj_attn(x2d, mod, nw, wqT, wkn, wvT, wgT, wdt, tabs, bsz, seq):
    d = x2d.shape[1]
    tm = KS
    tpb = seq // tm
    cosT, sinT, cosK, sinA, sinB = tabs
    full = lambda a: pl.BlockSpec(a.shape, lambda i: (0,) * a.ndim)
    orow = np.arange(tm)
    src = CMP_STRIDE * (orow % (tm // CMP_STRIDE)) + orow // (tm // CMP_STRIDE)
    perm = jnp.asarray(np.arange(tm)[None, :] == src[:, None], BF16)
    out_shape = (
        jax.ShapeDtypeStruct((bsz, seq // TQ, N_HEADS * HD, TQ), BF16),
        jax.ShapeDtypeStruct((bsz, N_GROUPS, seq, LANES), BF16),
        jax.ShapeDtypeStruct((bsz, N_GROUPS, seq, HD), BF16),
        jax.ShapeDtypeStruct((bsz, 2 * N_GROUPS, seq // CMP_STRIDE, CMP_STRIDE * HD), BF16),
        jax.ShapeDtypeStruct((bsz, seq // LANES, 4 * VROWS, LANES), BF16),
        jax.ShapeDtypeStruct((bsz, seq // TQ, N_GROUPS, 3 * HPG, TQ), F32),
        jax.ShapeDtypeStruct((bsz * seq, LANES), F32),
    )
    return pl.pallas_call(
        _inproj_attn_kernel,
        out_shape=out_shape,
        grid=(bsz * tpb,),
        in_specs=[pl.BlockSpec((tm, d), lambda i: (i, 0)),
                  pl.BlockSpec((1, 6, d), lambda i: (i // tpb, 0, 0)),
                  full(nw), full(wqT), full(wkn), full(wvT), full(wgT), full(wdt),
                  pl.BlockSpec((ROT_HALF, tm), lambda i: (0, i % tpb)),
                  pl.BlockSpec((ROT_HALF, tm), lambda i: (0, i % tpb)),
                  pl.BlockSpec((tm, LANES), lambda i: (i % tpb, 0)),
                  pl.BlockSpec((tm, LANES), lambda i: (i % tpb, 0)),
                  pl.BlockSpec((tm, LANES), lambda i: (i % tpb, 0)),
                  full(perm)],
        out_specs=(pl.BlockSpec((1, tm // TQ, N_HEADS * HD, TQ), lambda i: (i // tpb, i % tpb, 0, 0)),
                   pl.BlockSpec((1, N_GROUPS, tm, LANES), lambda i: (i // tpb, 0, i % tpb, 0)),
                   pl.BlockSpec((1, N_GROUPS, tm, HD), lambda i: (i // tpb, 0, i % tpb, 0)),
                   pl.BlockSpec((1, 2 * N_GROUPS, tm // CMP_STRIDE, CMP_STRIDE * HD),
                                lambda i: (i // tpb, 0, i % tpb, 0)),
                   pl.BlockSpec((1, tm // LANES, 4 * VROWS, LANES), lambda i: (i // tpb, i % tpb, 0, 0)),
                   pl.BlockSpec((1, tm // TQ, N_GROUPS, 3 * HPG, TQ), lambda i: (i // tpb, i % tpb, 0, 0, 0)),
                   pl.BlockSpec((tm, LANES), lambda i: (i, 0))),
        compiler_params=_cparams(("parallel",)),
        name="inproj_attn",
    )(x2d, mod, nw, wqT, wkn, wvT, wgT, wdt, cosT, sinT, cosK, sinA, sinB, perm)


def _inproj_big_kernel(x_ref, mod_ref, nw_ref, w_ref, o_ref, h_scr, *, gate_lo, gate_hi):
    j = pl.program_id(1)

    @pl.when(j == 0)
    def _():
        mod = mod_ref[0]
        h_scr[...] = _norm_mod(x_ref[...], nw_ref[...], mod[1:2], mod[0:1]).astype(BF16)

    acc = _dot(h_scr[...], w_ref[...])
    tn = acc.shape[1]
    col = j * tn + lax.broadcasted_iota(jnp.int32, (1, tn), 1)
    is_gate = (col >= gate_lo) & (col < gate_hi)
    o_ref[...] = jnp.where(is_gate, _sigmoid(acc), acc).astype(BF16)


def _inproj_big(x2d, mod, nw, wbig, bsz, seq):
    d = x2d.shape[1]
    n = wbig.shape[1]
    tm, tn = 1024, n // 4
    assert n % 4 == 0 and tn % LANES == 0
    tpb = seq // tm
    return pl.pallas_call(
        functools.partial(_inproj_big_kernel, gate_lo=SSM_INNER, gate_hi=SSM_INNER + 2 * d),
        out_shape=jax.ShapeDtypeStruct((bsz * seq, n), BF16),
        grid=(bsz * tpb, n // tn),
        in_specs=[pl.BlockSpec((tm, d), lambda i, j: (i, 0)),
                  pl.BlockSpec((1, 6, d), lambda i, j: (i // tpb, 0, 0)),
                  pl.BlockSpec((1, d), lambda i, j: (0, 0)),
                  pl.BlockSpec((d, tn), lambda i, j: (0, j))],
        out_specs=pl.BlockSpec((tm, tn), lambda i, j: (i, j)),
        scratch_shapes=[pltpu.VMEM((tm, d), BF16)],
        compiler_params=_cparams(("parallel", "arbitrary")),
        name="inproj_big",
    )(x2d, mod, nw, wbig)


def _compress_kernel(x_ref, pea_ref, peb_ref, w1a_ref, w1b_ref, w2_ref, o_ref, *, transpose_out, nc):
    x = x_ref[0, 0].astype(F32)
    nrow = x.shape[0]
    ya = _dot((x + pea_ref[...]).astype(BF16), w1a_ref[...])
    yb = _dot((x + peb_ref[...]).astype(BF16), w1b_ref[...])
    hp = ya + pltpu.roll(yb, nrow - 1, 0)
    row = lax.broadcasted_iota(jnp.int32, hp.shape, 0)
    hh = jnp.where(row < nc, _silu(hp), 0.0).astype(BF16)
    if transpose_out:
        o_ref[0, 0] = _dot_nt(w2_ref[...], hh).astype(BF16)
    else:
        o_ref[0, 0] = _dot(hh, w2_ref[...]).astype(BF16)


def _compress(kn16, slot0, pe, w1, w2, transpose_out, bsz, seq):
    nrow = seq // CMP_STRIDE
    nc = (seq - CMP_BLOCK) // CMP_STRIDE + 1
    half = CMP_STRIDE * HD
    pea = pe[:CMP_STRIDE].reshape(1, half)
    peb = pe[CMP_STRIDE:].reshape(1, half)
    w1a = w1[:half].astype(BF16)
    w1b = w1[half:].astype(BF16)
    if transpose_out:
        w2p = w2.T.astype(BF16)
        oshape, oblock = (bsz, N_GROUPS, HD, nrow), (1, 1, HD, nrow)
    else:
        w2p = w2.astype(BF16)
        oshape, oblock = (bsz, N_GROUPS, nrow, HD), (1, 1, nrow, HD)
    full = lambda a: pl.BlockSpec(a.shape, lambda b, g: (0,) * a.ndim)
    return pl.pallas_call(
        functools.partial(_compress_kernel, transpose_out=transpose_out, nc=nc),
        out_shape=jax.ShapeDtypeStruct(oshape, BF16),
        grid=(bsz, N_GROUPS),
        in_specs=[pl.BlockSpec((1, 1, nrow, half), lambda b, g: (b, slot0 + g, 0, 0)),
                  full(pea), full(peb), full(w1a), full(w1b), full(w2p)],
        out_specs=pl.BlockSpec(oblock, lambda b, g: (b, g, 0, 0)),
        compiler_params=_cparams(("parallel", "parallel")),
        name="compress_v" if transpose_out else "compress_k",
    )(kn16, pea, peb, w1a, w1b, w2p)


def _nsa_kernel(*refs, nb, tiles):
    def tile(u, carry):
        _nsa_tile(u, pl.program_id(2) * tiles + u, *refs, nb=nb)
        return carry

    lax.fori_loop(0, tiles, tile, 0)


def _nsa_tile(u, qi, q_ref, kc_ref, vcT_ref, ksa_ref, kw_ref, vsT_ref, vwT_ref, g_ref, bmapT_ref, o_ref,
              v_scr, selb_scr, m_scr, acc_scr, sa_scr, sb_scr, ma_scr, mb_scr, sw_scr, mw_scr, *, nb):
    t0 = qi * TQ
    wl = HPG * TQ
    q = q_ref[0, u]
    Q = jnp.concatenate([q[hh * HD:(hh + 1) * HD, :] for hh in range(HPG)], axis=1)
    tok1 = t0 + lax.broadcasted_iota(jnp.int32, (1, TQ), 1)
    tok = jnp.concatenate([tok1] * HPG, axis=1)

    def lanes8(x):
        return jnp.concatenate([x] * HPG, axis=1)

    nwt = WINDOW // TQ + 1
    kt0 = jnp.maximum(qi - WINDOW // TQ, 0)
    kw0 = pl.multiple_of(kt0 * TQ, TQ)
    sw = _dot(kw_ref[0, 0, pl.ds(kw0, nwt * TQ), :], Q)
    diff = tok1 - (kw0 + lax.broadcasted_iota(jnp.int32, (nwt * TQ, 1), 0))
    sw = sw + lanes8(jnp.where((diff >= 0) & (diff < WINDOW), 0.0, NEG))
    sw_scr[...] = sw
    mw_scr[...] = jnp.max(sw, axis=0, keepdims=True)

    ncp = kc_ref.shape[2]
    sc = _dot(kc_ref[0, 0], Q)
    n_idx = lax.broadcasted_iota(jnp.int32, (ncp, 1), 0)
    mask_c = (n_idx * CMP_STRIDE + (CMP_BLOCK - 1)) <= tok
    sc = jnp.where(mask_c, sc, NEG)
    mc = jnp.max(sc, axis=0, keepdims=True)
    ec = jnp.exp2(sc - mc)
    has_blk = tok >= CMP_BLOCK - 1
    pc = ec * jnp.where(has_blk, 1.0 / jnp.sum(ec, axis=0, keepdims=True), 0.0)
    o_c = _dot(vcT_ref[0, 0], pc.astype(BF16))

    psum = pc[:, 0:TQ]
    for hh in range(1, HPG):
        psum = psum + pc[:, hh * TQ:(hh + 1) * TQ]
    imp = _dot_split3(bmapT_ref[...], psum)
    jb = lax.broadcasted_iota(jnp.int32, (nb, 1), 0)
    cur = jnp.right_shift(tok1, int(np.log2(SLC_BLOCK)))
    forced = (jb == 0) | (jb == cur) | (jb == cur - 1)
    v = jnp.where(forced, jnp.inf, jnp.where(jb <= cur, imp, -jnp.inf))
    v_scr[...] = v
    jbf = lax.broadcasted_iota(jnp.int32, (nb, TQ), 0)

    def rank_body(i2, rank):
        for u in range(2):
            i = 2 * i2 + u
            vi = v_scr[pl.ds(i, 1), :]
            ge = jnp.where(vi >= v, 1.0, 0.0)
            gt = jnp.where(vi > v, 1.0, 0.0)
            rank = rank + jnp.where(jbf > i, ge, gt)
        return rank

    n_live = (TQ // SLC_BLOCK) * (qi + 1)
    rank_trips = jnp.where(n_live <= SLC_TOPK, 0, jnp.minimum(n_live, nb) // 2)
    rank = lax.fori_loop(0, rank_trips, rank_body, jnp.zeros((nb, TQ), F32))
    selb_scr[...] = jnp.where(rank < min(SLC_TOPK, nb), 0.0, NEG)

    tps = KS // TQ
    bps = KS // SLC_BLOCK
    m_scr[...] = jnp.full((1, wl), NEG, F32)
    acc_scr[...] = jnp.zeros((VROWS, wl), F32)
    st_hi = qi // tps

    def scores(st, s_scr, mx_scr, bias=None):
        sb = selb_scr[pl.ds(pl.multiple_of(st * bps, bps), bps), :]
        ext = lanes8(jnp.concatenate([sb, jnp.zeros((16 - bps, TQ), F32)], axis=0)).astype(BF16)
        qa = jnp.concatenate([Q, ext, jnp.zeros((LANES - HD - 16, wl), BF16)], axis=0)
        s = _dot(ksa_ref[0, 0, pl.ds(pl.multiple_of(st * KS, KS), KS), :], qa)
        if bias is not None:
            s = s + bias
        s_scr[...] = s
        mx_scr[...] = jnp.max(s, axis=0, keepdims=True)

    def update(s_scr, mx_scr, st):
        m_old = m_scr[...]
        m_new = jnp.maximum(m_old, mx_scr[...])
        p = jnp.exp2(s_scr[...] - m_new).astype(BF16)
        cps = KS // LANES
        vs = jnp.concatenate([vsT_ref[0, st * cps + j] for j in range(cps)], axis=1)
        acc_scr[...] = acc_scr[...] * jnp.exp2(m_old - m_new) + _dot(vs, p)
        m_scr[...] = m_new

    kpos = st_hi * KS + lax.broadcasted_iota(jnp.int32, (KS, 1), 0)
    scores(st_hi, sa_scr, ma_scr, lanes8(jnp.where(kpos <= tok1, 0.0, NEG)))

    pw = jnp.exp2(sw_scr[...] - mw_scr[...]).astype(BF16)
    cw0 = kt0 * (TQ // LANES)
    vwin = jnp.concatenate([vwT_ref[0, cw0 + j] for j in range(nwt * TQ // LANES)], axis=1)
    ow = _dot(vwin, pw)
    o_w = ow[:HD] / ow[HD:HD + 1]

    def pair_body(j, carry):
        ta = st_hi - 2 * j
        scores(ta - 1, sb_scr, mb_scr)
        update(sa_scr, ma_scr, ta)
        scores(ta - 2, sa_scr, ma_scr)
        update(sb_scr, mb_scr, ta - 1)
        return carry

    n_pairs = st_hi // 2
    lax.fori_loop(0, n_pairs, pair_body, 0)

    @pl.when(st_hi - 2 * n_pairs == 1)
    def _():
        scores(0, sb_scr, mb_scr)
        update(sa_scr, ma_scr, 1)
        update(sb_scr, mb_scr, 0)

    @pl.when(st_hi - 2 * n_pairs == 0)
    def _():
        update(sa_scr, ma_scr, 0)

    o_s = acc_scr[0:HD, :] / acc_scr[HD:HD + 1, :]

    gts = g_ref[0, u, 0]
    for hh in range(HPG):
        sl = slice(hh * TQ, (hh + 1) * TQ)
        o = (gts[hh:hh + 1] * o_c[:, sl] + gts[HPG + hh:HPG + hh + 1] * o_s[:, sl]
             + gts[2 * HPG + hh:2 * HPG + hh + 1] * o_w[:, sl])
        o_ref[0, u, hh * HD:(hh + 1) * HD, :] = o.astype(BF16)


def _nsa(qT, kc, vcT, ksa, kn, vT, gT, bmapT, bsz, seq):
    nq = seq // TQ
    nb = seq // SLC_BLOCK
    ncp = kc.shape[2]
    wl = HPG * TQ
    tiles = NSA_TILES_PER_STEP
    assert seq % KS == 0 and seq >= WINDOW + TQ and KS // SLC_BLOCK == 8 and nq % tiles == 0
    return pl.pallas_call(
        functools.partial(_nsa_kernel, nb=nb, tiles=tiles),
        out_shape=jax.ShapeDtypeStruct((bsz, nq, N_HEADS * HD, TQ), BF16),
        grid=(bsz, N_GROUPS, nq // tiles),
        in_specs=[pl.BlockSpec((1, tiles, HPG * HD, TQ), lambda b, g, i: (b, i, g, 0)),
                  pl.BlockSpec((1, 1, ncp, HD), lambda b, g, i: (b, g, 0, 0)),
                  pl.BlockSpec((1, 1, HD, ncp), lambda b, g, i: (b, g, 0, 0)),
                  pl.BlockSpec((1, 1, seq, LANES), lambda b, g, i: (b, g, 0, 0)),
                  pl.BlockSpec((1, 1, seq, HD), lambda b, g, i: (b, g, 0, 0)),
                  pl.BlockSpec((1, seq // LANES, VROWS, LANES), lambda b, g, i: (b, 0, g, 0)),
                  pl.BlockSpec((1, seq // LANES, VROWS, LANES), lambda b, g, i: (b, 0, 2 + g, 0)),
                  pl.BlockSpec((1, tiles, 1, 3 * HPG, TQ), lambda b, g, i: (b, i, g, 0, 0)),
                  pl.BlockSpec(bmapT.shape, lambda b, g, i: (0, 0))],
        out_specs=pl.BlockSpec((1, tiles, HPG * HD, TQ), lambda b, g, i: (b, i, g, 0)),
        scratch_shapes=[pltpu.VMEM((nb, TQ), F32),
                        pltpu.VMEM((nb, TQ), F32),
                        pltpu.VMEM((1, wl), F32),
                        pltpu.VMEM((VROWS, wl), F32),
                        pltpu.VMEM((KS, wl), F32),
                        pltpu.VMEM((KS, wl), F32),
                        pltpu.VMEM((1, wl), F32),
                        pltpu.VMEM((1, wl), F32),
                        pltpu.VMEM((WINDOW + TQ, wl), F32),
                        pltpu.VMEM((1, wl), F32)],
        compiler_params=_cparams(("parallel", "parallel", "arbitrary")),
        name="nsa_attention",
    )(qT, kc, vcT, ksa, kn, vT, vT, gT, bmapT)


def _ssd_kernel(*refs, chunks):
    def chunk(u, carry):
        _ssd_chunk(u, *refs)
        return carry

    lax.fori_loop(0, chunks, chunk, 0)


def _ssd_chunk(u, z_ref, xs_ref, bc_ref, dt_ref, cwx_ref, cwb_ref, cbx_ref, cbb_ref, dtb_ref, alog_ref,
               dsk_ref, nw_ref, o_ref, xpx_scr, xpb_scr, st_scr):
    L = SSM_CHUNK
    rows = pl.ds(pl.multiple_of(u * L, L), L)
    pad = SSM_PAD
    npair = SSM_HEADS // 2
    ppg = npair // SSM_GROUPS

    @pl.when((pl.program_id(1) == 0) & (u == 0))
    def _():
        xpx_scr[0:pad, :] = jnp.zeros((pad, xpx_scr.shape[1]), BF16)
        xpb_scr[0:pad, :] = jnp.zeros((pad, xpb_scr.shape[1]), BF16)
        st_scr[...] = jnp.zeros(st_scr.shape, F32)

    nshift = SSM_CONV - 1
    srow = lax.broadcasted_iota(jnp.int32, (nshift * L, pad + L), 0)
    scol = lax.broadcasted_iota(jnp.int32, (nshift * L, pad + L), 1)
    shift = jnp.where(scol == (srow & (L - 1)) + jnp.right_shift(srow, int(np.log2(L))) + (pad - nshift),
                      1.0, 0.0).astype(BF16)

    def conv_silu(x_ref, w_ref, b_ref, scr):
        x = x_ref[rows, :]
        scr[pad:pad + L, :] = x
        sh = _dot(shift, scr[...])
        acc = b_ref[...] + w_ref[nshift:nshift + 1, :] * x.astype(F32)
        for k in range(nshift):
            acc = acc + w_ref[k:k + 1, :] * sh[k * L:(k + 1) * L]
        scr[0:pad, :] = scr[L:L + pad, :]
        return _silu(acc)

    xs = conv_silu(xs_ref, cwx_ref, cbx_ref, xpx_scr)
    bcm = conv_silu(bc_ref, cwb_ref, cbb_ref, xpb_scr).astype(BF16)
    xs_b = xs.astype(BF16)

    dtv = dt_ref[rows, :] + dtb_ref[...]
    dt = jnp.maximum(dtv, 0.0) + jnp.log(1.0 + jnp.exp(-jnp.abs(dtv)))
    a = -jnp.exp(alog_ref[...]) * LOG2E
    row = lax.broadcasted_iota(jnp.int32, (L, L), 0)
    col = lax.broadcasted_iota(jnp.int32, (L, L), 1)
    causal = col <= row
    a_cum = _dot_split3(jnp.where(causal, 1.0, 0.0).astype(BF16), dt * a)
    a_cumT = a_cum.T
    dtT = dt.T
    a_last = a_cum[L - 1:L, :]
    a_lastT = a_cumT[:, L - 1:L]
    wT = dtT * jnp.exp2(a_lastT - a_cumT)
    e_last = jnp.exp2(a_last)
    lo = lax.broadcasted_iota(jnp.int32, (1, LANES), 1) < SSM_HD

    ys = []
    for g in range(SSM_GROUPS):
        Bg = bcm[:, g * SSM_STATE:(g + 1) * SSM_STATE]
        Cg = bcm[:, (SSM_GROUPS + g) * SSM_STATE:(SSM_GROUPS + g + 1) * SSM_STATE]
        cb = _dot_nt(Cg, Bg)
        BgT = Bg.astype(F32).T
        for jp in range(ppg):
            pj = g * ppg + jp
            h0, h1 = 2 * pj, 2 * pj + 1
            xp = xs_b[:, pj * LANES:(pj + 1) * LANES]

            def head_mats(h):
                ecol = jnp.broadcast_to(a_cum[:, h:h + 1], (L, L))
                dec = jnp.exp2(jnp.where(causal, ecol - a_cumT[h:h + 1, :], NEG))
                gm = (cb * dec * dtT[h:h + 1, :]).astype(BF16)
                bw = (BgT * wT[h:h + 1, :]).astype(BF16)
                return gm, bw, jnp.exp2(ecol)

            g0, b0, e0 = head_mats(h0)
            g1, b1, e1 = head_mats(h1)
            y_d = jnp.where(lo, _dot(g0, xp), _dot(g1, xp))
            st_prev = st_scr[pj]
            y_o = _dot(Cg, st_prev.astype(BF16)) * jnp.where(lo, e0, e1)
            st_new = jnp.where(lo, _dot(b0, xp), _dot(b1, xp))
            cdec = jnp.where(lo, jnp.broadcast_to(e_last[:, h0:h0 + 1], (1, LANES)),
                             jnp.broadcast_to(e_last[:, h1:h1 + 1], (1, LANES)))
            st_scr[pj] = st_prev * cdec + st_new
            sl = slice(pj * LANES, (pj + 1) * LANES)
            ys.append(y_d + y_o + dsk_ref[:, sl] * xs[:, sl])
    y = jnp.concatenate(ys, axis=1)
    y = y * _silu(z_ref[rows, :].astype(F32))
    gw = SSM_INNER // SSM_GROUPS
    for g in range(SSM_GROUPS):
        yg = y[:, g * gw:(g + 1) * gw]
        ms = jnp.mean(yg * yg, axis=-1, keepdims=True)
        o_ref[rows, g * gw:(g + 1) * gw] = (yg * lax.rsqrt(ms + EPS) * nw_ref[:, g * gw:(g + 1) * gw]).astype(BF16)


def _ssd(zxm, dt, cwx, cwb, cbx, cbb, dtb, alog, dsk, nw, bsz, seq):
    L = SSM_CHUNK
    chunks = SSD_CHUNKS_PER_STEP
    rows = chunks * L
    nstep = seq // rows
    assert seq % rows == 0
    nbc = 2 * SSM_GROUPS * SSM_STATE
    full = lambda a: pl.BlockSpec(a.shape, lambda b, c: (0,) * a.ndim)
    rowblk = lambda width, idx: pl.BlockSpec((rows, width), lambda b, c: (b * nstep + c, idx))
    return pl.pallas_call(
        functools.partial(_ssd_kernel, chunks=chunks),
        out_shape=jax.ShapeDtypeStruct((bsz * seq, SSM_INNER), BF16),
        grid=(bsz, nstep),
        in_specs=[rowblk(SSM_INNER, 0), rowblk(SSM_INNER, 2), rowblk(nbc, 6), rowblk(LANES, 0),
                  full(cwx), full(cwb), full(cbx), full(cbb), full(dtb), full(alog), full(dsk), full(nw)],
        out_specs=pl.BlockSpec((rows, SSM_INNER), lambda b, c: (b * nstep + c, 0)),
        scratch_shapes=[pltpu.VMEM((L + SSM_PAD, SSM_INNER), BF16),
                        pltpu.VMEM((L + SSM_PAD, nbc), BF16),
                        pltpu.VMEM((SSM_HEADS // 2, SSM_STATE, LANES), F32)],
        compiler_params=_cparams(("parallel", "arbitrary")),
        name="ssd",
    )(zxm, zxm, zxm, dt, cwx, cwb, cbx, cbb, dtb, alog, dsk, nw)


def _outproj_kernel(oT_ref, os_ref, ga_ref, gs_ref, x_ref, mod_ref, wa_ref, ws_ref, wo_ref, o_ref):
    oT = jnp.concatenate([oT_ref[0, j] for j in range(oT_ref.shape[1])], axis=1)
    ya = _dot_tn(oT, wa_ref[...])
    ys = _dot(os_ref[...], ws_ref[...])
    m = ga_ref[...].astype(F32) * ya + gs_ref[...].astype(F32) * ys
    r = _dot(m.astype(BF16), wo_ref[...])
    o_ref[...] = x_ref[...] + mod_ref[0][2:3] * r


def _outproj(oT, ossm, zxm, x2d, mod, wa, ws, wo, bsz, seq):
    d = x2d.shape[1]
    tm = 512
    tpb = seq // tm
    full = lambda a: pl.BlockSpec(a.shape, lambda i: (0,) * a.ndim)
    return pl.pallas_call(
        _outproj_kernel,
        out_shape=jax.ShapeDtypeStruct((bsz * seq, d), F32),
        grid=(bsz * tpb,),
        in_specs=[pl.BlockSpec((1, tm // TQ, N_HEADS * HD, TQ), lambda i: (i // tpb, i % tpb, 0, 0)),
                  pl.BlockSpec((tm, SSM_INNER), lambda i: (i, 0)),
                  pl.BlockSpec((tm, d), lambda i: (i, 2)),
                  pl.BlockSpec((tm, d), lambda i: (i, 3)),
                  pl.BlockSpec((tm, d), lambda i: (i, 0)),
                  pl.BlockSpec((1, 6, d), lambda i: (i // tpb, 0, 0)),
                  full(wa), full(ws), full(wo)],
        out_specs=pl.BlockSpec((tm, d), lambda i: (i, 0)),
        compiler_params=_cparams(("parallel",)),
        name="outproj",
    )(oT, ossm, zxm, zxm, x2d, mod, wa, ws, wo)


def _ffn_kernel(x_ref, mod_ref, n2_ref, nf_ref, wg_ref, wu_ref, wd_ref, o_ref, *, final_norm):
    x = x_ref[...]
    mod = mod_ref[0]
    h = _norm_mod(x, n2_ref[...], mod[4:5], mod[3:4]).astype(BF16)
    a = _silu(_dot(h, wg_ref[...])) * _dot(h, wu_ref[...])
    r = _dot(a.astype(BF16), wd_ref[...])
    x2 = x + mod[5:6] * r
    if final_norm:
        ms = jnp.mean(x2 * x2, axis=-1, keepdims=True)
        x2 = x2 * lax.rsqrt(ms + EPS) * nf_ref[...]
    o_ref[...] = x2


def _ffn(x1, mod, n2, nf, wg, wu, wd, final_norm, bsz, seq):
    d = x1.shape[1]
    tm = 512
    tpb = seq // tm
    const = lambda a: pl.BlockSpec(a.shape, lambda i: (0,) * a.ndim, pipeline_mode=pl.Buffered(1))
    return pl.pallas_call(
        functools.partial(_ffn_kernel, final_norm=final_norm),
        out_shape=jax.ShapeDtypeStruct((bsz * seq, d), F32),
        grid=(bsz * tpb,),
        in_specs=[pl.BlockSpec((tm, d), lambda i: (i, 0)),
                  pl.BlockSpec((1, 6, d), lambda i: (i // tpb, 0, 0)),
                  const(n2), const(nf), const(wg), const(wu), const(wd)],
        out_specs=pl.BlockSpec((tm, d), lambda i: (i, 0)),
        compiler_params=_cparams(("parallel",)),
        name="ffn",
    )(x1, mod, n2, nf, wg, wu, wd)


def _rotary_tables(seq):
    inv_freq = jnp.asarray(np.power(ROPE_THETA, -np.arange(0, 2 * ROT_HALF, 2) / (2 * ROT_HALF)).astype(np.float32))
    ang = jnp.arange(seq).astype(F32)[:, None] * inv_freq[None, :]
    cos, sin = jnp.cos(ang), jnp.sin(ang)
    ones = jnp.ones((seq, HD - 2 * ROT_HALF), F32)
    zeros8 = jnp.zeros((seq, ROT_HALF), F32)
    zeros = jnp.zeros((seq, HD - 2 * ROT_HALF), F32)
    cosK = jnp.concatenate([cos, cos, ones], axis=1)
    sinA = jnp.concatenate([-sin, zeros8, zeros], axis=1)
    sinB = jnp.concatenate([zeros8, sin, zeros], axis=1)
    rep = LANES // HD
    tile = lambda t: jnp.concatenate([t] * rep, axis=1)
    return cos.T, sin.T, tile(cosK), tile(sinA), tile(sinB)


def _selection_block_map_T(seq):
    nc = (seq - CMP_BLOCK) // CMP_STRIDE + 1
    nb = seq // SLC_BLOCK
    cs = np.arange(nc) * CMP_STRIDE
    bs = np.arange(nb) * SLC_BLOCK
    ov = np.minimum(cs[:, None] + CMP_BLOCK, bs[None, :] + SLC_BLOCK) - np.maximum(cs[:, None], bs[None, :])
    bmap = (np.clip(ov, 0, None) / CMP_BLOCK).astype(np.float32)
    out = np.zeros((nb, seq // CMP_STRIDE), np.float32)
    out[:, :nc] = bmap.T
    return jnp.asarray(out)


def kernel(x, c, w_ada, b_ada, norm1_w, w_in, cmp_pe_k, cmp_w1_k, cmp_w2_k, cmp_pe_v, cmp_w1_v, cmp_w2_v,
           conv_w, conv_b, dt_bias, a_log, d_skip, ssm_norm_w, w_attn_out, w_ssm_out, w_o, norm2_w,
           w_gate, w_up, w_down, norm_f_w):
    bsz, seq, d = x.shape
    depth = w_ada.shape[0]
    q_w = N_HEADS * HD
    kv_w = N_GROUPS * HD
    xbc_w = SSM_INNER + 2 * SSM_GROUPS * SSM_STATE
    offs = np.cumsum([q_w, 6 * kv_w, 3 * N_HEADS, SSM_INNER, xbc_w, SSM_HEADS])
    tabs = _rotary_tables(seq)
    bmapT = _selection_block_map_T(seq)
    x2d = x.reshape(bsz * seq, d)
    for l in range(depth):
        mod = _ada_mod(c, w_ada[l], b_ada[l]).reshape(bsz, 6, d)
        wi = w_in[l]
        wq = wi[:, :offs[0]]
        wkv = wi[:, offs[0]:offs[1]].reshape(d, 6, N_GROUPS * HD)
        wg = wi[:, offs[1]:offs[2]].reshape(d, N_GROUPS, HPG, 3)
        wz = wi[:, offs[2]:offs[3]]
        wxbc = wi[:, offs[3]:offs[4]]
        wdt = wi[:, offs[4]:offs[5]]
        wmg = wi[:, offs[5]:]
        wqT = wq.T.astype(BF16)
        zcol = jnp.zeros((d, HD), wi.dtype)
        wkn = jnp.concatenate([wkv[:, 2, :HD], zcol, wkv[:, 2, HD:], zcol, wkv[:, 0], wkv[:, 4], wkv[:, 1]],
                              axis=1).astype(BF16)
        wvT = jnp.concatenate([wkv[:, 3], wkv[:, 5]], axis=1).T.astype(BF16)
        wgT = jnp.transpose(wg, (1, 3, 2, 0)).reshape(3 * N_HEADS, d).astype(BF16)
        wdtp = jnp.pad(wdt, ((0, 0), (0, LANES - SSM_HEADS))).astype(BF16)
        wbig = jnp.concatenate([wz, wmg, wxbc], axis=1).astype(BF16)
        nw1 = norm1_w[l].reshape(1, d)

        qT, ksa, kn, kcv, vT, gT, dt = _inproj_attn(x2d, mod, nw1, wqT, wkn, wvT, wgT, wdtp, tabs, bsz, seq)
        zxm = _inproj_big(x2d, mod, nw1, wbig, bsz, seq)

        kc = _compress(kcv, 0, cmp_pe_k[l], cmp_w1_k[l], cmp_w2_k[l], False, bsz, seq)
        vcT = _compress(kcv, N_GROUPS, cmp_pe_v[l], cmp_w1_v[l], cmp_w2_v[l], True, bsz, seq)
        oT = _nsa(qT, kc, vcT, ksa, kn, vT, gT, bmapT.astype(BF16), bsz, seq)

        cw = conv_w[l]
        cb = conv_b[l].reshape(1, xbc_w)
        pad_h = lambda v: jnp.pad(v.reshape(1, SSM_HEADS), ((0, 0), (0, LANES - SSM_HEADS)))
        dsk = jnp.repeat(d_skip[l], SSM_HD).reshape(1, SSM_INNER)
        ossm = _ssd(zxm, dt, cw[:, :SSM_INNER], cw[:, SSM_INNER:], cb[:, :SSM_INNER], cb[:, SSM_INNER:],
                    pad_h(dt_bias[l]), pad_h(a_log[l]), dsk, ssm_norm_w[l].reshape(1, SSM_INNER), bsz, seq)

        x1 = _outproj(oT, ossm, zxm, x2d, mod, w_attn_out[l].astype(BF16), w_ssm_out[l].astype(BF16),
                      w_o[l].astype(BF16), bsz, seq)
        x2d = _ffn(x1, mod, norm2_w[l].reshape(1, d), norm_f_w.reshape(1, d), w_gate[l].astype(BF16),
                   w_up[l].astype(BF16), w_down[l].astype(BF16), l == depth - 1, bsz, seq)
    return x2d.reshape(bsz, seq, d)
```

```python
import functools

import numpy as np
import jax
import jax.numpy as jnp
from jax import lax
from jax.experimental import pallas as pl
from jax.experimental.pallas import tpu as pltpu

F32 = jnp.float32
BF16 = jnp.bfloat16
HIGHEST = lax.Precision.HIGHEST

EPS = 1e-6
NEG = -1e30

N_HEADS = 16
N_GROUPS = 2
HPG = N_HEADS // N_GROUPS
HD = 64
ROT_HALF = 8
ROPE_THETA = 500000.0
CMP_BLOCK = 32
CMP_STRIDE = 16
SLC_BLOCK = 64
SLC_TOPK = 16
WINDOW = 512

SSM_HEADS = 32
SSM_HD = 64
SSM_GROUPS = 4
SSM_STATE = 128
SSM_CONV = 4
SSM_CHUNK = 128
SSM_INNER = SSM_HEADS * SSM_HD
SSM_PAD = 16

TQ = 256
NSA_TILES_PER_STEP = 2
SSD_CHUNKS_PER_STEP = 4
KS = 512
LANES = 128
VROWS = HD + 16
LOG2E = float(np.log2(np.e))
VMEM_LIMIT = 56 * 1024 * 1024


def _cparams(sem, vmem=VMEM_LIMIT):
    return pltpu.CompilerParams(dimension_semantics=sem, vmem_limit_bytes=vmem)


def _sigmoid(x):
    return 0.5 + 0.5 * jnp.tanh(0.5 * x)


def _silu(x):
    hx = 0.5 * x
    return hx + hx * jnp.tanh(hx)


def _norm_mod(x, w, sc, sh):
    ms = jnp.mean(x * x, axis=-1, keepdims=True)
    return (x * lax.rsqrt(ms + EPS) * w) * (1.0 + sc) + sh


def _dot(a, b):
    return jnp.dot(a, b, preferred_element_type=F32)


def _dot_split3(a, x):
    hi = x.astype(BF16)
    r1 = x - hi.astype(F32)
    mid = r1.astype(BF16)
    lo = (r1 - mid.astype(F32)).astype(BF16)
    return _dot(a, hi) + _dot(a, mid) + _dot(a, lo)


def _dot_nt(a, b):
    return lax.dot_general(a, b, (((1,), (1,)), ((), ())), preferred_element_type=F32)


def _dot_tn(a, b):
    return lax.dot_general(a, b, (((0,), (0,)), ((), ())), preferred_element_type=F32)


def _ada_kernel(c_ref, w_ref, b_ref, o_ref):
    c = c_ref[...]
    o_ref[...] = jnp.dot(_silu(c), w_ref[...], precision=HIGHEST, preferred_element_type=F32) + b_ref[...]


def _ada_mod(c, w, b):
    bsz, d = c.shape
    n = w.shape[1]
    tn = 1024
    return pl.pallas_call(
        _ada_kernel,
        out_shape=jax.ShapeDtypeStruct((bsz, n), F32),
        grid=(n // tn,),
        in_specs=[pl.BlockSpec((bsz, d), lambda j: (0, 0)),
                  pl.BlockSpec((d, tn), lambda j: (0, j)),
                  pl.BlockSpec((1, tn), lambda j: (0, j))],
        out_specs=pl.BlockSpec((bsz, tn), lambda j: (0, j)),
        compiler_params=_cparams(("parallel",)),
        name="ada_mod",
    )(c, w, b.reshape(1, n))


def _inproj_attn_kernel(x_ref, mod_ref, nw_ref, wqT_ref, wkn_ref, wvT_ref, wgT_ref, wdt_ref,
                        cosT_ref, sinT_ref, cosK_ref, sinA_ref, sinB_ref, perm_ref,
                        qT_ref, ksa_ref, kn_ref, kcv_ref, vT_ref, gT_ref, dt_ref):
    tm = x_ref.shape[0]
    mod = mod_ref[0]
    h = _norm_mod(x_ref[...], nw_ref[...], mod[1:2], mod[0:1]).astype(BF16)

    qT = _dot_nt(wqT_ref[...], h)
    cT = cosT_ref[...]
    sT = sinT_ref[...]
    scale = HD ** -0.5 * LOG2E
    for hh in range(N_HEADS):
        r0 = hh * HD
        x1 = qT[r0:r0 + ROT_HALF]
        x2 = qT[r0 + ROT_HALF:r0 + 2 * ROT_HALF]
        blk = jnp.concatenate([x1 * cT - x2 * sT, x2 * cT + x1 * sT, qT[r0 + 2 * ROT_HALF:r0 + HD]], axis=0)
        blk = (blk * scale).astype(BF16)
        for j in range(tm // TQ):
            qT_ref[0, j, r0:r0 + HD, :] = blk[:, j * TQ:(j + 1) * TQ]

    kn = _dot(h, wkn_ref[...])
    nrot = 8 * HD
    kk = kn[:, :nrot]
    reps = nrot // LANES
    cK = jnp.concatenate([cosK_ref[...]] * reps, axis=1)
    sA = jnp.concatenate([sinA_ref[...]] * reps, axis=1)
    sB = jnp.concatenate([sinB_ref[...]] * reps, axis=1)
    kk = kk * cK + pltpu.roll(kk, nrot - ROT_HALF, 1) * sA + pltpu.roll(kk, ROT_HALF, 1) * sB
    row = lax.broadcasted_iota(jnp.int32, (tm, LANES), 0)
    lane = lax.broadcasted_iota(jnp.int32, (tm, LANES), 1)
    bps = KS // SLC_BLOCK
    blk = jnp.right_shift(row, int(np.log2(SLC_BLOCK)))
    onehot = jnp.where(lane == HD + (blk & (bps - 1)), 1.0, 0.0)
    for g in range(N_GROUPS):
        ksa_ref[0, g] = (kk[:, g * LANES:(g + 1) * LANES] + onehot).astype(BF16)
    for i in range(N_GROUPS):
        kn_ref[0, i] = kk[:, (6 + i) * HD:(7 + i) * HD].astype(BF16)

    nr = tm // CMP_STRIDE
    lo = lax.broadcasted_iota(jnp.int32, (nr, LANES), 1) < HD
    for i, pk in enumerate((kk[:, 4 * HD:6 * HD], kn[:, nrot:nrot + 2 * HD])):
        pm = _dot(perm_ref[...], pk.astype(BF16))
        for j in range(CMP_STRIDE // 2):
            a = pm[2 * j * nr:(2 * j + 1) * nr]
            b = pm[(2 * j + 1) * nr:(2 * j + 2) * nr]
            kcv_ref[0, 2 * i, :, j * LANES:(j + 1) * LANES] = jnp.where(lo, a, pltpu.roll(b, HD, 1)).astype(BF16)
            kcv_ref[0, 2 * i + 1, :, j * LANES:(j + 1) * LANES] = jnp.where(lo, pltpu.roll(a, HD, 1), b).astype(BF16)

    vT = _dot_nt(wvT_ref[...], h).astype(BF16)
    ones = jnp.ones((VROWS - HD, LANES), BF16)
    for j in range(tm // LANES):
        for r in range(4):
            vT_ref[0, j, r * VROWS:r * VROWS + HD, :] = vT[r * HD:(r + 1) * HD, j * LANES:(j + 1) * LANES]
            vT_ref[0, j, r * VROWS + HD:(r + 1) * VROWS, :] = ones

    gates = _sigmoid(_dot_nt(wgT_ref[...], h))
    for j in range(tm // TQ):
        for g in range(N_GROUPS):
            gT_ref[0, j, g] = gates[g * 3 * HPG:(g + 1) * 3 * HPG, j * TQ:(j + 1) * TQ]
    dt_ref[...] = _dot(h, wdt_ref[...])


def _inproj_attn(x2d, mod, nw, wqT, wkn, wvT, wgT, wdt, tabs, bsz, seq):
    d = x2d.shape[1]
    tm = KS
    tpb = seq // tm
    cosT, sinT, cosK, sinA, sinB = tabs
    full = lambda a: pl.BlockSpec(a.shape, lambda i: (0,) * a.ndim)
    orow = np.arange(tm)
    src = CMP_STRIDE * (orow % (tm // CMP_STRIDE)) + orow // (tm // CMP_STRIDE)
    perm = jnp.asarray(np.arange(tm)[None, :] == src[:, None], BF16)
    out_shape = (
        jax.ShapeDtypeStruct((bsz, seq // TQ, N_HEADS * HD, TQ), BF16),
        jax.ShapeDtypeStruct((bsz, N_GROUPS, seq, LANES), BF16),
        jax.ShapeDtypeStruct((bsz, N_GROUPS, seq, HD), BF16),
        jax.ShapeDtypeStruct((bsz, 2 * N_GROUPS, seq // CMP_STRIDE, CMP_STRIDE * HD), BF16),
        jax.ShapeDtypeStruct((bsz, seq // LANES, 4 * VROWS, LANES), BF16),
        jax.ShapeDtypeStruct((bsz, seq // TQ, N_GROUPS, 3 * HPG, TQ), F32),
        jax.ShapeDtypeStruct((bsz * seq, LANES), F32),
    )
    return pl.pallas_call(
        _inproj_attn_kernel,
        out_shape=out_shape,
        grid=(bsz * tpb,),
        in_specs=[pl.BlockSpec((tm, d), lambda i: (i, 0)),
                  pl.BlockSpec((1, 6, d), lambda i: (i // tpb, 0, 0)),
                  full(nw), full(wqT), full(wkn), full(wvT), full(wgT), full(wdt),
                  pl.BlockSpec((ROT_HALF, tm), lambda i: (0, i % tpb)),
                  pl.BlockSpec((ROT_HALF, tm), lambda i: (0, i % tpb)),
                  pl.BlockSpec((tm, LANES), lambda i: (i % tpb, 0)),
                  pl.BlockSpec((tm, LANES), lambda i: (i % tpb, 0)),
                  pl.BlockSpec((tm, LANES), lambda i: (i % tpb, 0)),
                  full(perm)],
        out_specs=(pl.BlockSpec((1, tm // TQ, N_HEADS * HD, TQ), lambda i: (i // tpb, i % tpb, 0, 0)),
                   pl.BlockSpec((1, N_GROUPS, tm, LANES), lambda i: (i // tpb, 0, i % tpb, 0)),
                   pl.BlockSpec((1, N_GROUPS, tm, HD), lambda i: (i // tpb, 0, i % tpb, 0)),
                   pl.BlockSpec((1, 2 * N_GROUPS, tm // CMP_STRIDE, CMP_STRIDE * HD),
                                lambda i: (i // tpb, 0, i % tpb, 0)),
                   pl.BlockSpec((1, tm // LANES, 4 * VROWS, LANES), lambda i: (i // tpb, i % tpb, 0, 0)),
                   pl.BlockSpec((1, tm // TQ, N_GROUPS, 3 * HPG, TQ), lambda i: (i // tpb, i % tpb, 0, 0, 0)),
                   pl.BlockSpec((tm, LANES), lambda i: (i, 0))),
        compiler_params=_cparams(("parallel",)),
        name="inproj_attn",
    )(x2d, mod, nw, wqT, wkn, wvT, wgT, wdt, cosT, sinT, cosK, sinA, sinB, perm)


def _inproj_big_kernel(x_ref, mod_ref, nw_ref, w_ref, o_ref, h_scr, *, gate_lo, gate_hi):
    j = pl.program_id(1)

    @pl.when(j == 0)
    def _():
        mod = mod_ref[0]
        h_scr[...] = _norm_mod(x_ref[...], nw_ref[...], mod[1:2], mod[0:1]).astype(BF16)

    acc = _dot(h_scr[...], w_ref[...])
    tn = acc.shape[1]
    col = j * tn + lax.broadcasted_iota(jnp.int32, (1, tn), 1)
    is_gate = (col >= gate_lo) & (col < gate_hi)
    o_ref[...] = jnp.where(is_gate, _sigmoid(acc), acc).astype(BF16)


def _inproj_big(x2d, mod, nw, wbig, bsz, seq):
    d = x2d.shape[1]
    n = wbig.shape[1]
    tm, tn = 1024, n // 4
    assert n % 4 == 0 and tn % LANES == 0
    tpb = seq // tm
    return pl.pallas_call(
        functools.partial(_inproj_big_kernel, gate_lo=SSM_INNER, gate_hi=SSM_INNER + 2 * d),
        out_shape=jax.ShapeDtypeStruct((bsz * seq, n), BF16),
        grid=(bsz * tpb, n // tn),
        in_specs=[pl.BlockSpec((tm, d), lambda i, j: (i, 0)),
                  pl.BlockSpec((1, 6, d), lambda i, j: (i // tpb, 0, 0)),
                  pl.BlockSpec((1, d), lambda i, j: (0, 0)),
                  pl.BlockSpec((d, tn), lambda i, j: (0, j))],
        out_specs=pl.BlockSpec((tm, tn), lambda i, j: (i, j)),
        scratch_shapes=[pltpu.VMEM((tm, d), BF16)],
        compiler_params=_cparams(("parallel", "arbitrary")),
        name="inproj_big",
    )(x2d, mod, nw, wbig)


def _compress_kernel(x_ref, pea_ref, peb_ref, w1a_ref, w1b_ref, w2_ref, o_ref, *, transpose_out, nc):
    x = x_ref[0, 0].astype(F32)
    nrow = x.shape[0]
    ya = _dot((x + pea_ref[...]).astype(BF16), w1a_ref[...])
    yb = _dot((x + peb_ref[...]).astype(BF16), w1b_ref[...])
    hp = ya + pltpu.roll(yb, nrow - 1, 0)
    row = lax.broadcasted_iota(jnp.int32, hp.shape, 0)
    hh = jnp.where(row < nc, _silu(hp), 0.0).astype(BF16)
    if transpose_out:
        o_ref[0, 0] = _dot_nt(w2_ref[...], hh).astype(BF16)
    else:
        o_ref[0, 0] = _dot(hh, w2_ref[...]).astype(BF16)


def _compress(kn16, slot0, pe, w1, w2, transpose_out, bsz, seq):
    nrow = seq // CMP_STRIDE
    nc = (seq - CMP_BLOCK) // CMP_STRIDE + 1
    half = CMP_STRIDE * HD
    pea = pe[:CMP_STRIDE].reshape(1, half)
    peb = pe[CMP_STRIDE:].reshape(1, half)
    w1a = w1[:half].astype(BF16)
    w1b = w1[half:].astype(BF16)
    if transpose_out:
        w2p = w2.T.astype(BF16)
        oshape, oblock = (bsz, N_GROUPS, HD, nrow), (1, 1, HD, nrow)
    else:
        w2p = w2.astype(BF16)
        oshape, oblock = (bsz, N_GROUPS, nrow, HD), (1, 1, nrow, HD)
    full = lambda a: pl.BlockSpec(a.shape, lambda b, g: (0,) * a.ndim)
    return pl.pallas_call(
        functools.partial(_compress_kernel, transpose_out=transpose_out, nc=nc),
        out_shape=jax.ShapeDtypeStruct(oshape, BF16),
        grid=(bsz, N_GROUPS),
        in_specs=[pl.BlockSpec((1, 1, nrow, half), lambda b, g: (b, slot0 + g, 0, 0)),
                  full(pea), full(peb), full(w1a), full(w1b), full(w2p)],
        out_specs=pl.BlockSpec(oblock, lambda b, g: (b, g, 0, 0)),
        compiler_params=_cparams(("parallel", "parallel")),
        name="compress_v" if transpose_out else "compress_k",
    )(kn16, pea, peb, w1a, w1b, w2p)


def _nsa_kernel(*refs, nb, tiles):
    def tile(u, carry):
        _nsa_tile(u, pl.program_id(2) * tiles + u, *refs, nb=nb)
        return carry

    lax.fori_loop(0, tiles, tile, 0)


def _nsa_tile(u, qi, q_ref, kc_ref, vcT_ref, ksa_ref, kw_ref, vsT_ref, vwT_ref, g_ref, bmapT_ref, o_ref,
              v_scr, selb_scr, m_scr, acc_scr, sa_scr, sb_scr, ma_scr, mb_scr, sw_scr, mw_scr, *, nb):
    t0 = qi * TQ
    wl = HPG * TQ
    q = q_ref[0, u]
    Q = jnp.concatenate([q[hh * HD:(hh + 1) * HD, :] for hh in range(HPG)], axis=1)
    tok1 = t0 + lax.broadcasted_iota(jnp.int32, (1, TQ), 1)
    tok = jnp.concatenate([tok1] * HPG, axis=1)

    def lanes8(x):
        return jnp.concatenate([x] * HPG, axis=1)

    nwt = WINDOW // TQ + 1
    kt0 = jnp.maximum(qi - WINDOW // TQ, 0)
    kw0 = pl.multiple_of(kt0 * TQ, TQ)
    sw = _dot(kw_ref[0, 0, pl.ds(kw0, nwt * TQ), :], Q)
    diff = tok1 - (kw0 + lax.broadcasted_iota(jnp.int32, (nwt * TQ, 1), 0))
    sw = sw + lanes8(jnp.where((diff >= 0) & (diff < WINDOW), 0.0, NEG))
    sw_scr[...] = sw
    mw_scr[...] = jnp.max(sw, axis=0, keepdims=True)

    ncp = kc_ref.shape[2]
    sc = _dot(kc_ref[0, 0], Q)
    n_idx = lax.broadcasted_iota(jnp.int32, (ncp, 1), 0)
    mask_c = (n_idx * CMP_STRIDE + (CMP_BLOCK - 1)) <= tok
    sc = jnp.where(mask_c, sc, NEG)
    mc = jnp.max(sc, axis=0, keepdims=True)
    ec = jnp.exp2(sc - mc)
    has_blk = tok >= CMP_BLOCK - 1
    pc = ec * jnp.where(has_blk, 1.0 / jnp.sum(ec, axis=0, keepdims=True), 0.0)
    o_c = _dot(vcT_ref[0, 0], pc.astype(BF16))

    psum = pc[:, 0:TQ]
    for hh in range(1, HPG):
        psum = psum + pc[:, hh * TQ:(hh + 1) * TQ]
    imp = _dot_split3(bmapT_ref[...], psum)
    jb = lax.broadcasted_iota(jnp.int32, (nb, 1), 0)
    cur = jnp.right_shift(tok1, int(np.log2(SLC_BLOCK)))
    forced = (jb == 0) | (jb == cur) | (jb == cur - 1)
    v = jnp.where(forced, jnp.inf, jnp.where(jb <= cur, imp, -jnp.inf))
    v_scr[...] = v
    jbf = lax.broadcasted_iota(jnp.int32, (nb, TQ), 0)

    def rank_body(i2, rank):
        for u in range(2):
            i = 2 * i2 + u
            vi = v_scr[pl.ds(i, 1), :]
            ge = jnp.where(vi >= v, 1.0, 0.0)
            gt = jnp.where(vi > v, 1.0, 0.0)
            rank = rank + jnp.where(jbf > i, ge, gt)
        return rank

    n_live = (TQ // SLC_BLOCK) * (qi + 1)
    rank_trips = jnp.where(n_live <= SLC_TOPK, 0, jnp.minimum(n_live, nb) // 2)
    rank = lax.fori_loop(0, rank_trips, rank_body, jnp.zeros((nb, TQ), F32))
    selb_scr[...] = jnp.where(rank < min(SLC_TOPK, nb), 0.0, NEG)

    tps = KS // TQ
    bps = KS // SLC_BLOCK
    m_scr[...] = jnp.full((1, wl), NEG, F32)
    acc_scr[...] = jnp.zeros((VROWS, wl), F32)
    st_hi = qi // tps

    def scores(st, s_scr, mx_scr, bias=None):
        sb = selb_scr[pl.ds(pl.multiple_of(st * bps, bps), bps), :]
        ext = lanes8(jnp.concatenate([sb, jnp.zeros((16 - bps, TQ), F32)], axis=0)).astype(BF16)
        qa = jnp.concatenate([Q, ext, jnp.zeros((LANES - HD - 16, wl), BF16)], axis=0)
        s = _dot(ksa_ref[0, 0, pl.ds(pl.multiple_of(st * KS, KS), KS), :], qa)
        if bias is not None:
            s = s + bias
        s_scr[...] = s
        mx_scr[...] = jnp.max(s, axis=0, keepdims=True)

    def update(s_scr, mx_scr, st):
        m_old = m_scr[...]
        m_new = jnp.maximum(m_old, mx_scr[...])
        p = jnp.exp2(s_scr[...] - m_new).astype(BF16)
        cps = KS // LANES
        vs = jnp.concatenate([vsT_ref[0, st * cps + j] for j in range(cps)], axis=1)
        acc_scr[...] = acc_scr[...] * jnp.exp2(m_old - m_new) + _dot(vs, p)
        m_scr[...] = m_new

    kpos = st_hi * KS + lax.broadcasted_iota(jnp.int32, (KS, 1), 0)
    scores(st_hi, sa_scr, ma_scr, lanes8(jnp.where(kpos <= tok1, 0.0, NEG)))

    pw = jnp.exp2(sw_scr[...] - mw_scr[...]).astype(BF16)
    cw0 = kt0 * (TQ // LANES)
    vwin = jnp.concatenate([vwT_ref[0, cw0 + j] for j in range(nwt * TQ // LANES)], axis=1)
    ow = _dot(vwin, pw)
    o_w = ow[:HD] / ow[HD:HD + 1]

    def pair_body(j, carry):
        ta = st_hi - 2 * j
        scores(ta - 1, sb_scr, mb_scr)
        update(sa_scr, ma_scr, ta)
        scores(ta - 2, sa_scr, ma_scr)
        update(sb_scr, mb_scr, ta - 1)
        return carry

    n_pairs = st_hi // 2
    lax.fori_loop(0, n_pairs, pair_body, 0)

    @pl.when(st_hi - 2 * n_pairs == 1)
    def _():
        scores(0, sb_scr, mb_scr)
        update(sa_scr, ma_scr, 1)
        update(sb_scr, mb_scr, 0)

    @pl.when(st_hi - 2 * n_pairs == 0)
    def _():
        update(sa_scr, ma_scr, 0)

    o_s = acc_scr[0:HD, :] / acc_scr[HD:HD + 1, :]

    gts = g_ref[0, u, 0]
    for hh in range(HPG):
        sl = slice(hh * TQ, (hh + 1) * TQ)
        o = (gts[hh:hh + 1] * o_c[:, sl] + gts[HPG + hh:HPG + hh + 1] * o_s[:, sl]
             + gts[2 * HPG + hh:2 * HPG + hh + 1] * o_w[:, sl])
        o_ref[0, u, hh * HD:(hh + 1) * HD, :] = o.astype(BF16)


def _nsa(qT, kc, vcT, ksa, kn, vT, gT, bmapT, bsz, seq):
    nq = seq // TQ
    nb = seq // SLC_BLOCK
    ncp = kc.shape[2]
    wl = HPG * TQ
    tiles = NSA_TILES_PER_STEP
    assert seq % KS == 0 and seq >= WINDOW + TQ and KS // SLC_BLOCK == 8 and nq % tiles == 0
    return pl.pallas_call(
        functools.partial(_nsa_kernel, nb=nb, tiles=tiles),
        out_shape=jax.ShapeDtypeStruct((bsz, nq, N_HEADS * HD, TQ), BF16),
        grid=(bsz, N_GROUPS, nq // tiles),
        in_specs=[pl.BlockSpec((1, tiles, HPG * HD, TQ), lambda b, g, i: (b, i, g, 0)),
                  pl.BlockSpec((1, 1, ncp, HD), lambda b, g, i: (b, g, 0, 0)),
                  pl.BlockSpec((1, 1, HD, ncp), lambda b, g, i: (b, g, 0, 0)),
                  pl.BlockSpec((1, 1, seq, LANES), lambda b, g, i: (b, g, 0, 0)),
                  pl.BlockSpec((1, 1, seq, HD), lambda b, g, i: (b, g, 0, 0)),
                  pl.BlockSpec((1, seq // LANES, VROWS, LANES), lambda b, g, i: (b, 0, g, 0)),
                  pl.BlockSpec((1, seq // LANES, VROWS, LANES), lambda b, g, i: (b, 0, 2 + g, 0)),
                  pl.BlockSpec((1, tiles, 1, 3 * HPG, TQ), lambda b, g, i: (b, i, g, 0, 0)),
                  pl.BlockSpec(bmapT.shape, lambda b, g, i: (0, 0))],
        out_specs=pl.BlockSpec((1, tiles, HPG * HD, TQ), lambda b, g, i: (b, i, g, 0)),
        scratch_shapes=[pltpu.VMEM((nb, TQ), F32),
                        pltpu.VMEM((nb, TQ), F32),
                        pltpu.VMEM((1, wl), F32),
                        pltpu.VMEM((VROWS, wl), F32),
                        pltpu.VMEM((KS, wl), F32),
                        pltpu.VMEM((KS, wl), F32),
                        pltpu.VMEM((1, wl), F32),
                        pltpu.VMEM((1, wl), F32),
                        pltpu.VMEM((WINDOW + TQ, wl), F32),
                        pltpu.VMEM((1, wl), F32)],
        compiler_params=_cparams(("parallel", "parallel", "arbitrary")),
        name="nsa_attention",
    )(qT, kc, vcT, ksa, kn, vT, vT, gT, bmapT)


def _ssd_kernel(*refs, chunks):
    def chunk(u, carry):
        _ssd_chunk(u, *refs)
        return carry

    lax.fori_loop(0, chunks, chunk, 0)


def _ssd_chunk(u, z_ref, xs_ref, bc_ref, dt_ref, cwx_ref, cwb_ref, cbx_ref, cbb_ref, dtb_ref, alog_ref,
               dsk_ref, nw_ref, o_ref, xpx_scr, xpb_scr, st_scr):
    L = SSM_CHUNK
    rows = pl.ds(pl.multiple_of(u * L, L), L)
    pad = SSM_PAD
    npair = SSM_HEADS // 2
    ppg = npair // SSM_GROUPS

    @pl.when((pl.program_id(1) == 0) & (u == 0))
    def _():
        xpx_scr[0:pad, :] = jnp.zeros((pad, xpx_scr.shape[1]), BF16)
        xpb_scr[0:pad, :] = jnp.zeros((pad, xpb_scr.shape[1]), BF16)
        st_scr[...] = jnp.zeros(st_scr.shape, F32)

    nshift = SSM_CONV - 1
    srow = lax.broadcasted_iota(jnp.int32, (nshift * L, pad + L), 0)
    scol = lax.broadcasted_iota(jnp.int32, (nshift * L, pad + L), 1)
    shift = jnp.where(scol == (srow & (L - 1)) + jnp.right_shift(srow, int(np.log2(L))) + (pad - nshift),
                      1.0, 0.0).astype(BF16)

    def conv_silu(x_ref, w_ref, b_ref, scr):
        x = x_ref[rows, :]
        scr[pad:pad + L, :] = x
        sh = _dot(shift, scr[...])
        acc = b_ref[...] + w_ref[nshift:nshift + 1, :] * x.astype(F32)
        for k in range(nshift):
            acc = acc + w_ref[k:k + 1, :] * sh[k * L:(k + 1) * L]
        scr[0:pad, :] = scr[L:L + pad, :]
        return _silu(acc)

    xs = conv_silu(xs_ref, cwx_ref, cbx_ref, xpx_scr)
    bcm = conv_silu(bc_ref, cwb_ref, cbb_ref, xpb_scr).astype(BF16)
    xs_b = xs.astype(BF16)

    dtv = dt_ref[rows, :] + dtb_ref[...]
    dt = jnp.maximum(dtv, 0.0) + jnp.log(1.0 + jnp.exp(-jnp.abs(dtv)))
    a = -jnp.exp(alog_ref[...]) * LOG2E
    row = lax.broadcasted_iota(jnp.int32, (L, L), 0)
    col = lax.broadcasted_iota(jnp.int32, (L, L), 1)
    causal = col <= row
    a_cum = _dot_split3(jnp.where(causal, 1.0, 0.0).astype(BF16), dt * a)
    a_cumT = a_cum.T
    dtT = dt.T
    a_last = a_cum[L - 1:L, :]
    a_lastT = a_cumT[:, L - 1:L]
    wT = dtT * jnp.exp2(a_lastT - a_cumT)
    arowT = a_cumT - jnp.log2(dtT)
    e_last = jnp.exp2(a_last)
    lo = lax.broadcasted_iota(jnp.int32, (1, LANES), 1) < SSM_HD

    ys = []
    for g in range(SSM_GROUPS):
        Bg = bcm[:, g * SSM_STATE:(g + 1) * SSM_STATE]
        Cg = bcm[:, (SSM_GROUPS + g) * SSM_STATE:(SSM_GROUPS + g + 1) * SSM_STATE]
        cb = _dot_nt(Cg, Bg)
        BgT = Bg.astype(F32).T
        for jp in range(ppg):
            pj = g * ppg + jp
            h0, h1 = 2 * pj, 2 * pj + 1
            xp = xs_b[:, pj * LANES:(pj + 1) * LANES]

            def head_mats(h):
                ecol = jnp.broadcast_to(a_cum[:, h:h + 1], (L, L))
                gm = (cb * jnp.exp2(jnp.where(causal, ecol - arowT[h:h + 1, :], NEG))).astype(BF16)
                bw = (BgT * wT[h:h + 1, :]).astype(BF16)
                return gm, bw, jnp.exp2(ecol)

            g0, b0, e0 = head_mats(h0)
            g1, b1, e1 = head_mats(h1)
            x2 = jnp.concatenate([jnp.where(lo, xp, jnp.zeros_like(xp)), jnp.where(lo, jnp.zeros_like(xp), xp)],
                                 axis=0)
            y_d = _dot(jnp.concatenate([g0, g1], axis=1), x2)
            st_prev = st_scr[pj]
            y_o = _dot(Cg, st_prev.astype(BF16)) * jnp.where(lo, e0, e1)
            st_new = _dot(jnp.concatenate([b0, b1], axis=1), x2)
            cdec = jnp.where(lo, jnp.broadcast_to(e_last[:, h0:h0 + 1], (1, LANES)),
                             jnp.broadcast_to(e_last[:, h1:h1 + 1], (1, LANES)))
            st_scr[pj] = st_prev * cdec + st_new
            sl = slice(pj * LANES, (pj + 1) * LANES)
            ys.append(y_d + y_o + dsk_ref[:, sl] * xs[:, sl])
    y = jnp.concatenate(ys, axis=1)
    y = y * _silu(z_ref[rows, :].astype(F32))
    gw = SSM_INNER // SSM_GROUPS
    for g in range(SSM_GROUPS):
        yg = y[:, g * gw:(g + 1) * gw]
        ms = jnp.mean(yg * yg, axis=-1, keepdims=True)
        o_ref[rows, g * gw:(g + 1) * gw] = (yg * lax.rsqrt(ms + EPS) * nw_ref[:, g * gw:(g + 1) * gw]).astype(BF16)


def _ssd(zxm, dt, cwx, cwb, cbx, cbb, dtb, alog, dsk, nw, bsz, seq):
    L = SSM_CHUNK
    chunks = SSD_CHUNKS_PER_STEP
    rows = chunks * L
    nstep = seq // rows
    assert seq % rows == 0
    nbc = 2 * SSM_GROUPS * SSM_STATE
    full = lambda a: pl.BlockSpec(a.shape, lambda b, c: (0,) * a.ndim)
    rowblk = lambda width, idx: pl.BlockSpec((rows, width), lambda b, c: (b * nstep + c, idx))
    return pl.pallas_call(
        functools.partial(_ssd_kernel, chunks=chunks),
        out_shape=jax.ShapeDtypeStruct((bsz * seq, SSM_INNER), BF16),
        grid=(bsz, nstep),
        in_specs=[rowblk(SSM_INNER, 0), rowblk(SSM_INNER, 2), rowblk(nbc, 6), rowblk(LANES, 0),
                  full(cwx), full(cwb), full(cbx), full(cbb), full(dtb), full(alog), full(dsk), full(nw)],
        out_specs=pl.BlockSpec((rows, SSM_INNER), lambda b, c: (b * nstep + c, 0)),
        scratch_shapes=[pltpu.VMEM((L + SSM_PAD, SSM_INNER), BF16),
                        pltpu.VMEM((L + SSM_PAD, nbc), BF16),
                        pltpu.VMEM((SSM_HEADS // 2, SSM_STATE, LANES), F32)],
        compiler_params=_cparams(("parallel", "arbitrary")),
        name="ssd",
    )(zxm, zxm, zxm, dt, cwx, cwb, cbx, cbb, dtb, alog, dsk, nw)


def _outproj_kernel(oT_ref, os_ref, ga_ref, gs_ref, x_ref, mod_ref, wa_ref, ws_ref, wo_ref, o_ref):
    oT = jnp.concatenate([oT_ref[0, j] for j in range(oT_ref.shape[1])], axis=1)
    ya = _dot_tn(oT, wa_ref[...])
    ys = _dot(os_ref[...], ws_ref[...])
    m = ga_ref[...].astype(F32) * ya + gs_ref[...].astype(F32) * ys
    r = _dot(m.astype(BF16), wo_ref[...])
    o_ref[...] = x_ref[...] + mod_ref[0][2:3] * r


def _outproj(oT, ossm, zxm, x2d, mod, wa, ws, wo, bsz, seq):
    d = x2d.shape[1]
    tm = 512
    tpb = seq // tm
    full = lambda a: pl.BlockSpec(a.shape, lambda i: (0,) * a.ndim)
    return pl.pallas_call(
        _outproj_kernel,
        out_shape=jax.ShapeDtypeStruct((bsz * seq, d), F32),
        grid=(bsz * tpb,),
        in_specs=[pl.BlockSpec((1, tm // TQ, N_HEADS * HD, TQ), lambda i: (i // tpb, i % tpb, 0, 0)),
                  pl.BlockSpec((tm, SSM_INNER), lambda i: (i, 0)),
                  pl.BlockSpec((tm, d), lambda i: (i, 2)),
                  pl.BlockSpec((tm, d), lambda i: (i, 3)),
                  pl.BlockSpec((tm, d), lambda i: (i, 0)),
                  pl.BlockSpec((1, 6, d), lambda i: (i // tpb, 0, 0)),
                  full(wa), full(ws), full(wo)],
        out_specs=pl.BlockSpec((tm, d), lambda i: (i, 0)),
        compiler_params=_cparams(("parallel",)),
        name="outproj",
    )(oT, ossm, zxm, zxm, x2d, mod, wa, ws, wo)


def _ffn_kernel(x_ref, mod_ref, n2_ref, nf_ref, wg_ref, wu_ref, wd_ref, o_ref, *, final_norm):
    x = x_ref[...]
    mod = mod_ref[0]
    h = _norm_mod(x, n2_ref[...], mod[4:5], mod[3:4]).astype(BF16)
    a = _silu(_dot(h, wg_ref[...])) * _dot(h, wu_ref[...])
    r = _dot(a.astype(BF16), wd_ref[...])
    x2 = x + mod[5:6] * r
    if final_norm:
        ms = jnp.mean(x2 * x2, axis=-1, keepdims=True)
        x2 = x2 * lax.rsqrt(ms + EPS) * nf_ref[...]
    o_ref[...] = x2


def _ffn(x1, mod, n2, nf, wg, wu, wd, final_norm, bsz, seq):
    d = x1.shape[1]
    tm = 512
    tpb = seq // tm
    const = lambda a: pl.BlockSpec(a.shape, lambda i: (0,) * a.ndim, pipeline_mode=pl.Buffered(1))
    return pl.pallas_call(
        functools.partial(_ffn_kernel, final_norm=final_norm),
        out_shape=jax.ShapeDtypeStruct((bsz * seq, d), F32),
        grid=(bsz * tpb,),
        in_specs=[pl.BlockSpec((tm, d), lambda i: (i, 0)),
                  pl.BlockSpec((1, 6, d), lambda i: (i // tpb, 0, 0)),
                  const(n2), const(nf), const(wg), const(wu), const(wd)],
        out_specs=pl.BlockSpec((tm, d), lambda i: (i, 0)),
        compiler_params=_cparams(("parallel",)),
        name="ffn",
    )(x1, mod, n2, nf, wg, wu, wd)


def _rotary_tables(seq):
    inv_freq = jnp.asarray(np.power(ROPE_THETA, -np.arange(0, 2 * ROT_HALF, 2) / (2 * ROT_HALF)).astype(np.float32))
    ang = jnp.arange(seq).astype(F32)[:, None] * inv_freq[None, :]
    cos, sin = jnp.cos(ang), jnp.sin(ang)
    ones = jnp.ones((seq, HD - 2 * ROT_HALF), F32)
    zeros8 = jnp.zeros((seq, ROT_HALF), F32)
    zeros = jnp.zeros((seq, HD - 2 * ROT_HALF), F32)
    cosK = jnp.concatenate([cos, cos, ones], axis=1)
    sinA = jnp.concatenate([-sin, zeros8, zeros], axis=1)
    sinB = jnp.concatenate([zeros8, sin, zeros], axis=1)
    rep = LANES // HD
    tile = lambda t: jnp.concatenate([t] * rep, axis=1)
    return cos.T, sin.T, tile(cosK), tile(sinA), tile(sinB)


def _selection_block_map_T(seq):
    nc = (seq - CMP_BLOCK) // CMP_STRIDE + 1
    nb = seq // SLC_BLOCK
    cs = np.arange(nc) * CMP_STRIDE
    bs = np.arange(nb) * SLC_BLOCK
    ov = np.minimum(cs[:, None] + CMP_BLOCK, bs[None, :] + SLC_BLOCK) - np.maximum(cs[:, None], bs[None, :])
    bmap = (np.clip(ov, 0, None) / CMP_BLOCK).astype(np.float32)
    out = np.zeros((nb, seq // CMP_STRIDE), np.float32)
    out[:, :nc] = bmap.T
    return jnp.asarray(out)


def kernel(x, c, w_ada, b_ada, norm1_w, w_in, cmp_pe_k, cmp_w1_k, cmp_w2_k, cmp_pe_v, cmp_w1_v, cmp_w2_v,
           conv_w, conv_b, dt_bias, a_log, d_skip, ssm_norm_w, w_attn_out, w_ssm_out, w_o, norm2_w,
           w_gate, w_up, w_down, norm_f_w):
    bsz, seq, d = x.shape
    depth = w_ada.shape[0]
    q_w = N_HEADS * HD
    kv_w = N_GROUPS * HD
    xbc_w = SSM_INNER + 2 * SSM_GROUPS * SSM_STATE
    offs = np.cumsum([q_w, 6 * kv_w, 3 * N_HEADS, SSM_INNER, xbc_w, SSM_HEADS])
    tabs = _rotary_tables(seq)
    bmapT = _selection_block_map_T(seq)
    x2d = x.reshape(bsz * seq, d)
    for l in range(depth):
        mod = _ada_mod(c, w_ada[l], b_ada[l]).reshape(bsz, 6, d)
        wi = w_in[l]
        wq = wi[:, :offs[0]]
        wkv = wi[:, offs[0]:offs[1]].reshape(d, 6, N_GROUPS * HD)
        wg = wi[:, offs[1]:offs[2]].reshape(d, N_GROUPS, HPG, 3)
        wz = wi[:, offs[2]:offs[3]]
        wxbc = wi[:, offs[3]:offs[4]]
        wdt = wi[:, offs[4]:offs[5]]
        wmg = wi[:, offs[5]:]
        wqT = wq.T.astype(BF16)
        zcol = jnp.zeros((d, HD), wi.dtype)
        wkn = jnp.concatenate([wkv[:, 2, :HD], zcol, wkv[:, 2, HD:], zcol, wkv[:, 0], wkv[:, 4], wkv[:, 1]],
                              axis=1).astype(BF16)
        wvT = jnp.concatenate([wkv[:, 3], wkv[:, 5]], axis=1).T.astype(BF16)
        wgT = jnp.transpose(wg, (1, 3, 2, 0)).reshape(3 * N_HEADS, d).astype(BF16)
        wdtp = jnp.pad(wdt, ((0, 0), (0, LANES - SSM_HEADS))).astype(BF16)
        wbig = jnp.concatenate([wz, wmg, wxbc], axis=1).astype(BF16)
        nw1 = norm1_w[l].reshape(1, d)

        qT, ksa, kn, kcv, vT, gT, dt = _inproj_attn(x2d, mod, nw1, wqT, wkn, wvT, wgT, wdtp, tabs, bsz, seq)
        zxm = _inproj_big(x2d, mod, nw1, wbig, bsz, seq)

        kc = _compress(kcv, 0, cmp_pe_k[l], cmp_w1_k[l], cmp_w2_k[l], False, bsz, seq)
        vcT = _compress(kcv, N_GROUPS, cmp_pe_v[l], cmp_w1_v[l], cmp_w2_v[l], True, bsz, seq)
        oT = _nsa(qT, kc, vcT, ksa, kn, vT, gT, bmapT.astype(BF16), bsz, seq)

        cw = conv_w[l]
        cb = conv_b[l].reshape(1, xbc_w)
        pad_h = lambda v: jnp.pad(v.reshape(1, SSM_HEADS), ((0, 0), (0, LANES - SSM_HEADS)))
        dsk = jnp.repeat(d_skip[l], SSM_HD).reshape(1, SSM_INNER)
        ossm = _ssd(zxm, dt, cw[:, :SSM_INNER], cw[:, SSM_INNER:], cb[:, :SSM_INNER], cb[:, SSM_INNER:],
                    pad_h(dt_bias[l]), pad_h(a_log[l]), dsk, ssm_norm_w[l].reshape(1, SSM_INNER), bsz, seq)

        x1 = _outproj(oT, ossm, zxm, x2d, mod, w_attn_out[l].astype(BF16), w_ssm_out[l].astype(BF16),
                      w_o[l].astype(BF16), bsz, seq)
        x2d = _ffn(x1, mod, norm2_w[l].reshape(1, d), norm_f_w.reshape(1, d), w_gate[l].astype(BF16),
                   w_up[l].astype(BF16), w_down[l].astype(BF16), l == depth - 1, bsz, seq)
    return x2d.reshape(bsz, seq, d)
```

```python
import functools

import numpy as np
import jax
import jax.numpy as jnp
from jax import lax
from jax.experimental import pallas as pl
from jax.experimental.pallas import tpu as pltpu

F32 = jnp.float32
BF16 = jnp.bfloat16
HIGHEST = lax.Precision.HIGHEST

EPS = 1e-6
NEG = -1e30

N_HEADS = 16
N_GROUPS = 2
HPG = N_HEADS // N_GROUPS
HD = 64
ROT_HALF = 8
ROPE_THETA = 500000.0
CMP_BLOCK = 32
CMP_STRIDE = 16
SLC_BLOCK = 64
SLC_TOPK = 16
WINDOW = 512

SSM_HEADS = 32
SSM_HD = 64
SSM_GROUPS = 4
SSM_STATE = 128
SSM_CONV = 4
SSM_CHUNK = 128
SSM_INNER = SSM_HEADS * SSM_HD
SSM_PAD = 16

TQ = 256
NSA_TILES_PER_STEP = 2
SSD_CHUNKS_PER_STEP = 4
KS = 512
LANES = 128
VROWS = HD + 16
LOG2E = float(np.log2(np.e))
VMEM_LIMIT = 56 * 1024 * 1024


def _cparams(sem, vmem=VMEM_LIMIT):
    return pltpu.CompilerParams(dimension_semantics=sem, vmem_limit_bytes=vmem)


def _sigmoid(x):
    return 0.5 + 0.5 * jnp.tanh(0.5 * x)


def _silu(x):
    hx = 0.5 * x
    return hx + hx * jnp.tanh(hx)


def _norm_mod(x, w, sc, sh):
    ms = jnp.mean(x * x, axis=-1, keepdims=True)
    return (x * lax.rsqrt(ms + EPS) * w) * (1.0 + sc) + sh


def _dot(a, b):
    return jnp.dot(a, b, preferred_element_type=F32)


def _dot_split3(a, x):
    hi = x.astype(BF16)
    r1 = x - hi.astype(F32)
    mid = r1.astype(BF16)
    lo = (r1 - mid.astype(F32)).astype(BF16)
    return _dot(a, hi) + _dot(a, mid) + _dot(a, lo)


def _dot_nt(a, b):
    return lax.dot_general(a, b, (((1,), (1,)), ((), ())), preferred_element_type=F32)


def _dot_tn(a, b):
    return lax.dot_general(a, b, (((0,), (0,)), ((), ())), preferred_element_type=F32)


def _ada_kernel(c_ref, w_ref, b_ref, o_ref):
    c = c_ref[...]
    o_ref[...] = jnp.dot(_silu(c), w_ref[...], precision=HIGHEST, preferred_element_type=F32) + b_ref[...]


def _ada_mod(c, w, b):
    bsz, d = c.shape
    n = w.shape[1]
    tn = 1024
    return pl.pallas_call(
        _ada_kernel,
        out_shape=jax.ShapeDtypeStruct((bsz, n), F32),
        grid=(n // tn,),
        in_specs=[pl.BlockSpec((bsz, d), lambda j: (0, 0)),
                  pl.BlockSpec((d, tn), lambda j: (0, j)),
                  pl.BlockSpec((1, tn), lambda j: (0, j))],
        out_specs=pl.BlockSpec((bsz, tn), lambda j: (0, j)),
        compiler_params=_cparams(("parallel",)),
        name="ada_mod",
    )(c, w, b.reshape(1, n))


def _inproj_attn_kernel(x_ref, mod_ref, nw_ref, wqT_ref, wkn_ref, wvT_ref, wgT_ref, wdt_ref,
                        cosT_ref, sinT_ref, cosK_ref, sinA_ref, sinB_ref, perm_ref,
                        qT_ref, ksa_ref, kn_ref, kcv_ref, vT_ref, gT_ref, dt_ref, h_ref):
    tm = x_ref.shape[0]
    mod = mod_ref[0]
    h = _norm_mod(x_ref[...], nw_ref[...], mod[1:2], mod[0:1]).astype(BF16)
    h_ref[...] = h

    qT = _dot_nt(wqT_ref[...], h)
    cT = cosT_ref[...]
    sT = sinT_ref[...]
    scale = HD ** -0.5 * LOG2E
    for hh in range(N_HEADS):
        r0 = hh * HD
        x1 = qT[r0:r0 + ROT_HALF]
        x2 = qT[r0 + ROT_HALF:r0 + 2 * ROT_HALF]
        blk = jnp.concatenate([x1 * cT - x2 * sT, x2 * cT + x1 * sT, qT[r0 + 2 * ROT_HALF:r0 + HD]], axis=0)
        blk = (blk * scale).astype(BF16)
        for j in range(tm // TQ):
            qT_ref[0, j, r0:r0 + HD, :] = blk[:, j * TQ:(j + 1) * TQ]

    kn = _dot(h, wkn_ref[...])
    nrot = 8 * HD
    kk = kn[:, :nrot]
    reps = nrot // LANES
    cK = jnp.concatenate([cosK_ref[...]] * reps, axis=1)
    sA = jnp.concatenate([sinA_ref[...]] * reps, axis=1)
    sB = jnp.concatenate([sinB_ref[...]] * reps, axis=1)
    kk = kk * cK + pltpu.roll(kk, nrot - ROT_HALF, 1) * sA + pltpu.roll(kk, ROT_HALF, 1) * sB
    row = lax.broadcasted_iota(jnp.int32, (tm, LANES), 0)
    lane = lax.broadcasted_iota(jnp.int32, (tm, LANES), 1)
    bps = KS // SLC_BLOCK
    blk = jnp.right_shift(row, int(np.log2(SLC_BLOCK)))
    onehot = jnp.where(lane == HD + (blk & (bps - 1)), 1.0, 0.0)
    for g in range(N_GROUPS):
        ksa_ref[0, g] = (kk[:, g * LANES:(g + 1) * LANES] + onehot).astype(BF16)
    for i in range(N_GROUPS):
        kn_ref[0, i] = kk[:, (6 + i) * HD:(7 + i) * HD].astype(BF16)

    nr = tm // CMP_STRIDE
    lo = lax.broadcasted_iota(jnp.int32, (nr, LANES), 1) < HD
    for i, pk in enumerate((kk[:, 4 * HD:6 * HD], kn[:, nrot:nrot + 2 * HD])):
        pm = _dot(perm_ref[...], pk.astype(BF16))
        for j in range(CMP_STRIDE // 2):
            a = pm[2 * j * nr:(2 * j + 1) * nr]
            b = pm[(2 * j + 1) * nr:(2 * j + 2) * nr]
            kcv_ref[0, 2 * i, :, j * LANES:(j + 1) * LANES] = jnp.where(lo, a, pltpu.roll(b, HD, 1)).astype(BF16)
            kcv_ref[0, 2 * i + 1, :, j * LANES:(j + 1) * LANES] = jnp.where(lo, pltpu.roll(a, HD, 1), b).astype(BF16)

    vT = _dot_nt(wvT_ref[...], h).astype(BF16)
    ones = jnp.ones((VROWS - HD, LANES), BF16)
    for j in range(tm // LANES):
        for r in range(4):
            vT_ref[0, j, r * VROWS:r * VROWS + HD, :] = vT[r * HD:(r + 1) * HD, j * LANES:(j + 1) * LANES]
            vT_ref[0, j, r * VROWS + HD:(r + 1) * VROWS, :] = ones

    gates = _sigmoid(_dot_nt(wgT_ref[...], h))
    for j in range(tm // TQ):
        for g in range(N_GROUPS):
            gT_ref[0, j, g] = gates[g * 3 * HPG:(g + 1) * 3 * HPG, j * TQ:(j + 1) * TQ]
    dt_ref[...] = _dot(h, wdt_ref[...])


def _inproj_attn(x2d, mod, nw, wqT, wkn, wvT, wgT, wdt, tabs, bsz, seq):
    d = x2d.shape[1]
    tm = KS
    tpb = seq // tm
    cosT, sinT, cosK, sinA, sinB = tabs
    full = lambda a: pl.BlockSpec(a.shape, lambda i: (0,) * a.ndim)
    orow = np.arange(tm)
    src = CMP_STRIDE * (orow % (tm // CMP_STRIDE)) + orow // (tm // CMP_STRIDE)
    perm = jnp.asarray(np.arange(tm)[None, :] == src[:, None], BF16)
    out_shape = (
        jax.ShapeDtypeStruct((bsz, seq // TQ, N_HEADS * HD, TQ), BF16),
        jax.ShapeDtypeStruct((bsz, N_GROUPS, seq, LANES), BF16),
        jax.ShapeDtypeStruct((bsz, N_GROUPS, seq, HD), BF16),
        jax.ShapeDtypeStruct((bsz, 2 * N_GROUPS, seq // CMP_STRIDE, CMP_STRIDE * HD), BF16),
        jax.ShapeDtypeStruct((bsz, seq // LANES, 4 * VROWS, LANES), BF16),
        jax.ShapeDtypeStruct((bsz, seq // TQ, N_GROUPS, 3 * HPG, TQ), F32),
        jax.ShapeDtypeStruct((bsz * seq, LANES), F32),
        jax.ShapeDtypeStruct((bsz * seq, d), BF16),
    )
    return pl.pallas_call(
        _inproj_attn_kernel,
        out_shape=out_shape,
        grid=(bsz * tpb,),
        in_specs=[pl.BlockSpec((tm, d), lambda i: (i, 0)),
                  pl.BlockSpec((1, 6, d), lambda i: (i // tpb, 0, 0)),
                  full(nw), full(wqT), full(wkn), full(wvT), full(wgT), full(wdt),
                  pl.BlockSpec((ROT_HALF, tm), lambda i: (0, i % tpb)),
                  pl.BlockSpec((ROT_HALF, tm), lambda i: (0, i % tpb)),
                  pl.BlockSpec((tm, LANES), lambda i: (i % tpb, 0)),
                  pl.BlockSpec((tm, LANES), lambda i: (i % tpb, 0)),
                  pl.BlockSpec((tm, LANES), lambda i: (i % tpb, 0)),
                  full(perm)],
        out_specs=(pl.BlockSpec((1, tm // TQ, N_HEADS * HD, TQ), lambda i: (i // tpb, i % tpb, 0, 0)),
                   pl.BlockSpec((1, N_GROUPS, tm, LANES), lambda i: (i // tpb, 0, i % tpb, 0)),
                   pl.BlockSpec((1, N_GROUPS, tm, HD), lambda i: (i // tpb, 0, i % tpb, 0)),
                   pl.BlockSpec((1, 2 * N_GROUPS, tm // CMP_STRIDE, CMP_STRIDE * HD),
                                lambda i: (i // tpb, 0, i % tpb, 0)),
                   pl.BlockSpec((1, tm // LANES, 4 * VROWS, LANES), lambda i: (i // tpb, i % tpb, 0, 0)),
                   pl.BlockSpec((1, tm // TQ, N_GROUPS, 3 * HPG, TQ), lambda i: (i // tpb, i % tpb, 0, 0, 0)),
                   pl.BlockSpec((tm, LANES), lambda i: (i, 0)),
                   pl.BlockSpec((tm, d), lambda i: (i, 0))),
        compiler_params=_cparams(("parallel",)),
        name="inproj_attn",
    )(x2d, mod, nw, wqT, wkn, wvT, wgT, wdt, cosT, sinT, cosK, sinA, sinB, perm)


def _inproj_big_kernel(h_ref, w_ref, o_ref, *, gate_lo, gate_hi):
    j = pl.program_id(1)
    acc = _dot(h_ref[...], w_ref[...])
    tn = acc.shape[1]
    col = j * tn + lax.broadcasted_iota(jnp.int32, (1, tn), 1)
    is_gate = (col >= gate_lo) & (col < gate_hi)
    o_ref[...] = jnp.where(is_gate, _sigmoid(acc), acc).astype(BF16)


def _inproj_big(h2d, wbig, bsz, seq):
    d = h2d.shape[1]
    n = wbig.shape[1]
    tm, tn = 1024, n // 4
    assert n % 4 == 0 and tn % LANES == 0
    return pl.pallas_call(
        functools.partial(_inproj_big_kernel, gate_lo=SSM_INNER, gate_hi=SSM_INNER + 2 * d),
        out_shape=jax.ShapeDtypeStruct((bsz * seq, n), BF16),
        grid=(bsz * seq // tm, n // tn),
        in_specs=[pl.BlockSpec((tm, d), lambda i, j: (i, 0)),
                  pl.BlockSpec((d, tn), lambda i, j: (0, j))],
        out_specs=pl.BlockSpec((tm, tn), lambda i, j: (i, j)),
        compiler_params=_cparams(("parallel", "arbitrary")),
        name="inproj_big",
    )(h2d, wbig)


def _compress_kernel(x_ref, pea_ref, peb_ref, w1a_ref, w1b_ref, w2_ref, o_ref, *, transpose_out, nc):
    x = x_ref[0, 0].astype(F32)
    nrow = x.shape[0]
    ya = _dot((x + pea_ref[...]).astype(BF16), w1a_ref[...])
    yb = _dot((x + peb_ref[...]).astype(BF16), w1b_ref[...])
    hp = ya + pltpu.roll(yb, nrow - 1, 0)
    row = lax.broadcasted_iota(jnp.int32, hp.shape, 0)
    hh = jnp.where(row < nc, _silu(hp), 0.0).astype(BF16)
    if transpose_out:
        o_ref[0, 0] = _dot_nt(w2_ref[...], hh).astype(BF16)
    else:
        o_ref[0, 0] = _dot(hh, w2_ref[...]).astype(BF16)


def _compress(kn16, slot0, pe, w1, w2, transpose_out, bsz, seq):
    nrow = seq // CMP_STRIDE
    nc = (seq - CMP_BLOCK) // CMP_STRIDE + 1
    half = CMP_STRIDE * HD
    pea = pe[:CMP_STRIDE].reshape(1, half)
    peb = pe[CMP_STRIDE:].reshape(1, half)
    w1a = w1[:half].astype(BF16)
    w1b = w1[half:].astype(BF16)
    if transpose_out:
        w2p = w2.T.astype(BF16)
        oshape, oblock = (bsz, N_GROUPS, HD, nrow), (1, 1, HD, nrow)
    else:
        w2p = w2.astype(BF16)
        oshape, oblock = (bsz, N_GROUPS, nrow, HD), (1, 1, nrow, HD)
    full = lambda a: pl.BlockSpec(a.shape, lambda b, g: (0,) * a.ndim)
    return pl.pallas_call(
        functools.partial(_compress_kernel, transpose_out=transpose_out, nc=nc),
        out_shape=jax.ShapeDtypeStruct(oshape, BF16),
        grid=(bsz, N_GROUPS),
        in_specs=[pl.BlockSpec((1, 1, nrow, half), lambda b, g: (b, slot0 + g, 0, 0)),
                  full(pea), full(peb), full(w1a), full(w1b), full(w2p)],
        out_specs=pl.BlockSpec(oblock, lambda b, g: (b, g, 0, 0)),
        compiler_params=_cparams(("parallel", "parallel")),
        name="compress_v" if transpose_out else "compress_k",
    )(kn16, pea, peb, w1a, w1b, w2p)


def _nsa_kernel(*refs, nb, tiles):
    def tile(u, carry):
        _nsa_tile(u, pl.program_id(2) * tiles + u, *refs, nb=nb)
        return carry

    lax.fori_loop(0, tiles, tile, 0)


def _nsa_tile(u, qi, q_ref, kc_ref, vcT_ref, ksa_ref, kw_ref, vsT_ref, vwT_ref, g_ref, bmapT_ref, o_ref,
              v_scr, selb_scr, m_scr, acc_scr, sa_scr, sb_scr, ma_scr, mb_scr, sw_scr, mw_scr, *, nb):
    t0 = qi * TQ
    wl = HPG * TQ
    q = q_ref[0, u]
    Q = jnp.concatenate([q[hh * HD:(hh + 1) * HD, :] for hh in range(HPG)], axis=1)
    tok1 = t0 + lax.broadcasted_iota(jnp.int32, (1, TQ), 1)
    tok = jnp.concatenate([tok1] * HPG, axis=1)

    def lanes8(x):
        return jnp.concatenate([x] * HPG, axis=1)

    nwt = WINDOW // TQ + 1
    kt0 = jnp.maximum(qi - WINDOW // TQ, 0)
    kw0 = pl.multiple_of(kt0 * TQ, TQ)
    sw = _dot(kw_ref[0, 0, pl.ds(kw0, nwt * TQ), :], Q)
    diff = tok1 - (kw0 + lax.broadcasted_iota(jnp.int32, (nwt * TQ, 1), 0))
    sw = sw + lanes8(jnp.where((diff >= 0) & (diff < WINDOW), 0.0, NEG))
    sw_scr[...] = sw
    mw_scr[...] = jnp.max(sw, axis=0, keepdims=True)

    ncp = kc_ref.shape[2]
    sc = _dot(kc_ref[0, 0], Q)
    n_idx = lax.broadcasted_iota(jnp.int32, (ncp, 1), 0)
    mask_c = (n_idx * CMP_STRIDE + (CMP_BLOCK - 1)) <= tok
    sc = jnp.where(mask_c, sc, NEG)
    mc = jnp.max(sc, axis=0, keepdims=True)
    ec = jnp.exp2(sc - mc)
    has_blk = tok >= CMP_BLOCK - 1
    pc = ec * jnp.where(has_blk, 1.0 / jnp.sum(ec, axis=0, keepdims=True), 0.0)
    o_c = _dot(vcT_ref[0, 0], pc.astype(BF16))

    psum = pc[:, 0:TQ]
    for hh in range(1, HPG):
        psum = psum + pc[:, hh * TQ:(hh + 1) * TQ]
    imp = _dot_split3(bmapT_ref[...], psum)
    jb = lax.broadcasted_iota(jnp.int32, (nb, 1), 0)
    cur = jnp.right_shift(tok1, int(np.log2(SLC_BLOCK)))
    forced = (jb == 0) | (jb == cur) | (jb == cur - 1)
    v = jnp.where(forced, jnp.inf, jnp.where(jb <= cur, imp, -jnp.inf))
    v_scr[...] = v
    jbf = lax.broadcasted_iota(jnp.int32, (nb, TQ), 0)

    def rank_body(i2, rank):
        for u in range(2):
            i = 2 * i2 + u
            vi = v_scr[pl.ds(i, 1), :]
            ge = jnp.where(vi >= v, 1.0, 0.0)
            gt = jnp.where(vi > v, 1.0, 0.0)
            rank = rank + jnp.where(jbf > i, ge, gt)
        return rank

    n_live = (TQ // SLC_BLOCK) * (qi + 1)
    rank_trips = jnp.where(n_live <= SLC_TOPK, 0, jnp.minimum(n_live, nb) // 2)
    rank = lax.fori_loop(0, rank_trips, rank_body, jnp.zeros((nb, TQ), F32))
    selb_scr[...] = jnp.where(rank < min(SLC_TOPK, nb), 0.0, NEG)

    tps = KS // TQ
    bps = KS // SLC_BLOCK
    m_scr[...] = jnp.full((1, wl), NEG, F32)
    acc_scr[...] = jnp.zeros((VROWS, wl), F32)
    st_hi = qi // tps

    def scores(st, s_scr, mx_scr, bias=None):
        sb = selb_scr[pl.ds(pl.multiple_of(st * bps, bps), bps), :]
        ext = lanes8(jnp.concatenate([sb, jnp.zeros((16 - bps, TQ), F32)], axis=0)).astype(BF16)
        qa = jnp.concatenate([Q, ext, jnp.zeros((LANES - HD - 16, wl), BF16)], axis=0)
        s = _dot(ksa_ref[0, 0, pl.ds(pl.multiple_of(st * KS, KS), KS), :], qa)
        if bias is not None:
            s = s + bias
        s_scr[...] = s
        mx_scr[...] = jnp.max(s, axis=0, keepdims=True)

    def update(s_scr, mx_scr, st):
        m_old = m_scr[...]
        m_new = jnp.maximum(m_old, mx_scr[...])
        p = jnp.exp2(s_scr[...] - m_new).astype(BF16)
        cps = KS // LANES
        vs = jnp.concatenate([vsT_ref[0, st * cps + j] for j in range(cps)], axis=1)
        acc_scr[...] = acc_scr[...] * jnp.exp2(m_old - m_new) + _dot(vs, p)
        m_scr[...] = m_new

    kpos = st_hi * KS + lax.broadcasted_iota(jnp.int32, (KS, 1), 0)
    scores(st_hi, sa_scr, ma_scr, lanes8(jnp.where(kpos <= tok1, 0.0, NEG)))

    pw = jnp.exp2(sw_scr[...] - mw_scr[...]).astype(BF16)
    cw0 = kt0 * (TQ // LANES)
    vwin = jnp.concatenate([vwT_ref[0, cw0 + j] for j in range(nwt * TQ // LANES)], axis=1)
    ow = _dot(vwin, pw)
    o_w = ow[:HD] / ow[HD:HD + 1]

    def pair_body(j, carry):
        ta = st_hi - 2 * j
        scores(ta - 1, sb_scr, mb_scr)
        update(sa_scr, ma_scr, ta)
        scores(ta - 2, sa_scr, ma_scr)
        update(sb_scr, mb_scr, ta - 1)
        return carry

    n_pairs = st_hi // 2
    lax.fori_loop(0, n_pairs, pair_body, 0)

    @pl.when(st_hi - 2 * n_pairs == 1)
    def _():
        scores(0, sb_scr, mb_scr)
        update(sa_scr, ma_scr, 1)
        update(sb_scr, mb_scr, 0)

    @pl.when(st_hi - 2 * n_pairs == 0)
    def _():
        update(sa_scr, ma_scr, 0)

    o_s = acc_scr[0:HD, :] / acc_scr[HD:HD + 1, :]

    gts = g_ref[0, u, 0]
    for hh in range(HPG):
        sl = slice(hh * TQ, (hh + 1) * TQ)
        o = (gts[hh:hh + 1] * o_c[:, sl] + gts[HPG + hh:HPG + hh + 1] * o_s[:, sl]
             + gts[2 * HPG + hh:2 * HPG + hh + 1] * o_w[:, sl])
        o_ref[0, u, hh * HD:(hh + 1) * HD, :] = o.astype(BF16)


def _nsa(qT, kc, vcT, ksa, kn, vT, gT, bmapT, bsz, seq):
    nq = seq // TQ
    nb = seq // SLC_BLOCK
    ncp = kc.shape[2]
    wl = HPG * TQ
    tiles = NSA_TILES_PER_STEP
    assert seq % KS == 0 and seq >= WINDOW + TQ and KS // SLC_BLOCK == 8 and nq % tiles == 0
    return pl.pallas_call(
        functools.partial(_nsa_kernel, nb=nb, tiles=tiles),
        out_shape=jax.ShapeDtypeStruct((bsz, nq, N_HEADS * HD, TQ), BF16),
        grid=(bsz, N_GROUPS, nq // tiles),
        in_specs=[pl.BlockSpec((1, tiles, HPG * HD, TQ), lambda b, g, i: (b, i, g, 0)),
                  pl.BlockSpec((1, 1, ncp, HD), lambda b, g, i: (b, g, 0, 0)),
                  pl.BlockSpec((1, 1, HD, ncp), lambda b, g, i: (b, g, 0, 0)),
                  pl.BlockSpec((1, 1, seq, LANES), lambda b, g, i: (b, g, 0, 0)),
                  pl.BlockSpec((1, 1, seq, HD), lambda b, g, i: (b, g, 0, 0)),
                  pl.BlockSpec((1, seq // LANES, VROWS, LANES), lambda b, g, i: (b, 0, g, 0)),
                  pl.BlockSpec((1, seq // LANES, VROWS, LANES), lambda b, g, i: (b, 0, 2 + g, 0)),
                  pl.BlockSpec((1, tiles, 1, 3 * HPG, TQ), lambda b, g, i: (b, i, g, 0, 0)),
                  pl.BlockSpec(bmapT.shape, lambda b, g, i: (0, 0))],
        out_specs=pl.BlockSpec((1, tiles, HPG * HD, TQ), lambda b, g, i: (b, i, g, 0)),
        scratch_shapes=[pltpu.VMEM((nb, TQ), F32),
                        pltpu.VMEM((nb, TQ), F32),
                        pltpu.VMEM((1, wl), F32),
                        pltpu.VMEM((VROWS, wl), F32),
                        pltpu.VMEM((KS, wl), F32),
                        pltpu.VMEM((KS, wl), F32),
                        pltpu.VMEM((1, wl), F32),
                        pltpu.VMEM((1, wl), F32),
                        pltpu.VMEM((WINDOW + TQ, wl), F32),
                        pltpu.VMEM((1, wl), F32)],
        compiler_params=_cparams(("parallel", "parallel", "arbitrary")),
        name="nsa_attention",
    )(qT, kc, vcT, ksa, kn, vT, vT, gT, bmapT)


def _ssd_kernel(*refs, chunks):
    def chunk(u, carry):
        _ssd_chunk(u, *refs)
        return carry

    lax.fori_loop(0, chunks, chunk, 0)


def _ssd_chunk(u, z_ref, xs_ref, bc_ref, dt_ref, cwx_ref, cwb_ref, cbx_ref, cbb_ref, dtb_ref, alog_ref,
               dsk_ref, nw_ref, o_ref, xpx_scr, xpb_scr, st_scr):
    L = SSM_CHUNK
    rows = pl.ds(pl.multiple_of(u * L, L), L)
    pad = SSM_PAD
    npair = SSM_HEADS // 2
    ppg = npair // SSM_GROUPS

    @pl.when((pl.program_id(1) == 0) & (u == 0))
    def _():
        xpx_scr[0:pad, :] = jnp.zeros((pad, xpx_scr.shape[1]), BF16)
        xpb_scr[0:pad, :] = jnp.zeros((pad, xpb_scr.shape[1]), BF16)
        st_scr[...] = jnp.zeros(st_scr.shape, F32)

    nshift = SSM_CONV - 1
    srow = lax.broadcasted_iota(jnp.int32, (nshift * L, pad + L), 0)
    scol = lax.broadcasted_iota(jnp.int32, (nshift * L, pad + L), 1)
    shift = jnp.where(scol == (srow & (L - 1)) + jnp.right_shift(srow, int(np.log2(L))) + (pad - nshift),
                      1.0, 0.0).astype(BF16)

    def conv_silu(x_ref, w_ref, b_ref, scr):
        x = x_ref[rows, :]
        scr[pad:pad + L, :] = x
        sh = _dot(shift, scr[...])
        acc = b_ref[...] + w_ref[nshift:nshift + 1, :] * x.astype(F32)
        for k in range(nshift):
            acc = acc + w_ref[k:k + 1, :] * sh[k * L:(k + 1) * L]
        scr[0:pad, :] = scr[L:L + pad, :]
        return _silu(acc)

    xs = conv_silu(xs_ref, cwx_ref, cbx_ref, xpx_scr)
    bcm = conv_silu(bc_ref, cwb_ref, cbb_ref, xpb_scr).astype(BF16)
    xs_b = xs.astype(BF16)

    dtv = dt_ref[rows, :] + dtb_ref[...]
    dt = jnp.maximum(dtv, 0.0) + jnp.log(1.0 + jnp.exp(-jnp.abs(dtv)))
    a = -jnp.exp(alog_ref[...]) * LOG2E
    row = lax.broadcasted_iota(jnp.int32, (L, L), 0)
    col = lax.broadcasted_iota(jnp.int32, (L, L), 1)
    causal = col <= row
    a_cum = _dot_split3(jnp.where(causal, 1.0, 0.0).astype(BF16), dt * a)
    a_cumT = a_cum.T
    dtT = dt.T
    a_last = a_cum[L - 1:L, :]
    a_lastT = a_cumT[:, L - 1:L]
    wT = dtT * jnp.exp2(a_lastT - a_cumT)
    arowT = a_cumT - jnp.log2(dtT)
    e_last = jnp.exp2(a_last)
    lo = lax.broadcasted_iota(jnp.int32, (1, LANES), 1) < SSM_HD

    ys = []
    for g in range(SSM_GROUPS):
        Bg = bcm[:, g * SSM_STATE:(g + 1) * SSM_STATE]
        Cg = bcm[:, (SSM_GROUPS + g) * SSM_STATE:(SSM_GROUPS + g + 1) * SSM_STATE]
        cb = _dot_nt(Cg, Bg)
        BgT = Bg.astype(F32).T
        for jp in range(ppg):
            pj = g * ppg + jp
            h0, h1 = 2 * pj, 2 * pj + 1
            xp = xs_b[:, pj * LANES:(pj + 1) * LANES]

            def head_mats(h):
                ecol = jnp.broadcast_to(a_cum[:, h:h + 1], (L, L))
                gm = (cb * jnp.exp2(jnp.where(causal, ecol - arowT[h:h + 1, :], NEG))).astype(BF16)
                bw = (BgT * wT[h:h + 1, :]).astype(BF16)
                return gm, bw, jnp.exp2(ecol)

            g0, b0, e0 = head_mats(h0)
            g1, b1, e1 = head_mats(h1)
            x2 = jnp.concatenate([jnp.where(lo, xp, jnp.zeros_like(xp)), jnp.where(lo, jnp.zeros_like(xp), xp)],
                                 axis=0)
            y_d = _dot(jnp.concatenate([g0, g1], axis=1), x2)
            st_prev = st_scr[pj]
            y_o = _dot(Cg, st_prev.astype(BF16)) * jnp.where(lo, e0, e1)
            st_new = _dot(jnp.concatenate([b0, b1], axis=1), x2)
            cdec = jnp.where(lo, jnp.broadcast_to(e_last[:, h0:h0 + 1], (1, LANES)),
                             jnp.broadcast_to(e_last[:, h1:h1 + 1], (1, LANES)))
            st_scr[pj] = st_prev * cdec + st_new
            sl = slice(pj * LANES, (pj + 1) * LANES)
            ys.append(y_d + y_o + dsk_ref[:, sl] * xs[:, sl])
    y = jnp.concatenate(ys, axis=1)
    y = y * _silu(z_ref[rows, :].astype(F32))
    gw = SSM_INNER // SSM_GROUPS
    for g in range(SSM_GROUPS):
        yg = y[:, g * gw:(g + 1) * gw]
        ms = jnp.mean(yg * yg, axis=-1, keepdims=True)
        o_ref[rows, g * gw:(g + 1) * gw] = (yg * lax.rsqrt(ms + EPS) * nw_ref[:, g * gw:(g + 1) * gw]).astype(BF16)


def _ssd(zxm, dt, cwx, cwb, cbx, cbb, dtb, alog, dsk, nw, bsz, seq):
    L = SSM_CHUNK
    chunks = SSD_CHUNKS_PER_STEP
    rows = chunks * L
    nstep = seq // rows
    assert seq % rows == 0
    nbc = 2 * SSM_GROUPS * SSM_STATE
    full = lambda a: pl.BlockSpec(a.shape, lambda b, c: (0,) * a.ndim)
    rowblk = lambda width, idx: pl.BlockSpec((rows, width), lambda b, c: (b * nstep + c, idx))
    return pl.pallas_call(
        functools.partial(_ssd_kernel, chunks=chunks),
        out_shape=jax.ShapeDtypeStruct((bsz * seq, SSM_INNER), BF16),
        grid=(bsz, nstep),
        in_specs=[rowblk(SSM_INNER, 0), rowblk(SSM_INNER, 2), rowblk(nbc, 6), rowblk(LANES, 0),
                  full(cwx), full(cwb), full(cbx), full(cbb), full(dtb), full(alog), full(dsk), full(nw)],
        out_specs=pl.BlockSpec((rows, SSM_INNER), lambda b, c: (b * nstep + c, 0)),
        scratch_shapes=[pltpu.VMEM((L + SSM_PAD, SSM_INNER), BF16),
                        pltpu.VMEM((L + SSM_PAD, nbc), BF16),
                        pltpu.VMEM((SSM_HEADS // 2, SSM_STATE, LANES), F32)],
        compiler_params=_cparams(("parallel", "arbitrary")),
        name="ssd",
    )(zxm, zxm, zxm, dt, cwx, cwb, cbx, cbb, dtb, alog, dsk, nw)


def _outproj_kernel(oT_ref, os_ref, ga_ref, gs_ref, x_ref, mod_ref, n2_ref, wa_ref, ws_ref, wo_ref, o_ref, h_ref):
    oT = jnp.concatenate([oT_ref[0, j] for j in range(oT_ref.shape[1])], axis=1)
    ya = _dot_tn(oT, wa_ref[...])
    ys = _dot(os_ref[...], ws_ref[...])
    m = ga_ref[...].astype(F32) * ya + gs_ref[...].astype(F32) * ys
    r = _dot(m.astype(BF16), wo_ref[...])
    mod = mod_ref[0]
    x1 = x_ref[...] + mod[2:3] * r
    o_ref[...] = x1
    h_ref[...] = _norm_mod(x1, n2_ref[...], mod[4:5], mod[3:4]).astype(BF16)


def _outproj(oT, ossm, zxm, x2d, mod, n2, wa, ws, wo, bsz, seq):
    d = x2d.shape[1]
    tm = 512
    tpb = seq // tm
    full = lambda a: pl.BlockSpec(a.shape, lambda i: (0,) * a.ndim)
    return pl.pallas_call(
        _outproj_kernel,
        out_shape=(jax.ShapeDtypeStruct((bsz * seq, d), F32), jax.ShapeDtypeStruct((bsz * seq, d), BF16)),
        grid=(bsz * tpb,),
        in_specs=[pl.BlockSpec((1, tm // TQ, N_HEADS * HD, TQ), lambda i: (i // tpb, i % tpb, 0, 0)),
                  pl.BlockSpec((tm, SSM_INNER), lambda i: (i, 0)),
                  pl.BlockSpec((tm, d), lambda i: (i, 2)),
                  pl.BlockSpec((tm, d), lambda i: (i, 3)),
                  pl.BlockSpec((tm, d), lambda i: (i, 0)),
                  pl.BlockSpec((1, 6, d), lambda i: (i // tpb, 0, 0)),
                  full(n2), full(wa), full(ws), full(wo)],
        out_specs=(pl.BlockSpec((tm, d), lambda i: (i, 0)), pl.BlockSpec((tm, d), lambda i: (i, 0))),
        compiler_params=_cparams(("parallel",)),
        name="outproj",
    )(oT, ossm, zxm, zxm, x2d, mod, n2, wa, ws, wo)


def _ffn_kernel(x_ref, h_ref, mod_ref, nf_ref, wg_ref, wu_ref, wd_ref, o_ref, *, final_norm):
    h = h_ref[...]
    a = _silu(_dot(h, wg_ref[...])) * _dot(h, wu_ref[...])
    r = _dot(a.astype(BF16), wd_ref[...])
    x2 = x_ref[...] + mod_ref[0][5:6] * r
    if final_norm:
        ms = jnp.mean(x2 * x2, axis=-1, keepdims=True)
        x2 = x2 * lax.rsqrt(ms + EPS) * nf_ref[...]
    o_ref[...] = x2


def _ffn(x1, h2, mod, nf, wg, wu, wd, final_norm, bsz, seq):
    d = x1.shape[1]
    tm = 512
    tpb = seq // tm
    const = lambda a: pl.BlockSpec(a.shape, lambda i: (0,) * a.ndim, pipeline_mode=pl.Buffered(1))
    return pl.pallas_call(
        functools.partial(_ffn_kernel, final_norm=final_norm),
        out_shape=jax.ShapeDtypeStruct((bsz * seq, d), F32),
        grid=(bsz * tpb,),
        in_specs=[pl.BlockSpec((tm, d), lambda i: (i, 0)),
                  pl.BlockSpec((tm, d), lambda i: (i, 0)),
                  pl.BlockSpec((1, 6, d), lambda i: (i // tpb, 0, 0)),
                  const(nf), const(wg), const(wu), const(wd)],
        out_specs=pl.BlockSpec((tm, d), lambda i: (i, 0)),
        compiler_params=_cparams(("parallel",)),
        name="ffn",
    )(x1, h2, mod, nf, wg, wu, wd)


def _rotary_tables(seq):
    inv_freq = jnp.asarray(np.power(ROPE_THETA, -np.arange(0, 2 * ROT_HALF, 2) / (2 * ROT_HALF)).astype(np.float32))
    ang = jnp.arange(seq).astype(F32)[:, None] * inv_freq[None, :]
    cos, sin = jnp.cos(ang), jnp.sin(ang)
    ones = jnp.ones((seq, HD - 2 * ROT_HALF), F32)
    zeros8 = jnp.zeros((seq, ROT_HALF), F32)
    zeros = jnp.zeros((seq, HD - 2 * ROT_HALF), F32)
    cosK = jnp.concatenate([cos, cos, ones], axis=1)
    sinA = jnp.concatenate([-sin, zeros8, zeros], axis=1)
    sinB = jnp.concatenate([zeros8, sin, zeros], axis=1)
    rep = LANES // HD
    tile = lambda t: jnp.concatenate([t] * rep, axis=1)
    return cos.T, sin.T, tile(cosK), tile(sinA), tile(sinB)


def _selection_block_map_T(seq):
    nc = (seq - CMP_BLOCK) // CMP_STRIDE + 1
    nb = seq // SLC_BLOCK
    cs = np.arange(nc) * CMP_STRIDE
    bs = np.arange(nb) * SLC_BLOCK
    ov = np.minimum(cs[:, None] + CMP_BLOCK, bs[None, :] + SLC_BLOCK) - np.maximum(cs[:, None], bs[None, :])
    bmap = (np.clip(ov, 0, None) / CMP_BLOCK).astype(np.float32)
    out = np.zeros((nb, seq // CMP_STRIDE), np.float32)
    out[:, :nc] = bmap.T
    return jnp.asarray(out)


def kernel(x, c, w_ada, b_ada, norm1_w, w_in, cmp_pe_k, cmp_w1_k, cmp_w2_k, cmp_pe_v, cmp_w1_v, cmp_w2_v,
           conv_w, conv_b, dt_bias, a_log, d_skip, ssm_norm_w, w_attn_out, w_ssm_out, w_o, norm2_w,
           w_gate, w_up, w_down, norm_f_w):
    bsz, seq, d = x.shape
    depth = w_ada.shape[0]
    q_w = N_HEADS * HD
    kv_w = N_GROUPS * HD
    xbc_w = SSM_INNER + 2 * SSM_GROUPS * SSM_STATE
    offs = np.cumsum([q_w, 6 * kv_w, 3 * N_HEADS, SSM_INNER, xbc_w, SSM_HEADS])
    tabs = _rotary_tables(seq)
    bmapT = _selection_block_map_T(seq)
    x2d = x.reshape(bsz * seq, d)
    for l in range(depth):
        mod = _ada_mod(c, w_ada[l], b_ada[l]).reshape(bsz, 6, d)
        wi = w_in[l]
        wq = wi[:, :offs[0]]
        wkv = wi[:, offs[0]:offs[1]].reshape(d, 6, N_GROUPS * HD)
        wg = wi[:, offs[1]:offs[2]].reshape(d, N_GROUPS, HPG, 3)
        wz = wi[:, offs[2]:offs[3]]
        wxbc = wi[:, offs[3]:offs[4]]
        wdt = wi[:, offs[4]:offs[5]]
        wmg = wi[:, offs[5]:]
        wqT = wq.T.astype(BF16)
        zcol = jnp.zeros((d, HD), wi.dtype)
        wkn = jnp.concatenate([wkv[:, 2, :HD], zcol, wkv[:, 2, HD:], zcol, wkv[:, 0], wkv[:, 4], wkv[:, 1]],
                              axis=1).astype(BF16)
        wvT = jnp.concatenate([wkv[:, 3], wkv[:, 5]], axis=1).T.astype(BF16)
        wgT = jnp.transpose(wg, (1, 3, 2, 0)).reshape(3 * N_HEADS, d).astype(BF16)
        wdtp = jnp.pad(wdt, ((0, 0), (0, LANES - SSM_HEADS))).astype(BF16)
        wbig = jnp.concatenate([wz, wmg, wxbc], axis=1).astype(BF16)
        nw1 = norm1_w[l].reshape(1, d)

        qT, ksa, kn, kcv, vT, gT, dt, h1 = _inproj_attn(x2d, mod, nw1, wqT, wkn, wvT, wgT, wdtp, tabs, bsz, seq)
        zxm = _inproj_big(h1, wbig, bsz, seq)

        kc = _compress(kcv, 0, cmp_pe_k[l], cmp_w1_k[l], cmp_w2_k[l], False, bsz, seq)
        vcT = _compress(kcv, N_GROUPS, cmp_pe_v[l], cmp_w1_v[l], cmp_w2_v[l], True, bsz, seq)
        oT = _nsa(qT, kc, vcT, ksa, kn, vT, gT, bmapT.astype(BF16), bsz, seq)

        cw = conv_w[l]
        cb = conv_b[l].reshape(1, xbc_w)
        pad_h = lambda v: jnp.pad(v.reshape(1, SSM_HEADS), ((0, 0), (0, LANES - SSM_HEADS)))
        dsk = jnp.repeat(d_skip[l], SSM_HD).reshape(1, SSM_INNER)
        ossm = _ssd(zxm, dt, cw[:, :SSM_INNER], cw[:, SSM_INNER:], cb[:, :SSM_INNER], cb[:, SSM_INNER:],
                    pad_h(dt_bias[l]), pad_h(a_log[l]), dsk, ssm_norm_w[l].reshape(1, SSM_INNER), bsz, seq)

        x1, h2 = _outproj(oT, ossm, zxm, x2d, mod, norm2_w[l].reshape(1, d), w_attn_out[l].astype(BF16),
                          w_ssm_out[l].astype(BF16), w_o[l].astype(BF16), bsz, seq)
        x2d = _ffn(x1, h2, mod, norm_f_w.reshape(1, d), w_gate[l].astype(BF16),
                   w_up[l].astype(BF16), w_down[l].astype(BF16), l == depth - 1, bsz, seq)
    return x2d.reshape(bsz, seq, d)
```
